```python
import math, functools
import jax, jax.numpy as jnp
from jax import lax
import numpy as np

D_MODEL = 1024
BATCH = 16
SEQ = 4096
DEPTH = 1
DEC_BATCH = 32
DEC_SEQ = 64
PAST_LEN = 1024

CHUNK = 64
D_MIX = D_MODEL
HEAD_DIM = 64
N_HEADS = (D_MIX // 2) // HEAD_DIM
N_KV_HEADS = 2
WINDOW = 128
N_PREV_CHUNKS = WINDOW // CHUNK
C_CONV = D_MIX - N_HEADS * HEAD_DIM
CONV_WIDTH = 31
N_GROUPS = 4
EXPERTS_PER_GROUP = 8
N_EXPERTS = N_GROUPS * EXPERTS_PER_GROUP
TOP_K = 2
D_EXPERT = D_MODEL // 2
PLE_DIM = 256
MOE_BLOCK = 128
EPS = 1e-6
Q_W = N_HEADS * HEAD_DIM
KV_W = N_KV_HEADS * HEAD_DIM
D_IN = Q_W + 2 * KV_W + 2 * C_CONV
IN_SPLITS = (Q_W, Q_W + KV_W, Q_W + 2 * KV_W, Q_W + 2 * KV_W + C_CONV)

kernel_name = 'hybrid_swa_conformer_hmoe_stream_step'


def rms_norm(x, g):
    xf = x.astype(jnp.float32)
    y = xf * lax.rsqrt(jnp.mean(xf * xf, axis=-1, keepdims=True) + EPS)
    return (y * g.astype(jnp.float32)).astype(x.dtype)


def layer_norm(x, g, b):
    xf = x.astype(jnp.float32)
    mu = jnp.mean(xf, axis=-1, keepdims=True)
    var = jnp.mean(jnp.square(xf - mu), axis=-1, keepdims=True)
    y = (xf - mu) * lax.rsqrt(var + EPS)
    return (y * g.astype(jnp.float32) + b.astype(jnp.float32)).astype(x.dtype)


def alibi_slopes(n):
    return jnp.array([2.0 ** (-8.0 * (i + 1) / n) for i in range(n)], jnp.float32)


def mixer_project(x, lp):
    B, T, _ = x.shape
    z = rms_norm(x, lp['g_mix']) @ lp['w_in']
    q, k, v, a, b = jnp.split(z, IN_SPLITS, axis=-1)
    q = rms_norm(q.reshape(B, T, N_HEADS, HEAD_DIM), lp['g_q'])
    k = rms_norm(k.reshape(B, T, N_KV_HEADS, HEAD_DIM), lp['g_k'])
    v = v.reshape(B, T, N_KV_HEADS, HEAD_DIM)
    u = a * jax.nn.sigmoid(b)
    return q, k, v, u


def chunk_band(t):
    B, T = t.shape[:2]
    nc = T // CHUNK
    tc = t.reshape(B, nc, CHUNK, t.shape[2], t.shape[3])
    tp = jnp.pad(tc, ((0, 0), (N_PREV_CHUNKS, 0), (0, 0), (0, 0), (0, 0)))
    return jnp.concatenate([tp[:, j:j + nc] for j in range(N_PREV_CHUNKS + 1)], axis=2)


def band_attention(q, k, v, q_pos, k_pos, k_valid, sinks):
    B, NB, QL = q.shape[:3]
    G = N_HEADS // N_KV_HEADS
    qg = q.reshape(B, NB, QL, N_KV_HEADS, G, HEAD_DIM)
    s = jnp.einsum('bnqhgd,bnkhd->bnhgqk', qg, k).astype(jnp.float32) * (HEAD_DIM ** -0.5)
    slopes = alibi_slopes(N_HEADS).reshape(N_KV_HEADS, G)
    dist = jnp.abs(q_pos[:, :, None] - k_pos[:, None, :]).astype(jnp.float32)
    s = s - slopes[None, None, :, :, None, None] * dist[None, :, None, None]
    s = jnp.where(k_valid[None, :, None, None, None, :], s, -jnp.inf)
    sink = jnp.broadcast_to(sinks.astype(jnp.float32).reshape(1, 1, N_KV_HEADS, G, 1, 1), s.shape[:-1] + (1,))
    pr = jax.nn.softmax(jnp.concatenate([s, sink], axis=-1), axis=-1)[..., :-1]
    o = jnp.einsum('bnhgqk,bnkhd->bnqhgd', pr.astype(v.dtype), v)
    return o.reshape(B, NB, QL, N_HEADS * HEAD_DIM)


def causal_depthwise(u_ext, w, b):
    C = u_ext.shape[-1]
    y = lax.conv_general_dilated(u_ext, w[:, None, :].astype(u_ext.dtype), (1,), 'VALID',
                                 dimension_numbers=('NWC', 'WIO', 'NWC'), feature_group_count=C)
    return y + b


def hier_route(h, w_coarse, b_coarse, w_fine, b_fine):
    lc = (h @ w_coarse).astype(jnp.float32) + b_coarse.astype(jnp.float32)
    grp = jnp.argmax(lc, axis=-1).astype(jnp.int32)
    g1 = jnp.take_along_axis(jax.nn.softmax(lc, axis=-1), grp[:, None], axis=-1)
    lf = jnp.einsum('nd,gde->nge', h, w_fine).astype(jnp.float32) + b_fine.astype(jnp.float32)
    lf = jnp.take_along_axis(lf, grp[:, None, None], axis=1)[:, 0]
    tv, ti = lax.top_k(lf, TOP_K)
    wts = g1 * jax.nn.softmax(tv, axis=-1)
    idx = grp[:, None] * EXPERTS_PER_GROUP + ti.astype(jnp.int32)
    return idx, wts


def moe_ffn(h, idx, wts, w_g, w_u, w_d):
    N, D = h.shape
    K = idx.shape[1]
    E = w_g.shape[0]
    A = N * K
    n_blocks = (A + E * (MOE_BLOCK - 1) + MOE_BLOCK - 1) // MOE_BLOCK
    S = n_blocks * MOE_BLOCK
    flat_e = idx.reshape(A)
    flat_t = jnp.repeat(jnp.arange(N, dtype=jnp.int32), K)
    flat_w = wts.reshape(A)
    order = jnp.argsort(flat_e)
    se, st, sw = flat_e[order], flat_t[order], flat_w[order]
    counts = jnp.bincount(flat_e, length=E)
    start = jnp.cumsum(counts) - counts
    padded = (counts + MOE_BLOCK - 1) // MOE_BLOCK * MOE_BLOCK
    pend = jnp.cumsum(padded)
    pstart = pend - padded
    dest = pstart[se] + (jnp.arange(A) - start[se])
    slot_tok = jnp.full((S,), N, jnp.int32).at[dest].set(st)
    slot_w = jnp.zeros((S,), sw.dtype).at[dest].set(sw)
    block_e = jnp.minimum(jnp.searchsorted(pend, jnp.arange(n_blocks) * MOE_BLOCK, side='right'), E - 1)
    h_pad = jnp.concatenate([h, jnp.zeros((1, D), h.dtype)], axis=0)
    xs = h_pad[slot_tok].reshape(n_blocks, MOE_BLOCK, D)

    def expert_block(args):
        xb, e = args
        return (jax.nn.silu(xb @ w_g[e]) * (xb @ w_u[e])) @ w_d[e]

    ys = lax.map(expert_block, (xs, block_e)).reshape(S, D)
    out = jnp.zeros((N + 1, D), h.dtype).at[slot_tok].add(ys * slot_w[:, None].astype(ys.dtype))
    return out[:N]


def finish_layer(x, p, o_attn, c, lp):
    B, T, D = x.shape
    c = jax.nn.silu(layer_norm(c, lp['g_cn'], lp['b_cn'])) @ lp['w_pw'] + lp['b_pw']
    mixed = jnp.concatenate([rms_norm(o_attn, lp['g_oa']), rms_norm(c, lp['g_oc'])], axis=-1) @ lp['w_out']
    x = x + mixed
    h = rms_norm(x, lp['g_ffn']).reshape(B * T, D)
    idx, wts = hier_route(h, lp['w_coarse'], lp['b_coarse'], lp['w_fine'], lp['b_fine'])
    x = x + moe_ffn(h, idx, wts, lp['w_e_gate'], lp['w_e_up'], lp['w_e_down']).reshape(B, T, D)
    gate = jax.nn.sigmoid(rms_norm(x, lp['g_ple']) @ lp['w_ple_gate'])
    return x + gate * (p @ lp['w_ple_proj'])


def setup_inputs(seed: int = 0) -> dict:
    key = jax.random.key(seed)
    ks = jax.random.split(key, 40)
    f32 = jnp.float32

    def nrm(k, shape, scale=1.0):
        return scale * jax.random.normal(k, shape, f32)

    def gain(k, shape):
        return 1.0 + nrm(k, shape, 0.1)

    w_cache = min(WINDOW, PAST_LEN)
    L = DEPTH
    return {
        'x_prompt': nrm(ks[0], (BATCH, SEQ, D_MODEL)),
        'x_sample': nrm(ks[1], (DEC_BATCH, DEC_SEQ, D_MODEL)),
        'p_prompt': nrm(ks[2], (DEPTH, BATCH, SEQ, PLE_DIM)),
        'p_sample': nrm(ks[3], (DEPTH, DEC_BATCH, DEC_SEQ, PLE_DIM)),
        'cache_k': nrm(ks[4], (DEPTH, DEC_BATCH, w_cache, N_KV_HEADS, HEAD_DIM)),
        'cache_v': nrm(ks[5], (DEPTH, DEC_BATCH, w_cache, N_KV_HEADS, HEAD_DIM)),
        'state_conv': nrm(ks[6], (DEPTH, DEC_BATCH, CONV_WIDTH - 1, C_CONV), 0.5),
        'g_mix': gain(ks[7], (L, D_MODEL)),
        'w_in': nrm(ks[8], (L, D_MODEL, D_IN), D_MODEL ** -0.5),
        'g_q': gain(ks[9], (L, HEAD_DIM)),
        'g_k': gain(ks[10], (L, HEAD_DIM)),
        'sinks': nrm(ks[11], (L, N_HEADS), 0.5),
        'w_dw': nrm(ks[12], (L, CONV_WIDTH, C_CONV), CONV_WIDTH ** -0.5),
        'b_dw': nrm(ks[13], (L, C_CONV), 0.02),
        'g_cn': gain(ks[14], (L, C_CONV)),
        'b_cn': nrm(ks[15], (L, C_CONV), 0.02),
        'w_pw': nrm(ks[16], (L, C_CONV, C_CONV), C_CONV ** -0.5),
        'b_pw': nrm(ks[17], (L, C_CONV), 0.02),
        'g_oa': gain(ks[18], (L, Q_W)),
        'g_oc': gain(ks[19], (L, C_CONV)),
        'w_out': nrm(ks[20], (L, D_MIX, D_MODEL), D_MIX ** -0.5),
        'g_ffn': gain(ks[21], (L, D_MODEL)),
        'w_coarse': nrm(ks[22], (L, D_MODEL, N_GROUPS), D_MODEL ** -0.5),
        'b_coarse': nrm(ks[23], (L, N_GROUPS), 0.01),
        'w_fine': nrm(ks[24], (L, N_GROUPS, D_MODEL, EXPERTS_PER_GROUP), D_MODEL ** -0.5),
        'b_fine': nrm(ks[25], (L, N_GROUPS, EXPERTS_PER_GROUP), 0.01),
        'w_e_gate': nrm(ks[26], (L, N_EXPERTS, D_MODEL, D_EXPERT), D_MODEL ** -0.5),
        'w_e_up': nrm(ks[27], (L, N_EXPERTS, D_MODEL, D_EXPERT), D_MODEL ** -0.5),
        'w_e_down': nrm(ks[28], (L, N_EXPERTS, D_EXPERT, D_MODEL), D_EXPERT ** -0.5),
        'g_ple': gain(ks[29], (L, D_MODEL)),
        'w_ple_gate': nrm(ks[30], (L, D_MODEL, D_MODEL), D_MODEL ** -0.5),
        'w_ple_proj': nrm(ks[31], (L, PLE_DIM, D_MODEL), PLE_DIM ** -0.5),
    }


def reference(x_prompt, x_sample, p_prompt, p_sample, cache_k, cache_v, state_conv,
              g_mix, w_in, g_q, g_k, sinks, w_dw, b_dw, g_cn, b_cn, w_pw, b_pw, g_oa, g_oc,
              w_out, g_ffn, w_coarse, b_coarse, w_fine, b_fine, w_e_gate, w_e_up, w_e_down,
              g_ple, w_ple_gate, w_ple_proj):
    yp, ys = x_prompt, x_sample
    nkp, nvp, ncp, nks, nvs, ncs = [], [], [], [], [], []
    for l in range(DEPTH):
        lp = {'g_mix': g_mix[l], 'w_in': w_in[l], 'g_q': g_q[l], 'g_k': g_k[l], 'sinks': sinks[l],
              'w_dw': w_dw[l], 'b_dw': b_dw[l], 'g_cn': g_cn[l], 'b_cn': b_cn[l], 'w_pw': w_pw[l],
              'b_pw': b_pw[l], 'g_oa': g_oa[l], 'g_oc': g_oc[l], 'w_out': w_out[l], 'g_ffn': g_ffn[l],
              'w_coarse': w_coarse[l], 'b_coarse': b_coarse[l], 'w_fine': w_fine[l], 'b_fine': b_fine[l],
              'w_e_gate': w_e_gate[l], 'w_e_up': w_e_up[l], 'w_e_down': w_e_down[l],
              'g_ple': g_ple[l], 'w_ple_gate': w_ple_gate[l], 'w_ple_proj': w_ple_proj[l]}

        B, T, _ = yp.shape
        nc = T // CHUNK
        q, k, v, u = mixer_project(yp, lp)
        band_len = (N_PREV_CHUNKS + 1) * CHUNK
        q_pos = jnp.arange(T, dtype=jnp.int32).reshape(nc, CHUNK)
        k_pos = (jnp.arange(nc, dtype=jnp.int32)[:, None] - N_PREV_CHUNKS) * CHUNK + jnp.arange(band_len, dtype=jnp.int32)[None]
        o = band_attention(q.reshape(B, nc, CHUNK, N_HEADS, HEAD_DIM), chunk_band(k), chunk_band(v),
                           q_pos, k_pos, k_pos >= 0, lp['sinks']).reshape(B, T, Q_W)
        c = causal_depthwise(jnp.pad(u, ((0, 0), (CONV_WIDTH - 1, 0), (0, 0))), lp['w_dw'], lp['b_dw'])
        w_keep = min(WINDOW, T)
        nkp.append(k[:, T - w_keep:])
        nvp.append(v[:, T - w_keep:])
        ncp.append(u[:, T - (CONV_WIDTH - 1):])
        yp = finish_layer(yp, p_prompt[l], o, c, lp)

        DB, DS, _ = ys.shape
        W = cache_k.shape[2]
        q, k, v, u = mixer_project(ys, lp)
        k_all = jnp.concatenate([cache_k[l].astype(k.dtype), k], axis=1)
        v_all = jnp.concatenate([cache_v[l].astype(v.dtype), v], axis=1)
        q_pos = (W + jnp.arange(DS, dtype=jnp.int32))[None]
        k_pos = jnp.arange(W + DS, dtype=jnp.int32)[None]
        o = band_attention(q[:, None], k_all[:, None], v_all[:, None], q_pos, k_pos,
                           jnp.ones((1, W + DS), bool), lp['sinks'])[:, 0]
        u_all = jnp.concatenate([state_conv[l].astype(u.dtype), u], axis=1)
        c = causal_depthwise(u_all, lp['w_dw'], lp['b_dw'])
        nks.append(k_all[:, k_all.shape[1] - W:])
        nvs.append(v_all[:, v_all.shape[1] - W:])
        ncs.append(u_all[:, u_all.shape[1] - (CONV_WIDTH - 1):])
        ys = finish_layer(ys, p_sample[l], o, c, lp)

    return (yp, ys, jnp.stack(nkp), jnp.stack(nvp), jnp.stack(ncp), jnp.stack(nks), jnp.stack(nvs), jnp.stack(ncs))
```

```python
import functools

import jax
import jax.numpy as jnp
from jax import lax
from jax.experimental import pallas as pl
from jax.experimental.pallas import tpu as pltpu

CHUNK = 64
HEAD_DIM = 64
N_HEADS = 8
N_KV_HEADS = 2
GQ = N_HEADS // N_KV_HEADS
WINDOW = 128
BAND = WINDOW + CHUNK
CONV_WIDTH = 31
CONV_HIST = CONV_WIDTH - 1
CONV_PAD = 32
N_GROUPS = 4
EXPERTS_PER_GROUP = 8
N_EXPERTS = N_GROUPS * EXPERTS_PER_GROUP
TOP_K = 2
EPS = 1e-6
LANES = 128
NEG = -1e30

F32 = jnp.float32
BF16 = jnp.bfloat16

VMEM_LIMIT = 56 * 1024 * 1024


def _rms(xf, g):
    return xf * lax.rsqrt(jnp.mean(xf * xf, axis=-1, keepdims=True) + EPS) * g


def _dot(a, b):
    return jnp.dot(a, b, preferred_element_type=F32)


def _dot_nt(a, b):
    return lax.dot_general(a, b, (((1,), (1,)), ((), ())), preferred_element_type=F32)


def _project(x, g_mix, w_in_ref, g_q, g_k):
    xn = _rms(x, g_mix).astype(BF16)
    z = _dot(xn, w_in_ref[...])
    q_w = N_HEADS * HEAD_DIM
    kv_w = N_KV_HEADS * HEAD_DIM
    qs = [_rms(z[:, h * HEAD_DIM:(h + 1) * HEAD_DIM], g_q).astype(BF16) for h in range(N_HEADS)]
    ks = [_rms(z[:, q_w + h * HEAD_DIM:q_w + (h + 1) * HEAD_DIM], g_k) for h in range(N_KV_HEADS)]
    vs = [z[:, q_w + kv_w + h * HEAD_DIM:q_w + kv_w + (h + 1) * HEAD_DIM] for h in range(N_KV_HEADS)]
    c0 = q_w + 2 * kv_w
    c_conv = (z.shape[1] - c0) // 2
    a = z[:, c0:c0 + c_conv]
    b = z[:, c0 + c_conv:]
    u = a * jax.nn.sigmoid(b)
    return qs, ks, vs, u


def _attend(q_stack, k_band, v_band, bias, sink_col, first_valid):
    s = _dot_nt(q_stack, k_band) * (HEAD_DIM ** -0.5) - bias
    if first_valid is not None:
        col = lax.broadcasted_iota(jnp.int32, s.shape, 1)
        s = jnp.where(col >= first_valid, s, NEG)
    m = jnp.maximum(jnp.max(s, axis=-1, keepdims=True), sink_col)
    e = jnp.exp(s - m)
    denom = jnp.sum(e, axis=-1, keepdims=True) + jnp.exp(sink_col - m)
    o = _dot(e.astype(BF16), v_band)
    return o * (1.0 / denom)


def _sink_col(sinks_ref, kvh):
    return jnp.concatenate(
        [jnp.full((CHUNK, 1), sinks_ref[kvh * GQ + g], F32) for g in range(GQ)], axis=0)


def _conv_rows(ux_ref, row0, rows, w_dw_ref, b_dw):
    acc = jnp.broadcast_to(b_dw, (rows, b_dw.shape[-1]))
    for j in range(CONV_WIDTH):
        acc = acc + w_dw_ref[j:j + 1, :] * ux_ref[pl.ds(row0 + j, rows), :]
    return acc


def _finish(x, oa, conv, refs):
    (g_cn, b_cn, w_pw, b_pw, g_oa, g_oc, w_out, g_ffn, w_r, b_r) = refs
    mu = jnp.mean(conv, axis=-1, keepdims=True)
    cen = conv - mu
    var = jnp.mean(cen * cen, axis=-1, keepdims=True)
    ln = cen * lax.rsqrt(var + EPS) * g_cn[...] + b_cn[...]
    act = (ln * jax.nn.sigmoid(ln)).astype(BF16)
    c = _dot(act, w_pw[...]) + b_pw[...]
    half = oa.shape[-1]
    mixed = (_dot(_rms(oa, g_oa[...]).astype(BF16), w_out[0:half, :])
             + _dot(_rms(c, g_oc[...]).astype(BF16), w_out[half:, :]))
    x1 = x + mixed
    h = _rms(x1, g_ffn[...])
    logits = jnp.dot(h, w_r[...], preferred_element_type=F32, precision=lax.Precision.HIGHEST) + b_r[...]
    return x1, h, logits


def _route(logits, cnt_ref):
    rows = logits.shape[0]
    lane = lax.broadcasted_iota(jnp.int32, logits.shape, 1).astype(F32)
    big = float(LANES)
    ninf = -jnp.inf
    lc = jnp.where(lane < N_GROUPS, logits, ninf)
    mc = jnp.max(lc, axis=-1, keepdims=True)
    grp = jnp.min(jnp.where(lc == mc, lane, big), axis=-1, keepdims=True)
    g1 = 1.0 / jnp.sum(jnp.exp(lc - mc), axis=-1, keepdims=True)
    lo = N_GROUPS + grp * EXPERTS_PER_GROUP
    lf = jnp.where((lane >= lo) & (lane < lo + EXPERTS_PER_GROUP), logits, ninf)
    t1 = jnp.max(lf, axis=-1, keepdims=True)
    i1 = jnp.min(jnp.where(lf == t1, lane, big), axis=-1, keepdims=True)
    lf2 = jnp.where(lane == i1, ninf, lf)
    t2 = jnp.max(lf2, axis=-1, keepdims=True)
    i2 = jnp.min(jnp.where(lf2 == t2, lane, big), axis=-1, keepdims=True)
    e2x = jnp.exp(t2 - t1)
    inv = 1.0 / (1.0 + e2x)
    w1 = g1 * inv
    w2 = g1 * (e2x * inv)
    e1 = i1 - N_GROUPS
    e2 = i2 - N_GROUPS
    oh1 = (lane == e1).astype(F32)
    oh2 = (lane == e2).astype(F32)
    oh = oh1 + oh2
    ri = lax.broadcasted_iota(jnp.int32, (rows, rows), 0)
    ci = lax.broadcasted_iota(jnp.int32, (rows, rows), 1)
    tri = (ci < ri).astype(BF16)
    tot = _dot(tri, oh.astype(BF16)) + cnt_ref[...]
    r1 = jnp.sum(oh1 * tot, axis=-1, keepdims=True)
    r2 = jnp.sum(oh2 * tot, axis=-1, keepdims=True)
    cnt_ref[...] = cnt_ref[...] + jnp.sum(oh, axis=0, keepdims=True)
    rec = jnp.where(lane == 0, e1, 0.0)
    rec = jnp.where(lane == 1, e2, rec)
    rec = jnp.where(lane == 2, w1, rec)
    rec = jnp.where(lane == 3, w2, rec)
    rec = jnp.where(lane == 4, r1, rec)
    rec = jnp.where(lane == 5, r2, rec)
    return rec


def _prompt_mixer_kernel(sinks_ref, x_ref, g_mix, w_in, g_q, g_k, bias_ref, w_dw, b_dw,
                         g_cn, b_cn, w_pw, b_pw, g_oa, g_oc, w_out, g_ffn, w_r, b_r,
                         x1_ref, h_ref, rec_ref, cnt_out, nk_ref, nv_ref, nc_ref,
                         kx, vx, ux, oa_s, cnt_s, *, tt):
    b = pl.program_id(0)
    t = pl.program_id(1)
    n_chunks = tt // CHUNK

    @pl.when((b == 0) & (t == 0))
    def _():
        cnt_s[...] = jnp.zeros_like(cnt_s)

    @pl.when(t == 0)
    def _():
        kx[:, 0:WINDOW, :] = jnp.zeros((N_KV_HEADS, WINDOW, HEAD_DIM), BF16)
        vx[:, 0:WINDOW, :] = jnp.zeros((N_KV_HEADS, WINDOW, HEAD_DIM), BF16)
        ux[0:CONV_PAD, :] = jnp.zeros((CONV_PAD, ux.shape[1]), F32)

    x = x_ref[0]
    qs, ks, vs, u = _project(x, g_mix[...], w_in, g_q[...], g_k[...])
    for h in range(N_KV_HEADS):
        kx[h, WINDOW:WINDOW + tt, :] = ks[h].astype(BF16)
        vx[h, WINDOW:WINDOW + tt, :] = vs[h].astype(BF16)
    ux[CONV_PAD:CONV_PAD + tt, :] = u

    for c in range(n_chunks):
        gc = t * n_chunks + c
        first_valid = WINDOW - CHUNK * jnp.minimum(gc, WINDOW // CHUNK)
        for h in range(N_KV_HEADS):
            q_stack = jnp.concatenate(
                [qs[h * GQ + g][c * CHUNK:(c + 1) * CHUNK, :] for g in range(GQ)], axis=0)
            o = _attend(q_stack, kx[h, c * CHUNK:c * CHUNK + BAND, :], vx[h, c * CHUNK:c * CHUNK + BAND, :],
                        bias_ref[h], _sink_col(sinks_ref, h), first_valid)
            for g in range(GQ):
                hd = h * GQ + g
                oa_s[c * CHUNK:(c + 1) * CHUNK, hd * HEAD_DIM:(hd + 1) * HEAD_DIM] = o[g * CHUNK:(g + 1) * CHUNK, :]

    conv = _conv_rows(ux, CONV_PAD - CONV_HIST, tt, w_dw, b_dw[...])
    x1, hh, logits = _finish(x, oa_s[...], conv,
                             (g_cn, b_cn, w_pw, b_pw, g_oa, g_oc, w_out, g_ffn, w_r, b_r))
    x1_ref[0] = x1
    h_ref[0] = hh
    rec_ref[0] = _route(logits, cnt_s)
    cnt_out[...] = cnt_s[...]

    nk_ref[0] = jnp.concatenate([ks[h][tt - WINDOW:, :] for h in range(N_KV_HEADS)], axis=1)
    nv_ref[0] = jnp.concatenate([vs[h][tt - WINDOW:, :] for h in range(N_KV_HEADS)], axis=1)
    nc_ref[0] = u[tt - CONV_HIST:, :]

    for h in range(N_KV_HEADS):
        kx[h, 0:WINDOW, :] = kx[h, tt:tt + WINDOW, :]
        vx[h, 0:WINDOW, :] = vx[h, tt:tt + WINDOW, :]
    ux[0:CONV_PAD, :] = ux[tt:tt + CONV_PAD, :]


def _sample_mixer_kernel(sinks_ref, x_ref, ck_ref, cv_ref, sc_ref, g_mix, w_in, g_q, g_k, bias_ref, w_dw, b_dw,
                         g_cn, b_cn, w_pw, b_pw, g_oa, g_oc, w_out, g_ffn, w_r, b_r,
                         x1_ref, h_ref, rec_ref, cnt_out, nk_ref, nv_ref, nc_ref,
                         kx, vx, ux, oa_s, conv_s, cnt_s, *, nb):
    i = pl.program_id(0)

    @pl.when(i == 0)
    def _():
        cnt_s[...] = jnp.zeros_like(cnt_s)

    rows = nb * CHUNK
    x = x_ref[...].reshape(rows, x_ref.shape[-1])
    qs, ks, vs, u = _project(x, g_mix[...], w_in, g_q[...], g_k[...])
    for j in range(nb):
        r0 = j * CHUNK
        ck = ck_ref[j]
        cv = cv_ref[j]
        for h in range(N_KV_HEADS):
            kx[h, 0:WINDOW, :] = ck[:, h * HEAD_DIM:(h + 1) * HEAD_DIM].astype(BF16)
            vx[h, 0:WINDOW, :] = cv[:, h * HEAD_DIM:(h + 1) * HEAD_DIM].astype(BF16)
            kx[h, WINDOW:BAND, :] = ks[h][r0:r0 + CHUNK, :].astype(BF16)
            vx[h, WINDOW:BAND, :] = vs[h][r0:r0 + CHUNK, :].astype(BF16)
        ux[0:CONV_PAD, :] = jnp.zeros((CONV_PAD, ux.shape[1]), F32)
        ux[CONV_PAD - CONV_HIST:CONV_PAD, :] = sc_ref[j]
        ux[CONV_PAD:CONV_PAD + CHUNK, :] = u[r0:r0 + CHUNK, :]
        for h in range(N_KV_HEADS):
            q_stack = jnp.concatenate([qs[h * GQ + g][r0:r0 + CHUNK, :] for g in range(GQ)], axis=0)
            o = _attend(q_stack, kx[h], vx[h], bias_ref[h], _sink_col(sinks_ref, h), None)
            for g in range(GQ):
                hd = h * GQ + g
                oa_s[r0:r0 + CHUNK, hd * HEAD_DIM:(hd + 1) * HEAD_DIM] = o[g * CHUNK:(g + 1) * CHUNK, :]
        conv_s[r0:r0 + CHUNK, :] = _conv_rows(ux, CONV_PAD - CONV_HIST, CHUNK, w_dw, b_dw[...])
        k_new = jnp.concatenate([ks[h][r0:r0 + CHUNK, :] for h in range(N_KV_HEADS)], axis=1)
        v_new = jnp.concatenate([vs[h][r0:r0 + CHUNK, :] for h in range(N_KV_HEADS)], axis=1)
        nk_ref[j] = jnp.concatenate([ck[CHUNK:, :], k_new], axis=0)
        nv_ref[j] = jnp.concatenate([cv[CHUNK:, :], v_new], axis=0)
        nc_ref[j] = ux[CONV_PAD + CHUNK - CONV_HIST:CONV_PAD + CHUNK, :]

    x1, hh, logits = _finish(x, oa_s[...], conv_s[...],
                             (g_cn, b_cn, w_pw, b_pw, g_oa, g_oc, w_out, g_ffn, w_r, b_r))
    x1_ref[...] = x1.reshape(x1_ref.shape)
    h_ref[...] = hh.reshape(h_ref.shape)
    rec_ref[...] = _route(logits, cnt_s).reshape(rec_ref.shape)
    cnt_out[...] = cnt_s[...]


def _full(shape):
    nd = len(shape)
    return pl.BlockSpec(shape, lambda *_: (0,) * nd)


def _mixer_weight_specs(wts):
    return [_full(w.shape) for w in wts]


def _prompt_mixer(x, sinks, wts, tt):
    B, T, D = x.shape
    c_conv = wts[5].shape[-1]
    nt = T // tt
    grid_spec = pltpu.PrefetchScalarGridSpec(
        num_scalar_prefetch=1,
        grid=(B, nt),
        in_specs=[pl.BlockSpec((1, tt, D), lambda b, t, s: (b, t, 0))] + _mixer_weight_specs(wts),
        out_specs=[
            pl.BlockSpec((1, tt, D), lambda b, t, s: (b, t, 0)),
            pl.BlockSpec((1, tt, D), lambda b, t, s: (b, t, 0)),
            pl.BlockSpec((1, tt, LANES), lambda b, t, s: (b, t, 0)),
            pl.BlockSpec((1, LANES), lambda b, t, s: (0, 0)),
            pl.BlockSpec((1, WINDOW, N_KV_HEADS * HEAD_DIM), lambda b, t, s: (b, 0, 0)),
            pl.BlockSpec((1, WINDOW, N_KV_HEADS * HEAD_DIM), lambda b, t, s: (b, 0, 0)),
            pl.BlockSpec((1, CONV_HIST, c_conv), lambda b, t, s: (b, 0, 0)),
        ],
        scratch_shapes=[
            pltpu.VMEM((N_KV_HEADS, WINDOW + tt, HEAD_DIM), BF16),
            pltpu.VMEM((N_KV_HEADS, WINDOW + tt, HEAD_DIM), BF16),
            pltpu.VMEM((CONV_PAD + tt, c_conv), F32),
            pltpu.VMEM((tt, N_HEADS * HEAD_DIM), F32),
            pltpu.VMEM((1, LANES), F32),
        ],
    )
    return pl.pallas_call(
        functools.partial(_prompt_mixer_kernel, tt=tt),
        grid_spec=grid_spec,
        out_shape=[
            jax.ShapeDtypeStruct((B, T, D), F32),
            jax.ShapeDtypeStruct((B, T, D), F32),
            jax.ShapeDtypeStruct((B, T, LANES), F32),
            jax.ShapeDtypeStruct((1, LANES), F32),
            jax.ShapeDtypeStruct((B, WINDOW, N_KV_HEADS * HEAD_DIM), F32),
            jax.ShapeDtypeStruct((B, WINDOW, N_KV_HEADS * HEAD_DIM), F32),
            jax.ShapeDtypeStruct((B, CONV_HIST, c_conv), F32),
        ],
        compiler_params=pltpu.CompilerParams(
            dimension_semantics=("arbitrary", "arbitrary"), vmem_limit_bytes=VMEM_LIMIT),
        name="prompt_mixer",
    )(sinks, x, *wts)


def _sample_mixer(x, ck, cv, sc, sinks, wts, nb):
    B, T, D = x.shape
    assert T == CHUNK and B % nb == 0
    c_conv = wts[5].shape[-1]
    kvw = N_KV_HEADS * HEAD_DIM
    grid_spec = pltpu.PrefetchScalarGridSpec(
        num_scalar_prefetch=1,
        grid=(B // nb,),
        in_specs=[
            pl.BlockSpec((nb, T, D), lambda i, s: (i, 0, 0)),
            pl.BlockSpec((nb, WINDOW, kvw), lambda i, s: (i, 0, 0)),
            pl.BlockSpec((nb, WINDOW, kvw), lambda i, s: (i, 0, 0)),
            pl.BlockSpec((nb, CONV_HIST, c_conv), lambda i, s: (i, 0, 0)),
        ] + _mixer_weight_specs(wts),
        out_specs=[
            pl.BlockSpec((nb, T, D), lambda i, s: (i, 0, 0)),
            pl.BlockSpec((nb, T, D), lambda i, s: (i, 0, 0)),
            pl.BlockSpec((nb, T, LANES), lambda i, s: (i, 0, 0)),
            pl.BlockSpec((1, LANES), lambda i, s: (0, 0)),
            pl.BlockSpec((nb, WINDOW, kvw), lambda i, s: (i, 0, 0)),
            pl.BlockSpec((nb, WINDOW, kvw), lambda i, s: (i, 0, 0)),
            pl.BlockSpec((nb, CONV_HIST, c_conv), lambda i, s: (i, 0, 0)),
        ],
        scratch_shapes=[
            pltpu.VMEM((N_KV_HEADS, BAND, HEAD_DIM), BF16),
            pltpu.VMEM((N_KV_HEADS, BAND, HEAD_DIM), BF16),
            pltpu.VMEM((CONV_PAD + CHUNK, c_conv), F32),
            pltpu.VMEM((nb * CHUNK, N_HEADS * HEAD_DIM), F32),
            pltpu.VMEM((nb * CHUNK, c_conv), F32),
            pltpu.VMEM((1, LANES), F32),
        ],
    )
    return pl.pallas_call(
        functools.partial(_sample_mixer_kernel, nb=nb),
        grid_spec=grid_spec,
        out_shape=[
            jax.ShapeDtypeStruct((B, T, D), F32),
            jax.ShapeDtypeStruct((B, T, D), F32),
            jax.ShapeDtypeStruct((B, T, LANES), F32),
            jax.ShapeDtypeStruct((1, LANES), F32),
            jax.ShapeDtypeStruct((B, WINDOW, kvw), F32),
            jax.ShapeDtypeStruct((B, WINDOW, kvw), F32),
            jax.ShapeDtypeStruct((B, CONV_HIST, c_conv), F32),
        ],
        compiler_params=pltpu.CompilerParams(
            dimension_semantics=("arbitrary",), vmem_limit_bytes=VMEM_LIMIT),
        name="sample_mixer",
    )(sinks, x, ck, cv, sc, *wts)


def _dispatch_kernel(dest_ref, h_ref, xs_in_ref, xs_ref, sem, *, td):
    del xs_in_ref

    def row_copy(r, k):
        return pltpu.make_async_copy(h_ref.at[pl.ds(r, 1)], xs_ref.at[pl.ds(dest_ref[0, 0, TOP_K * r + k], 1)], sem)

    def start(r, carry):
        for k in range(TOP_K):
            row_copy(r, k).start()
        return carry

    def wait(r, carry):
        for k in range(TOP_K):
            row_copy(r, k).wait()
        return carry

    lax.fori_loop(0, td, start, 0)
    lax.fori_loop(0, td, wait, 0)


def _dispatch(h2d, dest, n_slots, td):
    N, D = h2d.shape
    nt = N // td
    dest3 = dest.reshape(nt, 1, TOP_K * td)
    return pl.pallas_call(
        functools.partial(_dispatch_kernel, td=td),
        grid=(nt,),
        in_specs=[
            pl.BlockSpec((1, 1, TOP_K * td), lambda i: (i, 0, 0), memory_space=pltpu.SMEM),
            pl.BlockSpec((td, D), lambda i: (i, 0)),
            pl.BlockSpec(memory_space=pl.ANY),
        ],
        out_specs=pl.BlockSpec(memory_space=pl.ANY),
        out_shape=jax.ShapeDtypeStruct((n_slots, D), F32),
        scratch_shapes=[pltpu.SemaphoreType.DMA(())],
        input_output_aliases={2: 0},
        compiler_params=pltpu.CompilerParams(
            dimension_semantics=("arbitrary",), vmem_limit_bytes=VMEM_LIMIT),
        name="moe_dispatch",
    )(dest3, h2d, jnp.zeros((n_slots, D), F32))


def _expert_kernel(block_e_ref, n_used_ref, x_ref, wg_ref, wu_ref, wd_ref, y_ref):
    i = pl.program_id(0)

    @pl.when(i < n_used_ref[0])
    def _():
        xb = x_ref[...].astype(BF16)
        g = _dot(xb, wg_ref[0])
        u = _dot(xb, wu_ref[0])
        a = (g * jax.nn.sigmoid(g) * u).astype(BF16)
        y_ref[...] = _dot(a, wd_ref[0])

    @pl.when(i >= n_used_ref[0])
    def _():
        y_ref[...] = jnp.zeros_like(y_ref)


def _experts(xs, block_e, n_used, wg, wu, wd, blk):
    S, D = xs.shape
    n_blocks = S // blk
    de = wg.shape[-1]
    grid_spec = pltpu.PrefetchScalarGridSpec(
        num_scalar_prefetch=2,
        grid=(n_blocks,),
        in_specs=[
            pl.BlockSpec((blk, D), lambda i, be, nu: (i, 0)),
            pl.BlockSpec((1, D, de), lambda i, be, nu: (be[i], 0, 0)),
            pl.BlockSpec((1, D, de), lambda i, be, nu: (be[i], 0, 0)),
            pl.BlockSpec((1, de, D), lambda i, be, nu: (be[i], 0, 0)),
        ],
        out_specs=pl.BlockSpec((blk, D), lambda i, be, nu: (i, 0)),
    )
    return pl.pallas_call(
        _expert_kernel,
        grid_spec=grid_spec,
        out_shape=jax.ShapeDtypeStruct((S, D), F32),
        compiler_params=pltpu.CompilerParams(
            dimension_semantics=("arbitrary",), vmem_limit_bytes=VMEM_LIMIT),
        name="moe_experts",
    )(block_e, n_used, xs, wg, wu, wd)


def _final_kernel(dest_ref, x1_ref, rec_ref, p_ref, g_ple, w_gate, w_proj, ys_ref, y_ref, buf, sem, *, tf):
    def row_copy(r, k):
        return pltpu.make_async_copy(ys_ref.at[pl.ds(dest_ref[0, 0, TOP_K * r + k], 1)], buf.at[k, pl.ds(r, 1)], sem)

    def start(r, carry):
        for k in range(TOP_K):
            row_copy(r, k).start()
        return carry

    def wait(r, carry):
        for k in range(TOP_K):
            row_copy(r, k).wait()
        return carry

    lax.fori_loop(0, tf, start, 0)
    lax.fori_loop(0, tf, wait, 0)

    rec = rec_ref[...]
    w1 = rec[:, 2:3]
    w2 = rec[:, 3:4]
    x2 = x1_ref[...] + (w1 * buf[0] + w2 * buf[1])
    gate = jax.nn.sigmoid(_dot(_rms(x2, g_ple[...]).astype(BF16), w_gate[...]))
    y_ref[...] = x2 + gate * _dot(p_ref[...].astype(BF16), w_proj[...])


def _final(x1, rec, p, dest, ys, g_ple, w_gate, w_proj, tf):
    N, D = x1.shape
    nt = N // tf
    dest3 = dest.reshape(nt, 1, TOP_K * tf)
    return pl.pallas_call(
        functools.partial(_final_kernel, tf=tf),
        grid=(nt,),
        in_specs=[
            pl.BlockSpec((1, 1, TOP_K * tf), lambda i: (i, 0, 0), memory_space=pltpu.SMEM),
            pl.BlockSpec((tf, D), lambda i: (i, 0)),
            pl.BlockSpec((tf, LANES), lambda i: (i, 0)),
            pl.BlockSpec((tf, p.shape[-1]), lambda i: (i, 0)),
            _full(g_ple.shape), _full(w_gate.shape), _full(w_proj.shape),
            pl.BlockSpec(memory_space=pl.ANY),
        ],
        out_specs=pl.BlockSpec((tf, D), lambda i: (i, 0)),
        out_shape=jax.ShapeDtypeStruct((N, D), F32),
        scratch_shapes=[pltpu.VMEM((TOP_K, tf, D), F32), pltpu.SemaphoreType.DMA(())],
        compiler_params=pltpu.CompilerParams(
            dimension_semantics=("arbitrary",), vmem_limit_bytes=VMEM_LIMIT),
        name="moe_combine_ple",
    )(dest3, x1, rec, p, g_ple, w_gate, w_proj, ys)


def _moe_layout(rec, count_offset, counts_total, blk):
    e = rec[:, 0:TOP_K].astype(jnp.int32)
    rank = rec[:, 4:4 + TOP_K].astype(jnp.int32)
    counts = counts_total.astype(jnp.int32)
    padded = (counts + blk - 1) // blk * blk
    pend = jnp.cumsum(padded)
    pstart = pend - padded
    base = pstart + count_offset.astype(jnp.int32)
    dest = base[e] + rank
    return dest.reshape(-1), pend


def _moe_group(h2d, x1, rec, p2d, counts, wg, wu, wd, g_ple, w_gate, w_proj, blk, tile):
    N, D = h2d.shape
    A = N * TOP_K
    n_blocks = (A + N_EXPERTS * (blk - 1)) // blk
    cnt = counts[0, :N_EXPERTS]
    dest, pend = _moe_layout(rec, jnp.zeros_like(cnt), cnt, blk)
    block_e = jnp.minimum(
        jnp.searchsorted(pend, jnp.arange(n_blocks, dtype=jnp.int32) * blk, side='right'), N_EXPERTS - 1
    ).astype(jnp.int32)
    n_used = (pend[-1:] // blk).astype(jnp.int32)
    xs = _dispatch(h2d, dest, n_blocks * blk, tile)
    ys = _experts(xs, block_e, n_used, wg, wu, wd, blk)
    return _final(x1, rec, p2d, dest, ys, g_ple, w_gate, w_proj, tile)


def _alibi_bias():
    slopes = jnp.array([2.0 ** (-8.0 * (i + 1) / N_HEADS) for i in range(N_HEADS)], F32)
    qpos = WINDOW + jnp.arange(CHUNK, dtype=jnp.int32)
    kpos = jnp.arange(BAND, dtype=jnp.int32)
    dist = jnp.abs(qpos[:, None] - kpos[None, :]).astype(F32)
    bias = slopes[:, None, None] * dist[None]
    return bias.reshape(N_KV_HEADS, GQ * CHUNK, BAND)


def kernel(x_prompt, x_sample, p_prompt, p_sample, cache_k, cache_v, state_conv, g_mix, w_in, g_q, g_k, sinks,
           w_dw, b_dw, g_cn, b_cn, w_pw, b_pw, g_oa, g_oc, w_out, g_ffn, w_coarse, b_coarse, w_fine, b_fine,
           w_e_gate, w_e_up, w_e_down, g_ple, w_ple_gate, w_ple_proj):
    depth = g_mix.shape[0]
    assert depth == 1
    l = 0
    B, T, D = x_prompt.shape
    DB, DS, _ = x_sample.shape
    W = cache_k.shape[2]
    assert W == WINDOW and DS == CHUNK
    kvw = N_KV_HEADS * HEAD_DIM

    row = lambda a: a[l].reshape(1, -1)
    w_r = jnp.concatenate(
        [w_coarse[l], jnp.transpose(w_fine[l], (1, 0, 2)).reshape(D, N_EXPERTS),
         jnp.zeros((D, LANES - N_GROUPS - N_EXPERTS), F32)], axis=1)
    b_r = jnp.concatenate(
        [b_coarse[l], b_fine[l].reshape(-1), jnp.zeros((LANES - N_GROUPS - N_EXPERTS,), F32)]).reshape(1, LANES)
    wts = (row(g_mix), w_in[l].astype(BF16), row(g_q), row(g_k), _alibi_bias(), w_dw[l], row(b_dw),
           row(g_cn), row(b_cn), w_pw[l].astype(BF16), row(b_pw), row(g_oa), row(g_oc),
           w_out[l].astype(BF16), row(g_ffn), w_r, b_r)
    sk = sinks[l]

    wg = w_e_gate[l].astype(BF16)
    wu = w_e_up[l].astype(BF16)
    wd = w_e_down[l].astype(BF16)
    gp = row(g_ple)
    w_gate = w_ple_gate[l].astype(BF16)
    w_proj = w_ple_proj[l].astype(BF16)

    tt = 256 if T % 256 == 0 else T
    x1p, hp, recp, cntp, nkp, nvp, ncp = _prompt_mixer(x_prompt, sk, wts, tt)
    Np = B * T
    yp = _moe_group(hp.reshape(Np, D), x1p.reshape(Np, D), recp.reshape(Np, LANES),
                    p_prompt[l].reshape(Np, -1), cntp, wg, wu, wd, gp, w_gate, w_proj,
                    blk=512, tile=256 if Np % 256 == 0 else Np)

    nb = 4 if DB % 4 == 0 else 1
    x1s, hs, recs, cnts, nks, nvs, ncs = _sample_mixer(
        x_sample, cache_k[l].reshape(DB, W, kvw), cache_v[l].reshape(DB, W, kvw), state_conv[l], sk, wts, nb)
    Ns = DB * DS
    ys = _moe_group(hs.reshape(Ns, D), x1s.reshape(Ns, D), recs.reshape(Ns, LANES),
                    p_sample[l].reshape(Ns, -1), cnts, wg, wu, wd, gp, w_gate, w_proj,
                    blk=128, tile=256 if Ns % 256 == 0 else Ns)

    kv5 = lambda a, n: a.reshape(1, n, WINDOW, N_KV_HEADS, HEAD_DIM)
    return (yp.reshape(B, T, D), ys.reshape(DB, DS, D),
            kv5(nkp, B), kv5(nvp, B), ncp[None],
            kv5(nks, DB), kv5(nvs, DB), ncs[None])
```

```python
import functools

import jax
import jax.numpy as jnp
from jax import lax
from jax.experimental import pallas as pl
from jax.experimental.pallas import tpu as pltpu

CHUNK = 64
HEAD_DIM = 64
N_HEADS = 8
N_KV_HEADS = 2
GQ = N_HEADS // N_KV_HEADS
Q_W = N_HEADS * HEAD_DIM
KV_W = N_KV_HEADS * HEAD_DIM
WINDOW = 128
BAND = WINDOW + CHUNK
BAND_PAD = 256
CONV_WIDTH = 31
CONV_HIST = CONV_WIDTH - 1
CONV_PAD = 32
SUBLANES = 8
N_GROUPS = 4
EXPERTS_PER_GROUP = 8
N_EXPERTS = N_GROUPS * EXPERTS_PER_GROUP
TOP_K = 2
EPS = 1e-6
LANES = 128
REC_W = 8
MASKED = 1e30
DMA_UNROLL = 8

F32 = jnp.float32
BF16 = jnp.bfloat16
U32 = jnp.uint32
I32 = jnp.int32

VMEM_LIMIT = 56 * 1024 * 1024


def _tiles(n_prompt_seq, n_tokens):
    tt = 256 if n_prompt_seq % 256 == 0 else n_prompt_seq
    tok = 256 if n_tokens % 256 == 0 else n_tokens
    return tt, tok


def _rms(xf, g):
    return xf * lax.rsqrt(jnp.mean(xf * xf, axis=-1, keepdims=True) + EPS) * g


def _dot(a, b):
    return jnp.dot(a, b, preferred_element_type=F32)


def _dot_nt(a, b):
    return lax.dot_general(a, b, (((1,), (1,)), ((), ())), preferred_element_type=F32)


def _project(x, g_mix, w_in_ref, bd_ref, gain_qk):
    xn = _rms(x, g_mix).astype(BF16)
    z = _dot(xn, w_in_ref[...])
    qk = z[:, 0:Q_W + KV_W]
    sq = (qk * qk).astype(BF16)
    bd = bd_ref[...]
    two = 2 * LANES
    ms = jnp.concatenate(
        [_dot(sq[:, 0:two], bd), _dot(sq[:, two:2 * two], bd), _dot(sq[:, 2 * two:], bd[0:KV_W, 0:KV_W])], axis=1)
    qkn = qk * lax.rsqrt(ms + EPS) * gain_qk
    v = z[:, Q_W + KV_W:Q_W + 2 * KV_W]
    c0 = Q_W + 2 * KV_W
    c_conv = (z.shape[1] - c0) // 2
    u = z[:, c0:c0 + c_conv] * jax.nn.sigmoid(z[:, c0 + c_conv:])
    return qkn, v, u


def _attend_chunk(q_groups, k_band, v_band, bias):
    lane = lax.broadcasted_iota(I32, (CHUNK, LANES), 1)
    low = lane < HEAD_DIM
    zero = jnp.zeros((CHUNK, LANES), F32)
    q_all = jnp.concatenate(
        [jnp.where(low, qg, zero) for qg in q_groups] + [jnp.where(low, zero, qg) for qg in q_groups],
        axis=0).astype(BF16)
    pad = jnp.zeros((BAND_PAD - BAND, LANES), BF16)
    s = _dot_nt(q_all, jnp.concatenate([k_band, pad], axis=0)) - bias
    m = jnp.max(s, axis=-1, keepdims=True)
    e = jnp.exp(s - m)
    denom = jnp.sum(e, axis=-1, keepdims=True)
    o = _dot(e.astype(BF16), jnp.concatenate([v_band, pad], axis=0)) * (1.0 / denom)
    half = GQ * CHUNK
    return [jnp.where(low, o[g * CHUNK:(g + 1) * CHUNK, :], o[half + g * CHUNK:half + (g + 1) * CHUNK, :])
            for g in range(GQ)]


def _shift_copies(ush, rows):
    for n in range(1, SUBLANES):
        ush[n, 0:rows, :] = ush[0, n:n + rows, :]


def _conv_rows(ush, r0, rows, w_dw_ref, b_dw):
    acc = jnp.broadcast_to(b_dw, (rows, b_dw.shape[-1]))
    for j in range(CONV_WIDTH):
        off = CONV_PAD - CONV_HIST + j
        acc = acc + w_dw_ref[j:j + 1, :] * ush[off % SUBLANES, pl.ds(r0 + off - off % SUBLANES, rows), :]
    return acc


def _finish(x, oa, conv, refs):
    (g_cn, b_cn, w_pw, b_pw, g_oa, g_oc, w_out, g_ffn, w_r2, w_rh, b_r) = refs
    mu = jnp.mean(conv, axis=-1, keepdims=True)
    cen = conv - mu
    var = jnp.mean(cen * cen, axis=-1, keepdims=True)
    ln = cen * lax.rsqrt(var + EPS) * g_cn[...] + b_cn[...]
    act = (ln * jax.nn.sigmoid(ln)).astype(BF16)
    c = _dot(act, w_pw[...]) + b_pw[...]
    half = oa.shape[-1]
    mixed = (_dot(_rms(oa, g_oa[...]).astype(BF16), w_out[0:half, :])
             + _dot(_rms(c, g_oc[...]).astype(BF16), w_out[half:, :]))
    x1 = x + mixed
    h = _rms(x1, g_ffn[...])
    h_hi = h.astype(BF16)
    h_lo = (h - h_hi.astype(F32)).astype(BF16)
    a = _dot(h_hi, w_r2[...])
    logits = a[:, 0:LANES] + a[:, LANES:] + _dot(h_lo, w_rh[...]) + b_r[...]
    return x1, h, logits


def _pack_rows(h):
    half = h.shape[-1] // 2
    hb = h.astype(BF16).astype(F32)
    lo = lax.bitcast_convert_type(hb[:, 0:half], U32)
    hi = lax.bitcast_convert_type(hb[:, half:], U32)
    return (lo >> 16) | (hi & jnp.uint32(0xFFFF0000))


def _unpack_rows(w):
    lo = lax.bitcast_convert_type(w << 16, F32).astype(BF16)
    hi = lax.bitcast_convert_type(w & jnp.uint32(0xFFFF0000), F32).astype(BF16)
    return lo, hi


def _route(logits, cnt_ref):
    rows = logits.shape[0]
    lane = lax.broadcasted_iota(I32, logits.shape, 1).astype(F32)
    big = float(LANES)
    ninf = -jnp.inf
    lc = jnp.where(lane < N_GROUPS, logits, ninf)
    mc = jnp.max(lc, axis=-1, keepdims=True)
    grp = jnp.min(jnp.where(lc == mc, lane, big), axis=-1, keepdims=True)
    g1 = 1.0 / jnp.sum(jnp.exp(lc - mc), axis=-1, keepdims=True)
    lo = N_GROUPS + grp * EXPERTS_PER_GROUP
    lf = jnp.where((lane >= lo) & (lane < lo + EXPERTS_PER_GROUP), logits, ninf)
    t1 = jnp.max(lf, axis=-1, keepdims=True)
    i1 = jnp.min(jnp.where(lf == t1, lane, big), axis=-1, keepdims=True)
    lf2 = jnp.where(lane == i1, ninf, lf)
    t2 = jnp.max(lf2, axis=-1, keepdims=True)
    i2 = jnp.min(jnp.where(lf2 == t2, lane, big), axis=-1, keepdims=True)
    e2x = jnp.exp(t2 - t1)
    inv = 1.0 / (1.0 + e2x)
    w1 = g1 * inv
    w2 = g1 * (e2x * inv)
    e1 = i1 - N_GROUPS
    e2 = i2 - N_GROUPS
    oh1 = (lane == e1).astype(F32)
    oh2 = (lane == e2).astype(F32)
    oh = oh1 + oh2
    ri = lax.broadcasted_iota(I32, (rows, rows), 0)
    ci = lax.broadcasted_iota(I32, (rows, rows), 1)
    tri = (ci < ri).astype(BF16)
    tot = _dot(tri, oh.astype(BF16)) + cnt_ref[...]
    r1 = jnp.sum(oh1 * tot, axis=-1, keepdims=True)
    r2 = jnp.sum(oh2 * tot, axis=-1, keepdims=True)
    cnt_ref[...] = cnt_ref[...] + jnp.sum(oh, axis=0, keepdims=True)
    rec = jnp.where(lane == 0, e1, 0.0)
    rec = jnp.where(lane == 1, e2, rec)
    rec = jnp.where(lane == 2, w1, rec)
    rec = jnp.where(lane == 3, w2, rec)
    rec = jnp.where(lane == 4, r1, rec)
    rec = jnp.where(lane == 5, r2, rec)
    return rec[:, 0:REC_W]


CONV_BLOCK = 32


def _prompt_mixer_kernel(x_ref, g_mix, w_in, bd, gain_qk, bias_ref, w_dw, b_dw,
                         g_cn, b_cn, w_pw, b_pw, g_oa, g_oc, w_out, g_ffn, w_r2, w_rh, b_r,
                         x1_ref, h_ref, rec_ref, cnt_out, nk_ref, nv_ref, nc_ref,
                         kx, vx, ush, oa_s, conv_s, cnt_s, *, tt):
    b = pl.program_id(0)
    t = pl.program_id(1)
    n_chunks = tt // CHUNK

    @pl.when((b == 0) & (t == 0))
    def _():
        cnt_s[...] = jnp.zeros_like(cnt_s)

    @pl.when(t == 0)
    def _():
        kx[0:WINDOW, :] = jnp.zeros((WINDOW, LANES), BF16)
        vx[0:WINDOW, :] = jnp.zeros((WINDOW, LANES), BF16)
        ush[0, 0:CONV_PAD, :] = jnp.zeros((CONV_PAD, ush.shape[-1]), F32)

    x = x_ref[0]
    qkn, v, u = _project(x, g_mix[...], w_in, bd, gain_qk[...])
    k = qkn[:, Q_W:]
    kx[WINDOW:WINDOW + tt, :] = k.astype(BF16)
    vx[WINDOW:WINDOW + tt, :] = v.astype(BF16)
    ush[0, CONV_PAD:CONV_PAD + tt, :] = u
    _shift_copies(ush, tt + CONV_PAD - SUBLANES)

    for c in range(n_chunks):
        variant = jnp.minimum(t * n_chunks + c, WINDOW // CHUNK)
        rows = slice(c * CHUNK, (c + 1) * CHUNK)
        q_groups = [qkn[rows, m * LANES:(m + 1) * LANES] for m in range(GQ)]
        o_groups = _attend_chunk(q_groups, kx[c * CHUNK:c * CHUNK + BAND, :], vx[c * CHUNK:c * CHUNK + BAND, :],
                                 bias_ref[variant])
        for m in range(GQ):
            oa_s[rows, m * LANES:(m + 1) * LANES] = o_groups[m]

    for rb in range(tt // CONV_BLOCK):
        conv_s[rb * CONV_BLOCK:(rb + 1) * CONV_BLOCK, :] = _conv_rows(ush, rb * CONV_BLOCK, CONV_BLOCK, w_dw, b_dw[...])

    x1, hh, logits = _finish(x, oa_s[...], conv_s[...],
                             (g_cn, b_cn, w_pw, b_pw, g_oa, g_oc, w_out, g_ffn, w_r2, w_rh, b_r))
    x1_ref[0] = x1
    h_ref[0] = _pack_rows(hh)
    rec_ref[0] = _route(logits, cnt_s)
    cnt_out[...] = cnt_s[...]

    nk_ref[0] = k[tt - WINDOW:, :]
    nv_ref[0] = v[tt - WINDOW:, :]
    nc_ref[0] = u[tt - CONV_HIST:, :]

    kx[0:WINDOW, :] = kx[tt:tt + WINDOW, :]
    vx[0:WINDOW, :] = vx[tt:tt + WINDOW, :]
    ush[0, 0:CONV_PAD, :] = ush[0, tt:tt + CONV_PAD, :]


def _sample_mixer_kernel(x_ref, ck_ref, cv_ref, sc_ref, g_mix, w_in, bd, gain_qk, bias_ref, w_dw, b_dw,
                         g_cn, b_cn, w_pw, b_pw, g_oa, g_oc, w_out, g_ffn, w_r2, w_rh, b_r,
                         x1_ref, h_ref, rec_ref, cnt_out, nk_ref, nv_ref, nc_ref,
                         ush, oa_s, conv_s, cnt_s, *, nb):
    i = pl.program_id(0)

    @pl.when(i == 0)
    def _():
        cnt_s[...] = jnp.zeros_like(cnt_s)

    rows_all = nb * CHUNK
    x = x_ref[...].reshape(rows_all, x_ref.shape[-1])
    qkn, v, u = _project(x, g_mix[...], w_in, bd, gain_qk[...])
    k = qkn[:, Q_W:]
    for j in range(nb):
        rows = slice(j * CHUNK, (j + 1) * CHUNK)
        ck = ck_ref[j]
        cv = cv_ref[j]
        k_band = jnp.concatenate([ck, k[rows, :]], axis=0)
        v_band = jnp.concatenate([cv, v[rows, :]], axis=0)
        q_groups = [qkn[rows, m * LANES:(m + 1) * LANES] for m in range(GQ)]
        o_groups = _attend_chunk(q_groups, k_band.astype(BF16), v_band.astype(BF16), bias_ref[0])
        for m in range(GQ):
            oa_s[rows, m * LANES:(m + 1) * LANES] = o_groups[m]
        ush[0, 0:CONV_PAD, :] = jnp.zeros((CONV_PAD, ush.shape[-1]), F32)
        ush[0, CONV_PAD - CONV_HIST:CONV_PAD, :] = sc_ref[j]
        ush[0, CONV_PAD:CONV_PAD + CHUNK, :] = u[rows, :]
        _shift_copies(ush, CHUNK + CONV_PAD - SUBLANES)
        for rb in range(CHUNK // CONV_BLOCK):
            r0 = rb * CONV_BLOCK
            conv_s[j * CHUNK + r0:j * CHUNK + r0 + CONV_BLOCK, :] = _conv_rows(ush, r0, CONV_BLOCK, w_dw, b_dw[...])
        nk_ref[j] = k_band[CHUNK:, :]
        nv_ref[j] = v_band[CHUNK:, :]
        nc_ref[j] = ush[0, CONV_PAD + CHUNK - CONV_HIST:CONV_PAD + CHUNK, :]

    x1, hh, logits = _finish(x, oa_s[...], conv_s[...],
                             (g_cn, b_cn, w_pw, b_pw, g_oa, g_oc, w_out, g_ffn, w_r2, w_rh, b_r))
    x1_ref[...] = x1.reshape(x1_ref.shape)
    h_ref[...] = _pack_rows(hh).reshape(h_ref.shape)
    rec_ref[...] = _route(logits, cnt_s).reshape(rec_ref.shape)
    cnt_out[...] = cnt_s[...]


def _full(shape):
    nd = len(shape)
    return pl.BlockSpec(shape, lambda *_: (0,) * nd)


def _prompt_mixer(x, wts, tt):
    B, T, D = x.shape
    c_conv = wts[5].shape[-1]
    nt = T // tt
    tok = lambda last: pl.BlockSpec((1, tt, last), lambda b, t: (b, t, 0))
    per_batch = lambda rows, last: pl.BlockSpec((1, rows, last), lambda b, t: (b, 0, 0))
    return pl.pallas_call(
        functools.partial(_prompt_mixer_kernel, tt=tt),
        grid=(B, nt),
        in_specs=[tok(D)] + [_full(w.shape) for w in wts],
        out_specs=[tok(D), tok(D // 2), tok(REC_W), pl.BlockSpec((1, LANES), lambda b, t: (0, 0)),
                   per_batch(WINDOW, KV_W), per_batch(WINDOW, KV_W), per_batch(CONV_HIST, c_conv)],
        out_shape=[
            jax.ShapeDtypeStruct((B, T, D), F32),
            jax.ShapeDtypeStruct((B, T, D // 2), U32),
            jax.ShapeDtypeStruct((B, T, REC_W), F32),
            jax.ShapeDtypeStruct((1, LANES), F32),
            jax.ShapeDtypeStruct((B, WINDOW, KV_W), F32),
            jax.ShapeDtypeStruct((B, WINDOW, KV_W), F32),
            jax.ShapeDtypeStruct((B, CONV_HIST, c_conv), F32),
        ],
        scratch_shapes=[
            pltpu.VMEM((WINDOW + tt, LANES), BF16),
            pltpu.VMEM((WINDOW + tt, LANES), BF16),
            pltpu.VMEM((SUBLANES, CONV_PAD + tt, c_conv), F32),
            pltpu.VMEM((tt, Q_W), F32),
            pltpu.VMEM((tt, c_conv), F32),
            pltpu.VMEM((1, LANES), F32),
        ],
        compiler_params=pltpu.CompilerParams(
            dimension_semantics=("arbitrary", "arbitrary"), vmem_limit_bytes=VMEM_LIMIT),
        name="prompt_mixer",
    )(x, *wts)


def _sample_mixer(x, ck, cv, sc, wts, nb):
    B, T, D = x.shape
    assert T == CHUNK and B % nb == 0
    c_conv = wts[5].shape[-1]
    blk3 = lambda rows, last: pl.BlockSpec((nb, rows, last), lambda i: (i, 0, 0))
    return pl.pallas_call(
        functools.partial(_sample_mixer_kernel, nb=nb),
        grid=(B // nb,),
        in_specs=[blk3(T, D), blk3(WINDOW, KV_W), blk3(WINDOW, KV_W), blk3(CONV_HIST, c_conv)]
                 + [_full(w.shape) for w in wts],
        out_specs=[blk3(T, D), blk3(T, D // 2), blk3(T, REC_W), pl.BlockSpec((1, LANES), lambda i: (0, 0)),
                   blk3(WINDOW, KV_W), blk3(WINDOW, KV_W), blk3(CONV_HIST, c_conv)],
        out_shape=[
            jax.ShapeDtypeStruct((B, T, D), F32),
            jax.ShapeDtypeStruct((B, T, D // 2), U32),
            jax.ShapeDtypeStruct((B, T, REC_W), F32),
            jax.ShapeDtypeStruct((1, LANES), F32),
            jax.ShapeDtypeStruct((B, WINDOW, KV_W), F32),
            jax.ShapeDtypeStruct((B, WINDOW, KV_W), F32),
            jax.ShapeDtypeStruct((B, CONV_HIST, c_conv), F32),
        ],
        scratch_shapes=[
            pltpu.VMEM((SUBLANES, CONV_PAD + CHUNK, c_conv), F32),
            pltpu.VMEM((nb * CHUNK, Q_W), F32),
            pltpu.VMEM((nb * CHUNK, c_conv), F32),
            pltpu.VMEM((1, LANES), F32),
        ],
        compiler_params=pltpu.CompilerParams(
            dimension_semantics=("arbitrary",), vmem_limit_bytes=VMEM_LIMIT),
        name="sample_mixer",
    )(x, ck, cv, sc, *wts)


def _dispatch_kernel(meta_ref, dest_ref, h_ref, xs_ref, zbuf, sem, zsem, *, td, blk, n_blocks):
    i = pl.program_id(0)

    def issue(r8, carry):
        for uu in range(DMA_UNROLL):
            r = r8 * DMA_UNROLL + uu
            for kk in range(TOP_K):
                pltpu.make_async_copy(h_ref.at[pl.ds(r, 1)], xs_ref.at[pl.ds(dest_ref[0, 0, TOP_K * r + kk], 1)],
                                      sem).start()
        return carry

    lax.fori_loop(0, td // DMA_UNROLL, issue, 0)

    @pl.when(i == pl.num_programs(0) - 1)
    def _():
        zbuf[...] = jnp.zeros_like(zbuf)

        def pad_expert(e, carry):
            start = meta_ref[e]
            head = meta_ref[N_EXPERTS + e]
            body = meta_ref[2 * N_EXPERTS + e]
            for uu in range(SUBLANES - 1):
                @pl.when(uu < head)
                def _(uu=uu):
                    cp = pltpu.make_async_copy(zbuf.at[pl.ds(0, 1)], xs_ref.at[pl.ds(start + uu, 1)], zsem)
                    cp.start()
                    cp.wait()
            bit = blk // 2
            while bit >= SUBLANES:
                off = pl.multiple_of(start + head + (body // (2 * bit)) * (2 * bit), SUBLANES)

                @pl.when((body // bit) % 2 == 1)
                def _(bit=bit, off=off):
                    cp = pltpu.make_async_copy(zbuf.at[pl.ds(0, bit)], xs_ref.at[pl.ds(off, bit)], zsem)
                    cp.start()
                    cp.wait()
                bit //= 2
            return carry

        lax.fori_loop(0, N_EXPERTS, pad_expert, 0)

        def pad_block(j, carry):
            cp = pltpu.make_async_copy(zbuf, xs_ref.at[pl.ds(pl.multiple_of(j * blk, blk), blk)], zsem)
            cp.start()
            cp.wait()
            return carry

        lax.fori_loop(meta_ref[3 * N_EXPERTS], n_blocks, pad_block, 0)

    for _ in range(TOP_K):
        pltpu.make_async_copy(h_ref, xs_ref.at[pl.ds(0, td)], sem).wait()


def _dispatch(hpk, dest, meta, n_blocks, blk, td):
    N, DW = hpk.shape
    nt = N // td
    dest3 = dest.reshape(nt, 1, TOP_K * td)
    grid_spec = pltpu.PrefetchScalarGridSpec(
        num_scalar_prefetch=1,
        grid=(nt,),
        in_specs=[
            pl.BlockSpec((1, 1, TOP_K * td), lambda i, m: (i, 0, 0), memory_space=pltpu.SMEM),
            pl.BlockSpec((td, DW), lambda i, m: (i, 0)),
        ],
        out_specs=pl.BlockSpec(memory_space=pl.ANY),
        scratch_shapes=[pltpu.VMEM((blk, DW), U32), pltpu.SemaphoreType.DMA(()), pltpu.SemaphoreType.DMA(())],
    )
    return pl.pallas_call(
        functools.partial(_dispatch_kernel, td=td, blk=blk, n_blocks=n_blocks),
        grid_spec=grid_spec,
        out_shape=jax.ShapeDtypeStruct((n_blocks * blk, DW), U32),
        compiler_params=pltpu.CompilerParams(
            dimension_semantics=("arbitrary",), vmem_limit_bytes=VMEM_LIMIT),
        name="moe_dispatch",
    )(meta, dest3, hpk)


def _expert_kernel(block_e_ref, n_used_ref, x_ref, wg_ref, wu_ref, wd_ref, y_ref, wgb, wub, wdb):
    i = pl.program_id(0)
    used = i < n_used_ref[0]
    new_expert = (i == 0) | (block_e_ref[i] != block_e_ref[jnp.maximum(i - 1, 0)])

    @pl.when(used & new_expert)
    def _():
        wgb[...] = wg_ref[0].astype(BF16)
        wub[...] = wu_ref[0].astype(BF16)
        wdb[...] = wd_ref[0].astype(BF16)

    @pl.when(used)
    def _():
        lo, hi = _unpack_rows(x_ref[...])
        half = lo.shape[-1]
        g = _dot(lo, wgb[0:half, :]) + _dot(hi, wgb[half:, :])
        u = _dot(lo, wub[0:half, :]) + _dot(hi, wub[half:, :])
        a = (g * jax.nn.sigmoid(g) * u).astype(BF16)
        y_ref[...] = _dot(a, wdb[...])

    @pl.when(jnp.logical_not(used))
    def _():
        y_ref[...] = jnp.zeros_like(y_ref)


def _experts(xs, block_e, n_used, wg, wu, wd, blk):
    S, DW = xs.shape
    n_blocks = S // blk
    _, D, de = wg.shape
    grid_spec = pltpu.PrefetchScalarGridSpec(
        num_scalar_prefetch=2,
        grid=(n_blocks,),
        in_specs=[
            pl.BlockSpec((blk, DW), lambda i, be, nu: (i, 0)),
            pl.BlockSpec((1, D, de), lambda i, be, nu: (be[i], 0, 0)),
            pl.BlockSpec((1, D, de), lambda i, be, nu: (be[i], 0, 0)),
            pl.BlockSpec((1, de, D), lambda i, be, nu: (be[i], 0, 0)),
        ],
        out_specs=pl.BlockSpec((blk, D), lambda i, be, nu: (i, 0)),
        scratch_shapes=[pltpu.VMEM((D, de), BF16), pltpu.VMEM((D, de), BF16), pltpu.VMEM((de, D), BF16)],
    )
    return pl.pallas_call(
        _expert_kernel,
        grid_spec=grid_spec,
        out_shape=jax.ShapeDtypeStruct((S, D), F32),
        compiler_params=pltpu.CompilerParams(
            dimension_semantics=("arbitrary",), vmem_limit_bytes=VMEM_LIMIT),
        name="moe_experts",
    )(block_e, n_used, xs, wg, wu, wd)


def _final_kernel(dest_cur, dest_nxt, x1_ref, rec_ref, p_ref, g_ple, w_gate, w_proj, ys_ref, y_ref, buf, sem, *, tf):
    i = pl.program_id(0)
    slot = i % 2

    def issue(dref, s):
        def body(r8, carry):
            for uu in range(DMA_UNROLL):
                r = r8 * DMA_UNROLL + uu
                for kk in range(TOP_K):
                    pltpu.make_async_copy(ys_ref.at[pl.ds(dref[0, 0, TOP_K * r + kk], 1)],
                                          buf.at[s, kk, pl.ds(r, 1)], sem.at[s]).start()
            return carry
        lax.fori_loop(0, tf // DMA_UNROLL, body, 0)

    @pl.when(i == 0)
    def _():
        issue(dest_cur, 0)

    @pl.when(i + 1 < pl.num_programs(0))
    def _():
        issue(dest_nxt, 1 - slot)

    for kk in range(TOP_K):
        pltpu.make_async_copy(ys_ref.at[pl.ds(0, tf)], buf.at[slot, kk], sem.at[slot]).wait()

    rec = rec_ref[...]
    x2 = x1_ref[...] + (rec[:, 2:3] * buf[slot, 0] + rec[:, 3:4] * buf[slot, 1])
    gate = jax.nn.sigmoid(_dot(_rms(x2, g_ple[...]).astype(BF16), w_gate[...]))
    y_ref[...] = x2 + gate * _dot(p_ref[...].astype(BF16), w_proj[...])


def _final(x1, rec, p, dest, ys, g_ple, w_gate, w_proj, tf):
    N, D = x1.shape
    nt = N // tf
    dest3 = dest.reshape(nt, 1, TOP_K * tf)
    smem_blk = lambda fn: pl.BlockSpec((1, 1, TOP_K * tf), fn, memory_space=pltpu.SMEM)
    return pl.pallas_call(
        functools.partial(_final_kernel, tf=tf),
        grid=(nt,),
        in_specs=[
            smem_blk(lambda i: (i, 0, 0)),
            smem_blk(lambda i: (jnp.minimum(i + 1, nt - 1), 0, 0)),
            pl.BlockSpec((tf, D), lambda i: (i, 0)),
            pl.BlockSpec((tf, REC_W), lambda i: (i, 0)),
            pl.BlockSpec((tf, p.shape[-1]), lambda i: (i, 0)),
            _full(g_ple.shape), _full(w_gate.shape), _full(w_proj.shape),
            pl.BlockSpec(memory_space=pl.ANY),
        ],
        out_specs=pl.BlockSpec((tf, D), lambda i: (i, 0)),
        out_shape=jax.ShapeDtypeStruct((N, D), F32),
        scratch_shapes=[pltpu.VMEM((2, TOP_K, tf, D), F32), pltpu.SemaphoreType.DMA((2,))],
        compiler_params=pltpu.CompilerParams(
            dimension_semantics=("arbitrary",), vmem_limit_bytes=VMEM_LIMIT),
        name="moe_combine_ple",
    )(dest3, dest3, x1, rec, p, g_ple, w_gate, w_proj, ys)


def _moe_group(hpk, x1, rec, p2d, counts, wg, wu, wd, g_ple, w_gate, w_proj, blk, tile):
    N = hpk.shape[0]
    n_blocks = (N * TOP_K + N_EXPERTS * (blk - 1)) // blk
    cnt = counts[0, :N_EXPERTS].astype(I32)
    padded = (cnt + blk - 1) // blk * blk
    pend = jnp.cumsum(padded)
    pstart = pend - padded
    e = rec[:, 0:TOP_K].astype(I32)
    rank = rec[:, 4:4 + TOP_K].astype(I32)
    onehot = e[:, :, None] == jnp.arange(N_EXPERTS, dtype=I32)[None, None, :]
    dest = (rank + jnp.sum(jnp.where(onehot, pstart[None, None, :], 0), axis=-1)).reshape(-1)
    block_e = jnp.minimum(
        jnp.sum((pend[None, :] <= (jnp.arange(n_blocks, dtype=I32) * blk)[:, None]).astype(I32), axis=1),
        N_EXPERTS - 1).astype(I32)
    n_used = (pend[-1:] // blk).astype(I32)
    pad_start = pstart + cnt
    pad_head = (-pad_start) % SUBLANES
    meta = jnp.concatenate([pad_start, pad_head, padded - cnt - pad_head, n_used]).astype(I32)
    xs = _dispatch(hpk, dest, meta, n_blocks, blk, tile)
    ys = _experts(xs, block_e, n_used, wg, wu, wd, blk)
    return _final(x1, rec, p2d, dest, ys, g_ple, w_gate, w_proj, tile)


def _bias_table(sinks):
    slopes = jnp.array([2.0 ** (-8.0 * (i + 1) / N_HEADS) for i in range(N_HEADS)], F32)
    qpos = WINDOW + jnp.arange(CHUNK, dtype=I32)
    col = jnp.arange(BAND_PAD, dtype=I32)
    dist = jnp.abs(qpos[:, None] - col[None, :]).astype(F32)
    core = slopes[:, None, None] * dist[None]
    sink = jnp.broadcast_to(-sinks.astype(F32)[:, None, None], core.shape)
    table = jnp.where(col == BAND, sink, jnp.where(col < BAND, core, MASKED))
    first_valid = jnp.array([WINDOW, WINDOW - CHUNK, 0], I32)
    table = jnp.where(col[None, None, None, :] < first_valid[:, None, None, None], MASKED, table[None])
    return table.reshape(3, N_HEADS * CHUNK, BAND_PAD)


def kernel(x_prompt, x_sample, p_prompt, p_sample, cache_k, cache_v, state_conv, g_mix, w_in, g_q, g_k, sinks,
           w_dw, b_dw, g_cn, b_cn, w_pw, b_pw, g_oa, g_oc, w_out, g_ffn, w_coarse, b_coarse, w_fine, b_fine,
           w_e_gate, w_e_up, w_e_down, g_ple, w_ple_gate, w_ple_proj):
    assert g_mix.shape[0] == 1
    l = 0
    B, T, D = x_prompt.shape
    DB, DS, _ = x_sample.shape
    assert cache_k.shape[2] == WINDOW and DS == CHUNK

    row = lambda a: a[l].reshape(1, -1)
    perm = jnp.concatenate([jnp.concatenate([jnp.arange(HEAD_DIM) + m * HEAD_DIM,
                                             jnp.arange(HEAD_DIM) + (GQ + m) * HEAD_DIM]) for m in range(GQ)])
    w_in_p = jnp.concatenate([w_in[l][:, perm], w_in[l][:, Q_W:]], axis=1).astype(BF16)
    w_out_p = jnp.concatenate([w_out[l][perm, :], w_out[l][Q_W:, :]], axis=0).astype(BF16)
    g_oa_p = g_oa[l][perm].reshape(1, -1)
    gain_qk = jnp.concatenate([jnp.tile(g_q[l] * (HEAD_DIM ** -0.5), N_HEADS), jnp.tile(g_k[l], N_KV_HEADS)]).reshape(1, -1)
    blk_id = jnp.arange(2 * LANES) // HEAD_DIM
    bd = jnp.where(blk_id[:, None] == blk_id[None, :], 1.0 / HEAD_DIM, 0.0).astype(BF16)
    w_r = jnp.concatenate(
        [w_coarse[l], jnp.transpose(w_fine[l], (1, 0, 2)).reshape(D, N_EXPERTS),
         jnp.zeros((D, LANES - N_GROUPS - N_EXPERTS), F32)], axis=1)
    w_rh = w_r.astype(BF16)
    w_rl = (w_r - w_rh.astype(F32)).astype(BF16)
    w_r2 = jnp.concatenate([w_rh, w_rl], axis=1)
    b_r = jnp.concatenate(
        [b_coarse[l], b_fine[l].reshape(-1), jnp.zeros((LANES - N_GROUPS - N_EXPERTS,), F32)]).reshape(1, LANES)
    bias = _bias_table(sinks[l])

    def mixer_weights(bias_tbl):
        return (row(g_mix), w_in_p, bd, gain_qk, bias_tbl, w_dw[l], row(b_dw),
                row(g_cn), row(b_cn), w_pw[l].astype(BF16), row(b_pw), g_oa_p, row(g_oc),
                w_out_p, row(g_ffn), w_r2, w_rh, b_r)

    wg, wu, wd = w_e_gate[l], w_e_up[l], w_e_down[l]
    gp = row(g_ple)
    w_gate = w_ple_gate[l].astype(BF16)
    w_proj = w_ple_proj[l].astype(BF16)

    Np = B * T
    tt, tile_p = _tiles(T, Np)
    x1p, hp, recp, cntp, nkp, nvp, ncp = _prompt_mixer(x_prompt, mixer_weights(bias), tt)
    yp = _moe_group(hp.reshape(Np, D // 2), x1p.reshape(Np, D), recp.reshape(Np, REC_W),
                    p_prompt[l].reshape(Np, -1), cntp, wg, wu, wd, gp, w_gate, w_proj, blk=512, tile=tile_p)

    Ns = DB * DS
    _, tile_s = _tiles(T, Ns)
    nb = 4 if DB % 4 == 0 else 1
    x1s, hs, recs, cnts, nks, nvs, ncs = _sample_mixer(
        x_sample, cache_k[l].reshape(DB, WINDOW, KV_W), cache_v[l].reshape(DB, WINDOW, KV_W), state_conv[l],
        mixer_weights(bias[2:3]), nb)
    ys = _moe_group(hs.reshape(Ns, D // 2), x1s.reshape(Ns, D), recs.reshape(Ns, REC_W),
                    p_sample[l].reshape(Ns, -1), cnts, wg, wu, wd, gp, w_gate, w_proj, blk=128, tile=tile_s)

    kv5 = lambda a, n: a.reshape(1, n, WINDOW, N_KV_HEADS, HEAD_DIM)
    return (yp.reshape(B, T, D), ys.reshape(DB, DS, D),
            kv5(nkp, B), kv5(nvp, B), ncp[None],
            kv5(nks, DB), kv5(nvs, DB), ncs[None])
```

```python
import functools

import jax
import jax.numpy as jnp
from jax import lax
from jax.experimental import pallas as pl
from jax.experimental.pallas import tpu as pltpu

CHUNK = 64
HEAD_DIM = 64
N_HEADS = 8
N_KV_HEADS = 2
GQ = N_HEADS // N_KV_HEADS
Q_W = N_HEADS * HEAD_DIM
KV_W = N_KV_HEADS * HEAD_DIM
WINDOW = 128
BAND = WINDOW + CHUNK
BAND_PAD = 256
CONV_WIDTH = 31
CONV_HIST = CONV_WIDTH - 1
CONV_PAD = 32
CONV_BLOCK = 32
SUBLANES = 8
LANES = 128
N_GROUPS = 4
EXPERTS_PER_GROUP = 8
N_EXPERTS = N_GROUPS * EXPERTS_PER_GROUP
TOP_K = 2
EXPERT_BLOCK = 512
EPS = 1e-6
REC_W = 8
MASKED = 1e30
DMA_UNROLL = 8

F32 = jnp.float32
BF16 = jnp.bfloat16
U32 = jnp.uint32
I32 = jnp.int32

VMEM_LIMIT = 56 * 1024 * 1024


def _tiles(n_prompt_seq, n_prompt, n_sample):
    tt = 256 if n_prompt_seq % 256 == 0 else n_prompt_seq
    td = next(t for t in (1024, 512, 256, 128, 64) if n_prompt % t == 0 and n_sample % t == 0)
    tf = min(td, 256)
    return tt, td, tf


def _rms(xf, g):
    return xf * lax.rsqrt(jnp.mean(xf * xf, axis=-1, keepdims=True) + EPS) * g


def _dot(a, b):
    return jnp.dot(a, b, preferred_element_type=F32)


def _dot_nt(a, b):
    return lax.dot_general(a, b, (((1,), (1,)), ((), ())), preferred_element_type=F32)


def _store_tile_rows(ref, lead, value):
    rows, width = value.shape
    n = width // LANES
    for c in range(n):
        ref[lead + (pl.ds(c, rows, stride=n), slice(None))] = value[:, c * LANES:(c + 1) * LANES]


def _load_tile_rows(ref, lead, rows, n):
    return jnp.concatenate([ref[lead + (pl.ds(c, rows, stride=n), slice(None))] for c in range(n)], axis=1)


def _project(x, g_mix, w_in_ref, bd_ref, gain_qk):
    xn = _rms(x, g_mix).astype(BF16)
    z = _dot(xn, w_in_ref[...])
    qk = z[:, 0:Q_W + KV_W]
    sq = (qk * qk).astype(BF16)
    bd = bd_ref[...]
    two = 2 * LANES
    ms = jnp.concatenate(
        [_dot(sq[:, 0:two], bd), _dot(sq[:, two:2 * two], bd), _dot(sq[:, 2 * two:], bd[0:KV_W, 0:KV_W])], axis=1)
    qkn = qk * lax.rsqrt(ms + EPS) * gain_qk
    v = z[:, Q_W + KV_W:Q_W + 2 * KV_W]
    c0 = Q_W + 2 * KV_W
    c_conv = (z.shape[1] - c0) // 2
    u = z[:, c0:c0 + c_conv] * jax.nn.sigmoid(z[:, c0 + c_conv:])
    return qkn, v, u


def _attend_chunk(q_groups, k_band, v_band, bias):
    lane = lax.broadcasted_iota(I32, (CHUNK, LANES), 1)
    low = lane < HEAD_DIM
    zero = jnp.zeros((CHUNK, LANES), F32)
    q_all = jnp.concatenate(
        [jnp.where(low, qg, zero) for qg in q_groups] + [jnp.where(low, zero, qg) for qg in q_groups],
        axis=0).astype(BF16)
    pad = jnp.zeros((BAND_PAD - BAND, LANES), BF16)
    s = _dot_nt(q_all, jnp.concatenate([k_band, pad], axis=0)) - bias
    m = jnp.max(s, axis=-1, keepdims=True)
    e = jnp.exp(s - m)
    denom = jnp.sum(e, axis=-1, keepdims=True)
    o = _dot(e.astype(BF16), jnp.concatenate([v_band, pad], axis=0)) * (1.0 / denom)
    half = GQ * CHUNK
    return [jnp.where(low, o[g * CHUNK:(g + 1) * CHUNK, :], o[half + g * CHUNK:half + (g + 1) * CHUNK, :])
            for g in range(GQ)]


def _shift_copies(ush, rows):
    for n in range(1, SUBLANES):
        ush[n, 0:rows, :] = ush[0, n:n + rows, :]


def _conv_rows(ush, r0, rows, w_dw_ref, b_dw):
    ch = b_dw.shape[-1]
    groups = rows // SUBLANES
    acc = jnp.broadcast_to(b_dw, (groups, SUBLANES, ch))
    for j in range(CONV_WIDTH):
        off = CONV_PAD - CONV_HIST + j
        taps = ush[off % SUBLANES, pl.ds(r0 + off - off % SUBLANES, rows), :].reshape(groups, SUBLANES, ch)
        acc = acc + w_dw_ref[j * SUBLANES:(j + 1) * SUBLANES, :][None] * taps
    return acc.reshape(rows, ch)


def _finish(x, oa, conv, refs):
    (g_cn, b_cn, w_pw, b_pw, g_oa, g_oc, w_out, g_ffn, w_r2, w_rh, b_r) = refs
    mu = jnp.mean(conv, axis=-1, keepdims=True)
    cen = conv - mu
    var = jnp.mean(cen * cen, axis=-1, keepdims=True)
    ln = cen * lax.rsqrt(var + EPS) * g_cn[...] + b_cn[...]
    act = (ln * jax.nn.sigmoid(ln)).astype(BF16)
    c = _dot(act, w_pw[...]) + b_pw[...]
    half = oa.shape[-1]
    mixed = (_dot(_rms(oa, g_oa[...]).astype(BF16), w_out[0:half, :])
             + _dot(_rms(c, g_oc[...]).astype(BF16), w_out[half:, :]))
    x1 = x + mixed
    h = _rms(x1, g_ffn[...])
    h_hi = h.astype(BF16)
    h_lo = (h - h_hi.astype(F32)).astype(BF16)
    a = _dot(h_hi, w_r2[...])
    logits = a[:, 0:LANES] + a[:, LANES:] + _dot(h_lo, w_rh[...]) + b_r[...]
    return x1, h, logits


def _pack_rows(h):
    half = h.shape[-1] // 2
    hb = h.astype(BF16).astype(F32)
    lo = lax.bitcast_convert_type(hb[:, 0:half], U32)
    hi = lax.bitcast_convert_type(hb[:, half:], U32)
    return (lo >> 16) | (hi & jnp.uint32(0xFFFF0000))


def _unpack_rows(w):
    lo = lax.bitcast_convert_type(w << 16, F32).astype(BF16)
    hi = lax.bitcast_convert_type(w & jnp.uint32(0xFFFF0000), F32).astype(BF16)
    return lo, hi


def _route(logits, cnt_ref):
    rows = logits.shape[0]
    lane = lax.broadcasted_iota(I32, logits.shape, 1).astype(F32)
    big = float(LANES)
    ninf = -jnp.inf
    lc = jnp.where(lane < N_GROUPS, logits, ninf)
    mc = jnp.max(lc, axis=-1, keepdims=True)
    grp = jnp.min(jnp.where(lc == mc, lane, big), axis=-1, keepdims=True)
    g1 = 1.0 / jnp.sum(jnp.exp(lc - mc), axis=-1, keepdims=True)
    lo = N_GROUPS + grp * EXPERTS_PER_GROUP
    lf = jnp.where((lane >= lo) & (lane < lo + EXPERTS_PER_GROUP), logits, ninf)
    t1 = jnp.max(lf, axis=-1, keepdims=True)
    i1 = jnp.min(jnp.where(lf == t1, lane, big), axis=-1, keepdims=True)
    lf2 = jnp.where(lane == i1, ninf, lf)
    t2 = jnp.max(lf2, axis=-1, keepdims=True)
    i2 = jnp.min(jnp.where(lf2 == t2, lane, big), axis=-1, keepdims=True)
    e2x = jnp.exp(t2 - t1)
    inv = 1.0 / (1.0 + e2x)
    w1 = g1 * inv
    w2 = g1 * (e2x * inv)
    e1 = i1 - N_GROUPS
    e2 = i2 - N_GROUPS
    oh1 = (lane == e1).astype(F32)
    oh2 = (lane == e2).astype(F32)
    oh = oh1 + oh2
    ri = lax.broadcasted_iota(I32, (rows, rows), 0)
    ci = lax.broadcasted_iota(I32, (rows, rows), 1)
    tri = (ci < ri).astype(BF16)
    tot = _dot(tri, oh.astype(BF16)) + cnt_ref[...]
    r1 = jnp.sum(oh1 * tot, axis=-1, keepdims=True)
    r2 = jnp.sum(oh2 * tot, axis=-1, keepdims=True)
    cnt_ref[...] = cnt_ref[...] + jnp.sum(oh, axis=0, keepdims=True)
    rec = jnp.where(lane == 0, e1, 0.0)
    rec = jnp.where(lane == 1, e2, rec)
    rec = jnp.where(lane == 2, w1, rec)
    rec = jnp.where(lane == 3, w2, rec)
    rec = jnp.where(lane == 4, r1, rec)
    rec = jnp.where(lane == 5, r2, rec)
    return rec[:, 0:REC_W]


def _prompt_mixer_kernel(x_ref, g_mix, w_in, bd, gain_qk, bias_ref, w_dw, b_dw,
                         g_cn, b_cn, w_pw, b_pw, g_oa, g_oc, w_out, g_ffn, w_r2, w_rh, b_r,
                         x1_ref, h_ref, rec_ref, cnt_out, nk_ref, nv_ref, nc_ref,
                         kx, vx, ush, oa_s, conv_s, cnt_s, *, tt):
    b = pl.program_id(0)
    t = pl.program_id(1)
    n_chunks = tt // CHUNK

    @pl.when((b == 0) & (t == 0))
    def _():
        cnt_s[...] = jnp.zeros_like(cnt_s)

    @pl.when(t == 0)
    def _():
        kx[0:WINDOW, :] = jnp.zeros((WINDOW, LANES), BF16)
        vx[0:WINDOW, :] = jnp.zeros((WINDOW, LANES), BF16)
        ush[0, 0:CONV_PAD, :] = jnp.zeros((CONV_PAD, ush.shape[-1]), F32)

    x = x_ref[0]
    qkn, v, u = _project(x, g_mix[...], w_in, bd, gain_qk[...])
    k = qkn[:, Q_W:]
    kx[WINDOW:WINDOW + tt, :] = k.astype(BF16)
    vx[WINDOW:WINDOW + tt, :] = v.astype(BF16)
    ush[0, CONV_PAD:CONV_PAD + tt, :] = u
    _shift_copies(ush, tt + CONV_PAD - SUBLANES)

    for c in range(n_chunks):
        variant = jnp.minimum(t * n_chunks + c, WINDOW // CHUNK)
        rows = slice(c * CHUNK, (c + 1) * CHUNK)
        q_groups = [qkn[rows, m * LANES:(m + 1) * LANES] for m in range(GQ)]
        o_groups = _attend_chunk(q_groups, kx[c * CHUNK:c * CHUNK + BAND, :], vx[c * CHUNK:c * CHUNK + BAND, :],
                                 bias_ref[variant])
        for m in range(GQ):
            oa_s[rows, m * LANES:(m + 1) * LANES] = o_groups[m]

    for rb in range(tt // CONV_BLOCK):
        conv_s[rb * CONV_BLOCK:(rb + 1) * CONV_BLOCK, :] = _conv_rows(ush, rb * CONV_BLOCK, CONV_BLOCK, w_dw, b_dw[...])

    x1, hh, logits = _finish(x, oa_s[...], conv_s[...],
                             (g_cn, b_cn, w_pw, b_pw, g_oa, g_oc, w_out, g_ffn, w_r2, w_rh, b_r))
    x1_ref[0] = x1
    _store_tile_rows(h_ref, (0,), _pack_rows(hh))
    rec_ref[0] = _route(logits, cnt_s)
    cnt_out[...] = cnt_s[...]

    nk_ref[0] = k[tt - WINDOW:, :]
    nv_ref[0] = v[tt - WINDOW:, :]
    nc_ref[0] = u[tt - CONV_HIST:, :]

    kx[0:WINDOW, :] = kx[tt:tt + WINDOW, :]
    vx[0:WINDOW, :] = vx[tt:tt + WINDOW, :]
    ush[0, 0:CONV_PAD, :] = ush[0, tt:tt + CONV_PAD, :]


def _sample_mixer_kernel(x_ref, ck_ref, cv_ref, sc_ref, g_mix, w_in, bd, gain_qk, bias_ref, w_dw, b_dw,
                         g_cn, b_cn, w_pw, b_pw, g_oa, g_oc, w_out, g_ffn, w_r2, w_rh, b_r,
                         x1_ref, h_ref, rec_ref, cnt_out, nk_ref, nv_ref, nc_ref,
                         ush, oa_s, conv_s, cnt_s, *, nb):
    i = pl.program_id(0)

    @pl.when(i == 0)
    def _():
        cnt_s[...] = jnp.zeros_like(cnt_s)

    rows_all = nb * CHUNK
    x = x_ref[...].reshape(rows_all, x_ref.shape[-1])
    qkn, v, u = _project(x, g_mix[...], w_in, bd, gain_qk[...])
    k = qkn[:, Q_W:]
    for j in range(nb):
        rows = slice(j * CHUNK, (j + 1) * CHUNK)
        ck = ck_ref[j]
        cv = cv_ref[j]
        k_band = jnp.concatenate([ck, k[rows, :]], axis=0)
        v_band = jnp.concatenate([cv, v[rows, :]], axis=0)
        q_groups = [qkn[rows, m * LANES:(m + 1) * LANES] for m in range(GQ)]
        o_groups = _attend_chunk(q_groups, k_band.astype(BF16), v_band.astype(BF16), bias_ref[0])
        for m in range(GQ):
            oa_s[rows, m * LANES:(m + 1) * LANES] = o_groups[m]
        ush[0, 0:CONV_PAD, :] = jnp.zeros((CONV_PAD, ush.shape[-1]), F32)
        ush[0, CONV_PAD - CONV_HIST:CONV_PAD, :] = sc_ref[j]
        ush[0, CONV_PAD:CONV_PAD + CHUNK, :] = u[rows, :]
        _shift_copies(ush, CHUNK + CONV_PAD - SUBLANES)
        for rb in range(CHUNK // CONV_BLOCK):
            r0 = rb * CONV_BLOCK
            conv_s[j * CHUNK + r0:j * CHUNK + r0 + CONV_BLOCK, :] = _conv_rows(ush, r0, CONV_BLOCK, w_dw, b_dw[...])
        nk_ref[j] = k_band[CHUNK:, :]
        nv_ref[j] = v_band[CHUNK:, :]
        nc_ref[j] = ush[0, CONV_PAD + CHUNK - CONV_HIST:CONV_PAD + CHUNK, :]

    x1, hh, logits = _finish(x, oa_s[...], conv_s[...],
                             (g_cn, b_cn, w_pw, b_pw, g_oa, g_oc, w_out, g_ffn, w_r2, w_rh, b_r))
    x1_ref[...] = x1.reshape(x1_ref.shape)
    packed = _pack_rows(hh)
    for j in range(nb):
        _store_tile_rows(h_ref, (j,), packed[j * CHUNK:(j + 1) * CHUNK, :])
    rec_ref[...] = _route(logits, cnt_s).reshape(rec_ref.shape)
    cnt_out[...] = cnt_s[...]


def _full(shape):
    nd = len(shape)
    return pl.BlockSpec(shape, lambda *_: (0,) * nd)


def _prompt_mixer(x, wts, tt):
    B, T, D = x.shape
    c_conv = wts[6].shape[-1]
    xs_rows = D // 2 // LANES
    nt = T // tt
    tok = lambda rows, last: pl.BlockSpec((1, rows, last), lambda b, t: (b, t, 0))
    per_batch = lambda rows, last: pl.BlockSpec((1, rows, last), lambda b, t: (b, 0, 0))
    return pl.pallas_call(
        functools.partial(_prompt_mixer_kernel, tt=tt),
        grid=(B, nt),
        in_specs=[tok(tt, D)] + [_full(w.shape) for w in wts],
        out_specs=[tok(tt, D), tok(tt * xs_rows, LANES), tok(tt, REC_W), pl.BlockSpec((1, LANES), lambda b, t: (0, 0)),
                   per_batch(WINDOW, KV_W), per_batch(WINDOW, KV_W), per_batch(CONV_HIST, c_conv)],
        out_shape=[
            jax.ShapeDtypeStruct((B, T, D), F32),
            jax.ShapeDtypeStruct((B, T * xs_rows, LANES), U32),
            jax.ShapeDtypeStruct((B, T, REC_W), F32),
            jax.ShapeDtypeStruct((1, LANES), F32),
            jax.ShapeDtypeStruct((B, WINDOW, KV_W), F32),
            jax.ShapeDtypeStruct((B, WINDOW, KV_W), F32),
            jax.ShapeDtypeStruct((B, CONV_HIST, c_conv), F32),
        ],
        scratch_shapes=[
            pltpu.VMEM((WINDOW + tt, LANES), BF16),
            pltpu.VMEM((WINDOW + tt, LANES), BF16),
            pltpu.VMEM((SUBLANES, CONV_PAD + tt, c_conv), F32),
            pltpu.VMEM((tt, Q_W), F32),
            pltpu.VMEM((tt, c_conv), F32),
            pltpu.VMEM((1, LANES), F32),
        ],
        compiler_params=pltpu.CompilerParams(
            dimension_semantics=("arbitrary", "arbitrary"), vmem_limit_bytes=VMEM_LIMIT),
        name="prompt_mixer",
    )(x, *wts)


def _sample_mixer(x, ck, cv, sc, wts, nb):
    B, T, D = x.shape
    assert T == CHUNK and B % nb == 0
    c_conv = wts[6].shape[-1]
    xs_rows = D // 2 // LANES
    blk3 = lambda rows, last: pl.BlockSpec((nb, rows, last), lambda i: (i, 0, 0))
    return pl.pallas_call(
        functools.partial(_sample_mixer_kernel, nb=nb),
        grid=(B // nb,),
        in_specs=[blk3(T, D), blk3(WINDOW, KV_W), blk3(WINDOW, KV_W), blk3(CONV_HIST, c_conv)]
                 + [_full(w.shape) for w in wts],
        out_specs=[blk3(T, D), blk3(T * xs_rows, LANES), blk3(T, REC_W), pl.BlockSpec((1, LANES), lambda i: (0, 0)),
                   blk3(WINDOW, KV_W), blk3(WINDOW, KV_W), blk3(CONV_HIST, c_conv)],
        out_shape=[
            jax.ShapeDtypeStruct((B, T, D), F32),
            jax.ShapeDtypeStruct((B, T * xs_rows, LANES), U32),
            jax.ShapeDtypeStruct((B, T, REC_W), F32),
            jax.ShapeDtypeStruct((1, LANES), F32),
            jax.ShapeDtypeStruct((B, WINDOW, KV_W), F32),
            jax.ShapeDtypeStruct((B, WINDOW, KV_W), F32),
            jax.ShapeDtypeStruct((B, CONV_HIST, c_conv), F32),
        ],
        scratch_shapes=[
            pltpu.VMEM((SUBLANES, CONV_PAD + CHUNK, c_conv), F32),
            pltpu.VMEM((nb * CHUNK, Q_W), F32),
            pltpu.VMEM((nb * CHUNK, c_conv), F32),
            pltpu.VMEM((1, LANES), F32),
        ],
        compiler_params=pltpu.CompilerParams(
            dimension_semantics=("arbitrary",), vmem_limit_bytes=VMEM_LIMIT),
        name="sample_mixer",
    )(x, ck, cv, sc, *wts)


def _dispatch_kernel(meta_ref, dest_ref, hp_ref, hs_ref, xs_ref, zbuf, sem, zsem, *, td, n, blk, n_blocks, ntp):
    i = pl.program_id(0)

    def issue_from(h_ref):
        def body(r8, carry):
            for uu in range(DMA_UNROLL):
                r = r8 * DMA_UNROLL + uu
                src = h_ref.at[pl.ds(pl.multiple_of(r * n, n), n)]
                for kk in range(TOP_K):
                    d = pl.multiple_of(dest_ref[0, 0, TOP_K * r + kk], n)
                    pltpu.make_async_copy(src, xs_ref.at[pl.ds(d, n)], sem).start()
            return carry
        lax.fori_loop(0, td // DMA_UNROLL, body, 0)

    @pl.when(i < ntp)
    def _():
        issue_from(hp_ref)

    @pl.when(i >= ntp)
    def _():
        issue_from(hs_ref)

    def pad_pass(act):
        def pad_expert(e, carry):
            start = meta_ref[e]
            head = meta_ref[N_EXPERTS + e]
            body = meta_ref[2 * N_EXPERTS + e]

            @pl.when(head == 1)
            def _():
                act(pltpu.make_async_copy(zbuf.at[pl.ds(0, n)], xs_ref.at[pl.ds(pl.multiple_of(start * n, n), n)], zsem))
            bit = blk // 2
            while bit >= 2:
                off = pl.multiple_of((start + head + (body // (2 * bit)) * (2 * bit)) * n, 2 * n)

                @pl.when((body // bit) % 2 == 1)
                def _(bit=bit, off=off):
                    act(pltpu.make_async_copy(zbuf.at[pl.ds(0, bit * n)], xs_ref.at[pl.ds(off, bit * n)], zsem))
                bit //= 2
            return carry

        lax.fori_loop(0, N_EXPERTS, pad_expert, 0)

        def pad_block(j, carry):
            act(pltpu.make_async_copy(zbuf, xs_ref.at[pl.ds(pl.multiple_of(j * (blk * n), blk * n), blk * n)], zsem))
            return carry

        lax.fori_loop(meta_ref[3 * N_EXPERTS], n_blocks, pad_block, 0)

    @pl.when(i == pl.num_programs(0) - 1)
    def _():
        zbuf[...] = jnp.zeros_like(zbuf)
        pad_pass(lambda cp: cp.start())
        pad_pass(lambda cp: cp.wait())

    for _ in range(TOP_K):
        pltpu.make_async_copy(hp_ref, xs_ref.at[pl.ds(0, td * n)], sem).wait()


def _dispatch(hp, hs, dest_rows, meta, n_blocks, blk, td):
    rows_p, rows_s = hp.shape[0], hs.shape[0]
    n_tok = dest_rows.shape[0] // TOP_K
    n = (rows_p + rows_s) // n_tok
    ntp, nts = rows_p // (td * n), rows_s // (td * n)
    dest3 = dest_rows.reshape(ntp + nts, 1, TOP_K * td)
    grid_spec = pltpu.PrefetchScalarGridSpec(
        num_scalar_prefetch=1,
        grid=(ntp + nts,),
        in_specs=[
            pl.BlockSpec((1, 1, TOP_K * td), lambda i, m: (i, 0, 0), memory_space=pltpu.SMEM),
            pl.BlockSpec((td * n, LANES), lambda i, m: (jnp.minimum(i, ntp - 1), 0)),
            pl.BlockSpec((td * n, LANES), lambda i, m: (jnp.maximum(i - ntp, 0), 0)),
        ],
        out_specs=pl.BlockSpec(memory_space=pl.ANY),
        scratch_shapes=[pltpu.VMEM((blk * n, LANES), U32), pltpu.SemaphoreType.DMA(()), pltpu.SemaphoreType.DMA(())],
    )
    return pl.pallas_call(
        functools.partial(_dispatch_kernel, td=td, n=n, blk=blk, n_blocks=n_blocks, ntp=ntp),
        grid_spec=grid_spec,
        out_shape=jax.ShapeDtypeStruct((n_blocks * blk * n, LANES), U32),
        compiler_params=pltpu.CompilerParams(
            dimension_semantics=("arbitrary",), vmem_limit_bytes=VMEM_LIMIT),
        name="moe_dispatch",
    )(meta, dest3, hp, hs)


def _expert_kernel(block_e_ref, n_used_ref, x_ref, wg_ref, wu_ref, wd_ref, y_ref, wgb, wub, wdb, *, blk):
    i = pl.program_id(0)
    used = i < n_used_ref[0]
    new_expert = (i == 0) | (block_e_ref[i] != block_e_ref[jnp.maximum(i - 1, 0)])

    @pl.when(used & new_expert)
    def _():
        wgb[...] = wg_ref[0].astype(BF16)
        wub[...] = wu_ref[0].astype(BF16)
        wdb[...] = wd_ref[0].astype(BF16)

    @pl.when(used)
    def _():
        lo, hi = _unpack_rows(_load_tile_rows(x_ref, (), blk, x_ref.shape[0] // blk))
        half = lo.shape[-1]
        g = _dot(lo, wgb[0:half, :]) + _dot(hi, wgb[half:, :])
        u = _dot(lo, wub[0:half, :]) + _dot(hi, wub[half:, :])
        a = (g * jax.nn.sigmoid(g) * u).astype(BF16)
        _store_tile_rows(y_ref, (), _dot(a, wdb[...]))

    @pl.when(jnp.logical_not(used))
    def _():
        y_ref[...] = jnp.zeros_like(y_ref)


def _experts(xs, block_e, n_used, wg, wu, wd, blk):
    n_blocks = block_e.shape[0]
    _, D, de = wg.shape
    xn = xs.shape[0] // (n_blocks * blk)
    yn = D // LANES
    grid_spec = pltpu.PrefetchScalarGridSpec(
        num_scalar_prefetch=2,
        grid=(n_blocks,),
        in_specs=[
            pl.BlockSpec((blk * xn, LANES), lambda i, be, nu: (i, 0)),
            pl.BlockSpec((1, D, de), lambda i, be, nu: (be[i], 0, 0)),
            pl.BlockSpec((1, D, de), lambda i, be, nu: (be[i], 0, 0)),
            pl.BlockSpec((1, de, D), lambda i, be, nu: (be[i], 0, 0)),
        ],
        out_specs=pl.BlockSpec((blk * yn, LANES), lambda i, be, nu: (i, 0)),
        scratch_shapes=[pltpu.VMEM((D, de), BF16), pltpu.VMEM((D, de), BF16), pltpu.VMEM((de, D), BF16)],
    )
    return pl.pallas_call(
        functools.partial(_expert_kernel, blk=blk),
        grid_spec=grid_spec,
        out_shape=jax.ShapeDtypeStruct((n_blocks * blk * yn, LANES), F32),
        compiler_params=pltpu.CompilerParams(
            dimension_semantics=("arbitrary",), vmem_limit_bytes=VMEM_LIMIT),
        name="moe_experts",
    )(block_e, n_used, xs, wg, wu, wd)


def _final_kernel(dest_cur, dest_nxt, x1_ref, rec_ref, p_ref, g_ple, w_gate, w_proj, ys_ref, y_ref,
                  buf_a, buf_b, sem, *, tf, n):
    i = pl.program_id(0)
    bufs = (buf_a, buf_b)

    def issue(dref, half, s):
        for r in range(tf):
            for kk in range(TOP_K):
                d = pl.multiple_of(dref[0, 0, TOP_K * (half * tf + r) + kk], n)
                pltpu.make_async_copy(ys_ref.at[pl.ds(d, n)], bufs[s].at[kk, pl.ds(r * n, n)], sem.at[s]).start()

    def drain(s):
        for kk in range(TOP_K):
            pltpu.make_async_copy(ys_ref.at[pl.ds(0, tf * n)], bufs[s].at[kk], sem.at[s]).wait()

    def compute(half, s):
        rows = slice(half * tf, (half + 1) * tf)
        rec = rec_ref[rows, :]
        y1 = _load_tile_rows(bufs[s], (0,), tf, n)
        y2 = _load_tile_rows(bufs[s], (1,), tf, n)
        x2 = x1_ref[rows, :] + (rec[:, 2:3] * y1 + rec[:, 3:4] * y2)
        gate = jax.nn.sigmoid(_dot(_rms(x2, g_ple[...]).astype(BF16), w_gate[...]))
        y_ref[rows, :] = x2 + gate * _dot(p_ref[rows, :].astype(BF16), w_proj[...])

    @pl.when(i == 0)
    def _():
        issue(dest_cur, 0, 0)

    drain(0)
    issue(dest_cur, 1, 1)
    compute(0, 0)
    drain(1)
    issue(dest_nxt, 0, 0)
    compute(1, 1)

    @pl.when(i == pl.num_programs(0) - 1)
    def _():
        drain(0)


def _final(x1, rec, p, dest_rows, ys, g_ple, w_gate, w_proj, tf):
    N, D = x1.shape
    n = D // LANES
    pair = 2 * tf
    assert N % pair == 0
    nt = N // pair
    dest3 = dest_rows.reshape(nt, 1, TOP_K * pair)
    smem_blk = lambda fn: pl.BlockSpec((1, 1, TOP_K * pair), fn, memory_space=pltpu.SMEM)
    return pl.pallas_call(
        functools.partial(_final_kernel, tf=tf, n=n),
        grid=(nt,),
        in_specs=[
            smem_blk(lambda i: (i, 0, 0)),
            smem_blk(lambda i: (jnp.minimum(i + 1, nt - 1), 0, 0)),
            pl.BlockSpec((pair, D), lambda i: (i, 0)),
            pl.BlockSpec((pair, REC_W), lambda i: (i, 0)),
            pl.BlockSpec((pair, p.shape[-1]), lambda i: (i, 0)),
            _full(g_ple.shape), _full(w_gate.shape), _full(w_proj.shape),
            pl.BlockSpec(memory_space=pl.ANY),
        ],
        out_specs=pl.BlockSpec((pair, D), lambda i: (i, 0)),
        out_shape=jax.ShapeDtypeStruct((N, D), F32),
        scratch_shapes=[pltpu.VMEM((TOP_K, tf * n, LANES), F32), pltpu.VMEM((TOP_K, tf * n, LANES), F32),
                        pltpu.SemaphoreType.DMA((2,))],
        compiler_params=pltpu.CompilerParams(
            dimension_semantics=("arbitrary",), vmem_limit_bytes=VMEM_LIMIT),
        name="moe_combine_ple",
    )(dest3, dest3, x1, rec, p, g_ple, w_gate, w_proj, ys)


def _slots(rec, pstart, offset):
    e = rec[:, 0:TOP_K].astype(I32)
    rank = rec[:, 4:4 + TOP_K].astype(I32)
    onehot = e[:, :, None] == jnp.arange(N_EXPERTS, dtype=I32)[None, None, :]
    return (rank + jnp.sum(jnp.where(onehot, (pstart + offset)[None, None, :], 0), axis=-1)).reshape(-1)


def _bias_table(sinks):
    slopes = jnp.array([2.0 ** (-8.0 * (i + 1) / N_HEADS) for i in range(N_HEADS)], F32)
    qpos = WINDOW + jnp.arange(CHUNK, dtype=I32)
    col = jnp.arange(BAND_PAD, dtype=I32)
    dist = jnp.abs(qpos[:, None] - col[None, :]).astype(F32)
    core = slopes[:, None, None] * dist[None]
    sink = jnp.broadcast_to(-sinks.astype(F32)[:, None, None], core.shape)
    table = jnp.where(col == BAND, sink, jnp.where(col < BAND, core, MASKED))
    first_valid = jnp.array([WINDOW, WINDOW - CHUNK, 0], I32)
    table = jnp.where(col[None, None, None, :] < first_valid[:, None, None, None], MASKED, table[None])
    return table.reshape(3, N_HEADS * CHUNK, BAND_PAD)


def kernel(x_prompt, x_sample, p_prompt, p_sample, cache_k, cache_v, state_conv, g_mix, w_in, g_q, g_k, sinks,
           w_dw, b_dw, g_cn, b_cn, w_pw, b_pw, g_oa, g_oc, w_out, g_ffn, w_coarse, b_coarse, w_fine, b_fine,
           w_e_gate, w_e_up, w_e_down, g_ple, w_ple_gate, w_ple_proj):
    assert g_mix.shape[0] == 1
    l = 0
    B, T, D = x_prompt.shape
    DB, DS, _ = x_sample.shape
    assert cache_k.shape[2] == WINDOW and DS == CHUNK
    Np, Ns = B * T, DB * DS
    tt, td, tf = _tiles(T, Np, Ns)
    blk = EXPERT_BLOCK

    row = lambda a: a[l].reshape(1, -1)
    perm = jnp.concatenate([jnp.concatenate([jnp.arange(HEAD_DIM) + m * HEAD_DIM,
                                             jnp.arange(HEAD_DIM) + (GQ + m) * HEAD_DIM]) for m in range(GQ)])
    w_in_p = jnp.concatenate([w_in[l][:, perm], w_in[l][:, Q_W:]], axis=1).astype(BF16)
    w_out_p = jnp.concatenate([w_out[l][perm, :], w_out[l][Q_W:, :]], axis=0).astype(BF16)
    g_oa_p = g_oa[l][perm].reshape(1, -1)
    gain_qk = jnp.concatenate([jnp.tile(g_q[l] * (HEAD_DIM ** -0.5), N_HEADS), jnp.tile(g_k[l], N_KV_HEADS)]).reshape(1, -1)
    blk_id = jnp.arange(2 * LANES) // HEAD_DIM
    bd = jnp.where(blk_id[:, None] == blk_id[None, :], 1.0 / HEAD_DIM, 0.0).astype(BF16)
    w_r = jnp.concatenate(
        [w_coarse[l], jnp.transpose(w_fine[l], (1, 0, 2)).reshape(D, N_EXPERTS),
         jnp.zeros((D, LANES - N_GROUPS - N_EXPERTS), F32)], axis=1)
    w_rh = w_r.astype(BF16)
    w_rl = (w_r - w_rh.astype(F32)).astype(BF16)
    w_r2 = jnp.concatenate([w_rh, w_rl], axis=1)
    b_r = jnp.concatenate(
        [b_coarse[l], b_fine[l].reshape(-1), jnp.zeros((LANES - N_GROUPS - N_EXPERTS,), F32)]).reshape(1, LANES)
    bias = _bias_table(sinks[l])
    w_dw_rep = jnp.repeat(w_dw[l], SUBLANES, axis=0)

    def mixer_weights(bias_tbl):
        return (row(g_mix), w_in_p, bd, gain_qk, bias_tbl, w_dw_rep, row(b_dw),
                row(g_cn), row(b_cn), w_pw[l].astype(BF16), row(b_pw), g_oa_p, row(g_oc),
                w_out_p, row(g_ffn), w_r2, w_rh, b_r)

    x1p, hp, recp, cntp, nkp, nvp, ncp = _prompt_mixer(x_prompt, mixer_weights(bias), tt)
    nb = 4 if DB % 4 == 0 else 1
    x1s, hs, recs, cnts, nks, nvs, ncs = _sample_mixer(
        x_sample, cache_k[l].reshape(DB, WINDOW, KV_W), cache_v[l].reshape(DB, WINDOW, KV_W), state_conv[l],
        mixer_weights(bias[2:3]), nb)

    recp, recs = recp.reshape(Np, REC_W), recs.reshape(Ns, REC_W)
    cnt_p = cntp[0, :N_EXPERTS].astype(I32)
    cnt = cnt_p + cnts[0, :N_EXPERTS].astype(I32)
    n_blocks = ((Np + Ns) * TOP_K + N_EXPERTS * (blk - 1)) // blk
    padded = (cnt + blk - 1) // blk * blk
    pend = jnp.cumsum(padded)
    pstart = pend - padded
    dest_p = _slots(recp, pstart, jnp.zeros_like(cnt_p))
    dest_s = _slots(recs, pstart, cnt_p)
    block_e = jnp.minimum(
        jnp.sum((pend[None, :] <= (jnp.arange(n_blocks, dtype=I32) * blk)[:, None]).astype(I32), axis=1),
        N_EXPERTS - 1).astype(I32)
    n_used = (pend[-1:] // blk).astype(I32)
    pad_start = pstart + cnt
    pad_head = pad_start % 2
    meta = jnp.concatenate([pad_start, pad_head, padded - cnt - pad_head, n_used]).astype(I32)

    xn = D // 2 // LANES
    yn = D // LANES
    xs = _dispatch(hp.reshape(Np * xn, LANES), hs.reshape(Ns * xn, LANES),
                   jnp.concatenate([dest_p, dest_s]) * xn, meta, n_blocks, blk, td)
    ys = _experts(xs, block_e, n_used, w_e_gate[l], w_e_up[l], w_e_down[l], blk)

    gp = row(g_ple)
    w_gate = w_ple_gate[l].astype(BF16)
    w_proj = w_ple_proj[l].astype(BF16)
    yp = _final(x1p.reshape(Np, D), recp, p_prompt[l].reshape(Np, -1), dest_p * yn, ys, gp, w_gate, w_proj, tf)
    ysm = _final(x1s.reshape(Ns, D), recs, p_sample[l].reshape(Ns, -1), dest_s * yn, ys, gp, w_gate, w_proj, tf)

    kv5 = lambda a, nbat: a.reshape(1, nbat, WINDOW, N_KV_HEADS, HEAD_DIM)
    return (yp.reshape(B, T, D), ysm.reshape(DB, DS, D),
            kv5(nkp, B), kv5(nvp, B), ncp[None],
            kv5(nks, DB), kv5(nvs, DB), ncs[None])
```

```python
import functools

import jax
import jax.numpy as jnp
from jax import lax
from jax.experimental import pallas as pl
from jax.experimental.pallas import tpu as pltpu

CHUNK = 64
HEAD_DIM = 64
N_HEADS = 8
N_KV_HEADS = 2
GQ = N_HEADS // N_KV_HEADS
Q_W = N_HEADS * HEAD_DIM
KV_W = N_KV_HEADS * HEAD_DIM
WINDOW = 128
BAND = WINDOW + CHUNK
BAND_PAD = 256
CONV_WIDTH = 31
CONV_HIST = CONV_WIDTH - 1
CONV_PAD = 32
CONV_BLOCK = 32
SUBLANES = 8
LANES = 128
N_GROUPS = 4
EXPERTS_PER_GROUP = 8
N_EXPERTS = N_GROUPS * EXPERTS_PER_GROUP
TOP_K = 2
EXPERT_BLOCK = 512
EPS = 1e-6
REC_W = 8
MASKED = 1e30
DMA_UNROLL = 8

F32 = jnp.float32
BF16 = jnp.bfloat16
U32 = jnp.uint32
I32 = jnp.int32

VMEM_LIMIT = 56 * 1024 * 1024


def _tiles(n_prompt_seq, n_prompt, n_sample):
    tt = 256 if n_prompt_seq % 256 == 0 else n_prompt_seq
    td = next(t for t in (1024, 512, 256, 128, 64) if n_prompt % t == 0 and n_sample % t == 0)
    tf = min(td, 256)
    return tt, td, tf


def _rms(xf, g):
    return xf * lax.rsqrt(jnp.mean(xf * xf, axis=-1, keepdims=True) + EPS) * g


def _dot(a, b):
    return jnp.dot(a, b, preferred_element_type=F32)


def _dot_nt(a, b):
    return lax.dot_general(a, b, (((1,), (1,)), ((), ())), preferred_element_type=F32)


def _store_tile_rows(ref, lead, value):
    rows, width = value.shape
    n = width // LANES
    for c in range(n):
        ref[lead + (pl.ds(c, rows, stride=n), slice(None))] = value[:, c * LANES:(c + 1) * LANES]


def _load_tile_rows(ref, lead, rows, n):
    return jnp.concatenate([ref[lead + (pl.ds(c, rows, stride=n), slice(None))] for c in range(n)], axis=1)


def _project(x, g_mix, w_in_ref, bd_ref, gain_qk):
    xn = _rms(x, g_mix).astype(BF16)
    z = _dot(xn, w_in_ref[...])
    qk = z[:, 0:Q_W + KV_W]
    sq = (qk * qk).astype(BF16)
    bd = bd_ref[...]
    two = 2 * LANES
    ms = jnp.concatenate(
        [_dot(sq[:, 0:two], bd), _dot(sq[:, two:2 * two], bd), _dot(sq[:, 2 * two:], bd[0:KV_W, 0:KV_W])], axis=1)
    qkn = qk * lax.rsqrt(ms + EPS) * gain_qk
    v = z[:, Q_W + KV_W:Q_W + 2 * KV_W]
    c0 = Q_W + 2 * KV_W
    c_conv = (z.shape[1] - c0) // 2
    u = z[:, c0:c0 + c_conv] * jax.nn.sigmoid(z[:, c0 + c_conv:])
    return qkn, v, u


def _attend_chunk(q_groups, k_band, v_band, bias):
    lane = lax.broadcasted_iota(I32, (CHUNK, LANES), 1)
    low = lane < HEAD_DIM
    zero = jnp.zeros((CHUNK, LANES), F32)
    q_all = jnp.concatenate(
        [jnp.where(low, qg, zero) for qg in q_groups] + [jnp.where(low, zero, qg) for qg in q_groups],
        axis=0).astype(BF16)
    pad = jnp.zeros((BAND_PAD - BAND, LANES), BF16)
    s = _dot_nt(q_all, jnp.concatenate([k_band, pad], axis=0)) - bias
    m = jnp.max(s, axis=-1, keepdims=True)
    e = jnp.exp(s - m)
    denom = jnp.sum(e, axis=-1, keepdims=True)
    o = _dot(e.astype(BF16), jnp.concatenate([v_band, pad], axis=0)) * (1.0 / denom)
    half = GQ * CHUNK
    return [jnp.where(low, o[g * CHUNK:(g + 1) * CHUNK, :], o[half + g * CHUNK:half + (g + 1) * CHUNK, :])
            for g in range(GQ)]


def _shift_copies(ush, rows):
    for n in range(1, SUBLANES):
        ush[n, 0:rows, :] = ush[0, n:n + rows, :]


def _conv_rows(ush, r0, rows, w_dw_ref, b_dw):
    ch = b_dw.shape[-1]
    groups = rows // SUBLANES
    acc = jnp.broadcast_to(b_dw, (groups, SUBLANES, ch))
    for j in range(CONV_WIDTH):
        off = CONV_PAD - CONV_HIST + j
        taps = ush[off % SUBLANES, pl.ds(r0 + off - off % SUBLANES, rows), :].reshape(groups, SUBLANES, ch)
        acc = acc + w_dw_ref[j * SUBLANES:(j + 1) * SUBLANES, :][None] * taps
    return acc.reshape(rows, ch)


def _finish(x, oa, conv, refs):
    (g_cn, b_cn, w_pw, b_pw, g_oa, g_oc, w_out, g_ffn, w_r2, w_rh, b_r) = refs
    mu = jnp.mean(conv, axis=-1, keepdims=True)
    cen = conv - mu
    var = jnp.mean(cen * cen, axis=-1, keepdims=True)
    ln = cen * lax.rsqrt(var + EPS) * g_cn[...] + b_cn[...]
    act = (ln * jax.nn.sigmoid(ln)).astype(BF16)
    c = _dot(act, w_pw[...]) + b_pw[...]
    half = oa.shape[-1]
    mixed = (_dot(_rms(oa, g_oa[...]).astype(BF16), w_out[0:half, :])
             + _dot(_rms(c, g_oc[...]).astype(BF16), w_out[half:, :]))
    x1 = x + mixed
    h = _rms(x1, g_ffn[...])
    h_hi = h.astype(BF16)
    h_lo = (h - h_hi.astype(F32)).astype(BF16)
    a = _dot(h_hi, w_r2[...])
    logits = a[:, 0:LANES] + a[:, LANES:] + _dot(h_lo, w_rh[...]) + b_r[...]
    return x1, h, logits


def _pack_rows(h):
    half = h.shape[-1] // 2
    hb = h.astype(BF16).astype(F32)
    lo = lax.bitcast_convert_type(hb[:, 0:half], U32)
    hi = lax.bitcast_convert_type(hb[:, half:], U32)
    return (lo >> 16) | (hi & jnp.uint32(0xFFFF0000))


def _unpack_rows(w):
    lo = lax.bitcast_convert_type(w << 16, F32).astype(BF16)
    hi = lax.bitcast_convert_type(w & jnp.uint32(0xFFFF0000), F32).astype(BF16)
    return lo, hi


def _route(logits, cnt_ref):
    rows = logits.shape[0]
    lane = lax.broadcasted_iota(I32, logits.shape, 1).astype(F32)
    big = float(LANES)
    ninf = -jnp.inf
    lc = jnp.where(lane < N_GROUPS, logits, ninf)
    mc = jnp.max(lc, axis=-1, keepdims=True)
    grp = jnp.min(jnp.where(lc == mc, lane, big), axis=-1, keepdims=True)
    g1 = 1.0 / jnp.sum(jnp.exp(lc - mc), axis=-1, keepdims=True)
    lo = N_GROUPS + grp * EXPERTS_PER_GROUP
    lf = jnp.where((lane >= lo) & (lane < lo + EXPERTS_PER_GROUP), logits, ninf)
    t1 = jnp.max(lf, axis=-1, keepdims=True)
    i1 = jnp.min(jnp.where(lf == t1, lane, big), axis=-1, keepdims=True)
    lf2 = jnp.where(lane == i1, ninf, lf)
    t2 = jnp.max(lf2, axis=-1, keepdims=True)
    i2 = jnp.min(jnp.where(lf2 == t2, lane, big), axis=-1, keepdims=True)
    e2x = jnp.exp(t2 - t1)
    inv = 1.0 / (1.0 + e2x)
    w1 = g1 * inv
    w2 = g1 * (e2x * inv)
    e1 = i1 - N_GROUPS
    e2 = i2 - N_GROUPS
    oh1 = (lane == e1).astype(F32)
    oh2 = (lane == e2).astype(F32)
    oh = oh1 + oh2
    ri = lax.broadcasted_iota(I32, (rows, rows), 0)
    ci = lax.broadcasted_iota(I32, (rows, rows), 1)
    tri = (ci < ri).astype(BF16)
    tot = _dot(tri, oh.astype(BF16)) + cnt_ref[...]
    r1 = jnp.sum(oh1 * tot, axis=-1, keepdims=True)
    r2 = jnp.sum(oh2 * tot, axis=-1, keepdims=True)
    cnt_ref[...] = cnt_ref[...] + jnp.sum(oh, axis=0, keepdims=True)
    rec = jnp.where(lane == 0, e1, 0.0)
    rec = jnp.where(lane == 1, e2, rec)
    rec = jnp.where(lane == 2, w1, rec)
    rec = jnp.where(lane == 3, w2, rec)
    rec = jnp.where(lane == 4, r1, rec)
    rec = jnp.where(lane == 5, r2, rec)
    return rec[:, 0:REC_W], jnp.transpose(rec)[0:REC_W, :]


def _prompt_mixer_kernel(x_ref, g_mix, w_in, bd, gain_qk, bias_ref, w_dw, b_dw,
                         g_cn, b_cn, w_pw, b_pw, g_oa, g_oc, w_out, g_ffn, w_r2, w_rh, b_r,
                         x1_ref, h_ref, rec_ref, rect_ref, cnt_out, nk_ref, nv_ref, nc_ref,
                         kx, vx, ush, oa_s, conv_s, cnt_s, *, tt):
    b = pl.program_id(0)
    t = pl.program_id(1)
    n_chunks = tt // CHUNK

    @pl.when((b == 0) & (t == 0))
    def _():
        cnt_s[...] = jnp.zeros_like(cnt_s)

    @pl.when(t == 0)
    def _():
        kx[0:WINDOW, :] = jnp.zeros((WINDOW, LANES), BF16)
        vx[0:WINDOW, :] = jnp.zeros((WINDOW, LANES), BF16)
        ush[0, 0:CONV_PAD, :] = jnp.zeros((CONV_PAD, ush.shape[-1]), F32)

    x = x_ref[0]
    qkn, v, u = _project(x, g_mix[...], w_in, bd, gain_qk[...])
    k = qkn[:, Q_W:]
    kx[WINDOW:WINDOW + tt, :] = k.astype(BF16)
    vx[WINDOW:WINDOW + tt, :] = v.astype(BF16)
    ush[0, CONV_PAD:CONV_PAD + tt, :] = u
    _shift_copies(ush, tt + CONV_PAD - SUBLANES)

    for c in range(n_chunks):
        variant = jnp.minimum(t * n_chunks + c, WINDOW // CHUNK)
        rows = slice(c * CHUNK, (c + 1) * CHUNK)
        q_groups = [qkn[rows, m * LANES:(m + 1) * LANES] for m in range(GQ)]
        o_groups = _attend_chunk(q_groups, kx[c * CHUNK:c * CHUNK + BAND, :], vx[c * CHUNK:c * CHUNK + BAND, :],
                                 bias_ref[variant])
        for m in range(GQ):
            oa_s[rows, m * LANES:(m + 1) * LANES] = o_groups[m]

    for rb in range(tt // CONV_BLOCK):
        conv_s[rb * CONV_BLOCK:(rb + 1) * CONV_BLOCK, :] = _conv_rows(ush, rb * CONV_BLOCK, CONV_BLOCK, w_dw, b_dw[...])

    x1, hh, logits = _finish(x, oa_s[...], conv_s[...],
                             (g_cn, b_cn, w_pw, b_pw, g_oa, g_oc, w_out, g_ffn, w_r2, w_rh, b_r))
    x1_ref[0] = x1
    _store_tile_rows(h_ref, (0,), _pack_rows(hh))
    rec_ref[0], rect_ref[0] = _route(logits, cnt_s)
    cnt_out[...] = cnt_s[...]

    nk_ref[0] = k[tt - WINDOW:, :]
    nv_ref[0] = v[tt - WINDOW:, :]
    nc_ref[0] = u[tt - CONV_HIST:, :]

    kx[0:WINDOW, :] = kx[tt:tt + WINDOW, :]
    vx[0:WINDOW, :] = vx[tt:tt + WINDOW, :]
    ush[0, 0:CONV_PAD, :] = ush[0, tt:tt + CONV_PAD, :]


def _sample_mixer_kernel(x_ref, ck_ref, cv_ref, sc_ref, g_mix, w_in, bd, gain_qk, bias_ref, w_dw, b_dw,
                         g_cn, b_cn, w_pw, b_pw, g_oa, g_oc, w_out, g_ffn, w_r2, w_rh, b_r,
                         x1_ref, h_ref, rec_ref, rect_ref, cnt_out, nk_ref, nv_ref, nc_ref,
                         ush, oa_s, conv_s, cnt_s, *, nb):
    i = pl.program_id(0)

    @pl.when(i == 0)
    def _():
        cnt_s[...] = jnp.zeros_like(cnt_s)

    rows_all = nb * CHUNK
    x = x_ref[...].reshape(rows_all, x_ref.shape[-1])
    qkn, v, u = _project(x, g_mix[...], w_in, bd, gain_qk[...])
    k = qkn[:, Q_W:]
    for j in range(nb):
        rows = slice(j * CHUNK, (j + 1) * CHUNK)
        ck = ck_ref[j]
        cv = cv_ref[j]
        k_band = jnp.concatenate([ck, k[rows, :]], axis=0)
        v_band = jnp.concatenate([cv, v[rows, :]], axis=0)
        q_groups = [qkn[rows, m * LANES:(m + 1) * LANES] for m in range(GQ)]
        o_groups = _attend_chunk(q_groups, k_band.astype(BF16), v_band.astype(BF16), bias_ref[0])
        for m in range(GQ):
            oa_s[rows, m * LANES:(m + 1) * LANES] = o_groups[m]
        ush[0, 0:CONV_PAD, :] = jnp.zeros((CONV_PAD, ush.shape[-1]), F32)
        ush[0, CONV_PAD - CONV_HIST:CONV_PAD, :] = sc_ref[j]
        ush[0, CONV_PAD:CONV_PAD + CHUNK, :] = u[rows, :]
        _shift_copies(ush, CHUNK + CONV_PAD - SUBLANES)
        for rb in range(CHUNK // CONV_BLOCK):
            r0 = rb * CONV_BLOCK
            conv_s[j * CHUNK + r0:j * CHUNK + r0 + CONV_BLOCK, :] = _conv_rows(ush, r0, CONV_BLOCK, w_dw, b_dw[...])
        nk_ref[j] = k_band[CHUNK:, :]
        nv_ref[j] = v_band[CHUNK:, :]
        nc_ref[j] = ush[0, CONV_PAD + CHUNK - CONV_HIST:CONV_PAD + CHUNK, :]

    x1, hh, logits = _finish(x, oa_s[...], conv_s[...],
                             (g_cn, b_cn, w_pw, b_pw, g_oa, g_oc, w_out, g_ffn, w_r2, w_rh, b_r))
    x1_ref[...] = x1.reshape(x1_ref.shape)
    packed = _pack_rows(hh)
    for j in range(nb):
        _store_tile_rows(h_ref, (j,), packed[j * CHUNK:(j + 1) * CHUNK, :])
    rec, rec_t = _route(logits, cnt_s)
    rec_ref[...] = rec.reshape(rec_ref.shape)
    rect_ref[0] = rec_t
    cnt_out[...] = cnt_s[...]


def _full(shape):
    nd = len(shape)
    return pl.BlockSpec(shape, lambda *_: (0,) * nd)


def _prompt_mixer(x, wts, tt):
    B, T, D = x.shape
    c_conv = wts[6].shape[-1]
    xs_rows = D // 2 // LANES
    nt = T // tt
    tok = lambda rows, last: pl.BlockSpec((1, rows, last), lambda b, t: (b, t, 0))
    per_batch = lambda rows, last: pl.BlockSpec((1, rows, last), lambda b, t: (b, 0, 0))
    return pl.pallas_call(
        functools.partial(_prompt_mixer_kernel, tt=tt),
        grid=(B, nt),
        in_specs=[tok(tt, D)] + [_full(w.shape) for w in wts],
        out_specs=[tok(tt, D), tok(tt * xs_rows, LANES), tok(tt, REC_W),
                   pl.BlockSpec((1, REC_W, tt), lambda b, t: (b, 0, t)), pl.BlockSpec((1, LANES), lambda b, t: (0, 0)),
                   per_batch(WINDOW, KV_W), per_batch(WINDOW, KV_W), per_batch(CONV_HIST, c_conv)],
        out_shape=[
            jax.ShapeDtypeStruct((B, T, D), F32),
            jax.ShapeDtypeStruct((B, T * xs_rows, LANES), U32),
            jax.ShapeDtypeStruct((B, T, REC_W), F32),
            jax.ShapeDtypeStruct((B, REC_W, T), F32),
            jax.ShapeDtypeStruct((1, LANES), F32),
            jax.ShapeDtypeStruct((B, WINDOW, KV_W), F32),
            jax.ShapeDtypeStruct((B, WINDOW, KV_W), F32),
            jax.ShapeDtypeStruct((B, CONV_HIST, c_conv), F32),
        ],
        scratch_shapes=[
            pltpu.VMEM((WINDOW + tt, LANES), BF16),
            pltpu.VMEM((WINDOW + tt, LANES), BF16),
            pltpu.VMEM((SUBLANES, CONV_PAD + tt, c_conv), F32),
            pltpu.VMEM((tt, Q_W), F32),
            pltpu.VMEM((tt, c_conv), F32),
            pltpu.VMEM((1, LANES), F32),
        ],
        compiler_params=pltpu.CompilerParams(
            dimension_semantics=("arbitrary", "arbitrary"), vmem_limit_bytes=VMEM_LIMIT),
        name="prompt_mixer",
    )(x, *wts)


def _sample_mixer(x, ck, cv, sc, wts, nb):
    B, T, D = x.shape
    assert T == CHUNK and B % nb == 0
    c_conv = wts[6].shape[-1]
    xs_rows = D // 2 // LANES
    blk3 = lambda rows, last: pl.BlockSpec((nb, rows, last), lambda i: (i, 0, 0))
    return pl.pallas_call(
        functools.partial(_sample_mixer_kernel, nb=nb),
        grid=(B // nb,),
        in_specs=[blk3(T, D), blk3(WINDOW, KV_W), blk3(WINDOW, KV_W), blk3(CONV_HIST, c_conv)]
                 + [_full(w.shape) for w in wts],
        out_specs=[blk3(T, D), blk3(T * xs_rows, LANES), blk3(T, REC_W),
                   pl.BlockSpec((1, REC_W, nb * T), lambda i: (i, 0, 0)), pl.BlockSpec((1, LANES), lambda i: (0, 0)),
                   blk3(WINDOW, KV_W), blk3(WINDOW, KV_W), blk3(CONV_HIST, c_conv)],
        out_shape=[
            jax.ShapeDtypeStruct((B, T, D), F32),
            jax.ShapeDtypeStruct((B, T * xs_rows, LANES), U32),
            jax.ShapeDtypeStruct((B, T, REC_W), F32),
            jax.ShapeDtypeStruct((B // nb, REC_W, nb * T), F32),
            jax.ShapeDtypeStruct((1, LANES), F32),
            jax.ShapeDtypeStruct((B, WINDOW, KV_W), F32),
            jax.ShapeDtypeStruct((B, WINDOW, KV_W), F32),
            jax.ShapeDtypeStruct((B, CONV_HIST, c_conv), F32),
        ],
        scratch_shapes=[
            pltpu.VMEM((SUBLANES, CONV_PAD + CHUNK, c_conv), F32),
            pltpu.VMEM((nb * CHUNK, Q_W), F32),
            pltpu.VMEM((nb * CHUNK, c_conv), F32),
            pltpu.VMEM((1, LANES), F32),
        ],
        compiler_params=pltpu.CompilerParams(
            dimension_semantics=("arbitrary",), vmem_limit_bytes=VMEM_LIMIT),
        name="sample_mixer",
    )(x, ck, cv, sc, *wts)


def _dispatch_kernel(meta_ref, dest_ref, hp_ref, hs_ref, xs_ref, zbuf, sem, zsem, *, td, n, blk, n_blocks, ntp):
    i = pl.program_id(0)

    def issue_from(h_ref):
        def body(r8, carry):
            for uu in range(DMA_UNROLL):
                r = r8 * DMA_UNROLL + uu
                src = h_ref.at[pl.ds(pl.multiple_of(r * n, n), n)]
                for kk in range(TOP_K):
                    d = pl.multiple_of(dest_ref[kk, 0, 0, r], n)
                    pltpu.make_async_copy(src, xs_ref.at[pl.ds(d, n)], sem).start(priority=kk % 2)
            return carry
        lax.fori_loop(0, td // DMA_UNROLL, body, 0)

    @pl.when(i < ntp)
    def _():
        issue_from(hp_ref)

    @pl.when(i >= ntp)
    def _():
        issue_from(hs_ref)

    def pad_pass(act):
        def pad_expert(e, carry):
            start = meta_ref[e]
            head = meta_ref[N_EXPERTS + e]
            body = meta_ref[2 * N_EXPERTS + e]

            @pl.when(head == 1)
            def _():
                act(pltpu.make_async_copy(zbuf.at[pl.ds(0, n)], xs_ref.at[pl.ds(pl.multiple_of(start * n, n), n)], zsem))
            bit = blk // 2
            while bit >= 2:
                off = pl.multiple_of((start + head + (body // (2 * bit)) * (2 * bit)) * n, 2 * n)

                @pl.when((body // bit) % 2 == 1)
                def _(bit=bit, off=off):
                    act(pltpu.make_async_copy(zbuf.at[pl.ds(0, bit * n)], xs_ref.at[pl.ds(off, bit * n)], zsem))
                bit //= 2
            return carry

        lax.fori_loop(0, N_EXPERTS, pad_expert, 0)

        def pad_block(j, carry):
            act(pltpu.make_async_copy(zbuf, xs_ref.at[pl.ds(pl.multiple_of(j * (blk * n), blk * n), blk * n)], zsem))
            return carry

        lax.fori_loop(meta_ref[3 * N_EXPERTS], n_blocks, pad_block, 0)

    @pl.when(i == pl.num_programs(0) - 1)
    def _():
        zbuf[...] = jnp.zeros_like(zbuf)
        pad_pass(lambda cp: cp.start())
        pad_pass(lambda cp: cp.wait())

    for _ in range(TOP_K):
        pltpu.make_async_copy(hp_ref, xs_ref.at[pl.ds(0, td * n)], sem).wait()


def _dispatch(hp, hs, dest_rows, meta, n_blocks, blk, td):
    rows_p, rows_s = hp.shape[0], hs.shape[0]
    n_tok = dest_rows.shape[1]
    n = (rows_p + rows_s) // n_tok
    ntp, nts = rows_p // (td * n), rows_s // (td * n)
    dest3 = dest_rows.reshape(TOP_K, ntp + nts, 1, td)
    grid_spec = pltpu.PrefetchScalarGridSpec(
        num_scalar_prefetch=1,
        grid=(ntp + nts,),
        in_specs=[
            pl.BlockSpec((TOP_K, 1, 1, td), lambda i, m: (0, i, 0, 0), memory_space=pltpu.SMEM),
            pl.BlockSpec((td * n, LANES), lambda i, m: (jnp.minimum(i, ntp - 1), 0)),
            pl.BlockSpec((td * n, LANES), lambda i, m: (jnp.maximum(i - ntp, 0), 0)),
        ],
        out_specs=pl.BlockSpec(memory_space=pl.ANY),
        scratch_shapes=[pltpu.VMEM((blk * n, LANES), U32), pltpu.SemaphoreType.DMA(()), pltpu.SemaphoreType.DMA(())],
    )
    return pl.pallas_call(
        functools.partial(_dispatch_kernel, td=td, n=n, blk=blk, n_blocks=n_blocks, ntp=ntp),
        grid_spec=grid_spec,
        out_shape=jax.ShapeDtypeStruct((n_blocks * blk * n, LANES), U32),
        compiler_params=pltpu.CompilerParams(
            dimension_semantics=("arbitrary",), vmem_limit_bytes=VMEM_LIMIT),
        name="moe_dispatch",
    )(meta, dest3, hp, hs)


def _expert_kernel(block_e_ref, n_used_ref, x_ref, wg_ref, wu_ref, wd_ref, y_ref, wgb, wub, wdb, *, blk):
    i = pl.program_id(0)
    used = i < n_used_ref[0]
    new_expert = (i == 0) | (block_e_ref[i] != block_e_ref[jnp.maximum(i - 1, 0)])

    @pl.when(used & new_expert)
    def _():
        wgb[...] = wg_ref[0].astype(BF16)
        wub[...] = wu_ref[0].astype(BF16)
        wdb[...] = wd_ref[0].astype(BF16)

    @pl.when(used)
    def _():
        lo, hi = _unpack_rows(_load_tile_rows(x_ref, (), blk, x_ref.shape[0] // blk))
        half = lo.shape[-1]
        g = _dot(lo, wgb[0:half, :]) + _dot(hi, wgb[half:, :])
        u = _dot(lo, wub[0:half, :]) + _dot(hi, wub[half:, :])
        a = (g * jax.nn.sigmoid(g) * u).astype(BF16)
        _store_tile_rows(y_ref, (), _dot(a, wdb[...]))

    @pl.when(jnp.logical_not(used))
    def _():
        y_ref[...] = jnp.zeros_like(y_ref)


def _experts(xs, block_e, n_used, wg, wu, wd, blk):
    n_blocks = block_e.shape[0]
    _, D, de = wg.shape
    xn = xs.shape[0] // (n_blocks * blk)
    yn = D // LANES
    grid_spec = pltpu.PrefetchScalarGridSpec(
        num_scalar_prefetch=2,
        grid=(n_blocks,),
        in_specs=[
            pl.BlockSpec((blk * xn, LANES), lambda i, be, nu: (i, 0)),
            pl.BlockSpec((1, D, de), lambda i, be, nu: (be[i], 0, 0)),
            pl.BlockSpec((1, D, de), lambda i, be, nu: (be[i], 0, 0)),
            pl.BlockSpec((1, de, D), lambda i, be, nu: (be[i], 0, 0)),
        ],
        out_specs=pl.BlockSpec((blk * yn, LANES), lambda i, be, nu: (i, 0)),
        scratch_shapes=[pltpu.VMEM((D, de), BF16), pltpu.VMEM((D, de), BF16), pltpu.VMEM((de, D), BF16)],
    )
    return pl.pallas_call(
        functools.partial(_expert_kernel, blk=blk),
        grid_spec=grid_spec,
        out_shape=jax.ShapeDtypeStruct((n_blocks * blk * yn, LANES), F32),
        compiler_params=pltpu.CompilerParams(
            dimension_semantics=("arbitrary",), vmem_limit_bytes=VMEM_LIMIT),
        name="moe_experts",
    )(block_e, n_used, xs, wg, wu, wd)


def _final_kernel(dest_cur, dest_nxt, x1_ref, rec_ref, p_ref, g_ple, w_gate, w_proj, ys_ref, y_ref,
                  buf_a, buf_b, sem, *, tf, n):
    i = pl.program_id(0)
    bufs = (buf_a, buf_b)

    def issue(dref, half, s):
        for r in range(tf):
            for kk in range(TOP_K):
                d = pl.multiple_of(dref[kk, 0, 0, half * tf + r], n)
                pltpu.make_async_copy(ys_ref.at[pl.ds(d, n)], bufs[s].at[kk, pl.ds(r * n, n)],
                                      sem.at[s]).start(priority=kk % 2)

    def drain(s):
        for kk in range(TOP_K):
            pltpu.make_async_copy(ys_ref.at[pl.ds(0, tf * n)], bufs[s].at[kk], sem.at[s]).wait()

    def compute(half, s):
        rows = slice(half * tf, (half + 1) * tf)
        rec = rec_ref[rows, :]
        y1 = _load_tile_rows(bufs[s], (0,), tf, n)
        y2 = _load_tile_rows(bufs[s], (1,), tf, n)
        x2 = x1_ref[rows, :] + (rec[:, 2:3] * y1 + rec[:, 3:4] * y2)
        gate = jax.nn.sigmoid(_dot(_rms(x2, g_ple[...]).astype(BF16), w_gate[...]))
        y_ref[rows, :] = x2 + gate * _dot(p_ref[rows, :].astype(BF16), w_proj[...])

    @pl.when(i == 0)
    def _():
        issue(dest_cur, 0, 0)

    drain(0)
    issue(dest_cur, 1, 1)
    compute(0, 0)
    drain(1)
    issue(dest_nxt, 0, 0)
    compute(1, 1)

    @pl.when(i == pl.num_programs(0) - 1)
    def _():
        drain(0)


def _final(x1, rec, p, dest_rows, ys, g_ple, w_gate, w_proj, tf):
    N, D = x1.shape
    n = D // LANES
    pair = 2 * tf
    assert N % pair == 0
    nt = N // pair
    dest3 = dest_rows.reshape(TOP_K, nt, 1, pair)
    smem_blk = lambda fn: pl.BlockSpec((TOP_K, 1, 1, pair), fn, memory_space=pltpu.SMEM)
    return pl.pallas_call(
        functools.partial(_final_kernel, tf=tf, n=n),
        grid=(nt,),
        in_specs=[
            smem_blk(lambda i: (0, i, 0, 0)),
            smem_blk(lambda i: (0, jnp.minimum(i + 1, nt - 1), 0, 0)),
            pl.BlockSpec((pair, D), lambda i: (i, 0)),
            pl.BlockSpec((pair, REC_W), lambda i: (i, 0)),
            pl.BlockSpec((pair, p.shape[-1]), lambda i: (i, 0)),
            _full(g_ple.shape), _full(w_gate.shape), _full(w_proj.shape),
            pl.BlockSpec(memory_space=pl.ANY),
        ],
        out_specs=pl.BlockSpec((pair, D), lambda i: (i, 0)),
        out_shape=jax.ShapeDtypeStruct((N, D), F32),
        scratch_shapes=[pltpu.VMEM((TOP_K, tf * n, LANES), F32), pltpu.VMEM((TOP_K, tf * n, LANES), F32),
                        pltpu.SemaphoreType.DMA((2,))],
        compiler_params=pltpu.CompilerParams(
            dimension_semantics=("arbitrary",), vmem_limit_bytes=VMEM_LIMIT),
        name="moe_combine_ple",
    )(dest3, dest3, x1, rec, p, g_ple, w_gate, w_proj, ys)


def _slots(rec, pstart, offset):
    e = rec[0:TOP_K, :].astype(I32)
    rank = rec[4:4 + TOP_K, :].astype(I32)
    onehot = e[None, :, :] == jnp.arange(N_EXPERTS, dtype=I32)[:, None, None]
    return rank + jnp.sum(jnp.where(onehot, (pstart + offset)[:, None, None], 0), axis=0)


def _bias_table(sinks):
    slopes = jnp.array([2.0 ** (-8.0 * (i + 1) / N_HEADS) for i in range(N_HEADS)], F32)
    qpos = WINDOW + jnp.arange(CHUNK, dtype=I32)
    col = jnp.arange(BAND_PAD, dtype=I32)
    dist = jnp.abs(qpos[:, None] - col[None, :]).astype(F32)
    core = slopes[:, None, None] * dist[None]
    sink = jnp.broadcast_to(-sinks.astype(F32)[:, None, None], core.shape)
    table = jnp.where(col == BAND, sink, jnp.where(col < BAND, core, MASKED))
    first_valid = jnp.array([WINDOW, WINDOW - CHUNK, 0], I32)
    table = jnp.where(col[None, None, None, :] < first_valid[:, None, None, None], MASKED, table[None])
    return table.reshape(3, N_HEADS * CHUNK, BAND_PAD)


def kernel(x_prompt, x_sample, p_prompt, p_sample, cache_k, cache_v, state_conv, g_mix, w_in, g_q, g_k, sinks,
           w_dw, b_dw, g_cn, b_cn, w_pw, b_pw, g_oa, g_oc, w_out, g_ffn, w_coarse, b_coarse, w_fine, b_fine,
           w_e_gate, w_e_up, w_e_down, g_ple, w_ple_gate, w_ple_proj):
    assert g_mix.shape[0] == 1
    l = 0
    B, T, D = x_prompt.shape
    DB, DS, _ = x_sample.shape
    assert cache_k.shape[2] == WINDOW and DS == CHUNK
    Np, Ns = B * T, DB * DS
    tt, td, tf = _tiles(T, Np, Ns)
    blk = EXPERT_BLOCK

    row = lambda a: a[l].reshape(1, -1)
    perm = jnp.concatenate([jnp.concatenate([jnp.arange(HEAD_DIM) + m * HEAD_DIM,
                                             jnp.arange(HEAD_DIM) + (GQ + m) * HEAD_DIM]) for m in range(GQ)])
    w_in_p = jnp.concatenate([w_in[l][:, perm], w_in[l][:, Q_W:]], axis=1).astype(BF16)
    w_out_p = jnp.concatenate([w_out[l][perm, :], w_out[l][Q_W:, :]], axis=0).astype(BF16)
    g_oa_p = g_oa[l][perm].reshape(1, -1)
    gain_qk = jnp.concatenate([jnp.tile(g_q[l] * (HEAD_DIM ** -0.5), N_HEADS), jnp.tile(g_k[l], N_KV_HEADS)]).reshape(1, -1)
    blk_id = jnp.arange(2 * LANES) // HEAD_DIM
    bd = jnp.where(blk_id[:, None] == blk_id[None, :], 1.0 / HEAD_DIM, 0.0).astype(BF16)
    w_r = jnp.concatenate(
        [w_coarse[l], jnp.transpose(w_fine[l], (1, 0, 2)).reshape(D, N_EXPERTS),
         jnp.zeros((D, LANES - N_GROUPS - N_EXPERTS), F32)], axis=1)
    w_rh = w_r.astype(BF16)
    w_rl = (w_r - w_rh.astype(F32)).astype(BF16)
    w_r2 = jnp.concatenate([w_rh, w_rl], axis=1)
    b_r = jnp.concatenate(
        [b_coarse[l], b_fine[l].reshape(-1), jnp.zeros((LANES - N_GROUPS - N_EXPERTS,), F32)]).reshape(1, LANES)
    bias = _bias_table(sinks[l])
    w_dw_rep = jnp.repeat(w_dw[l], SUBLANES, axis=0)

    def mixer_weights(bias_tbl):
        return (row(g_mix), w_in_p, bd, gain_qk, bias_tbl, w_dw_rep, row(b_dw),
                row(g_cn), row(b_cn), w_pw[l].astype(BF16), row(b_pw), g_oa_p, row(g_oc),
                w_out_p, row(g_ffn), w_r2, w_rh, b_r)

    x1p, hp, recp, rectp, cntp, nkp, nvp, ncp = _prompt_mixer(x_prompt, mixer_weights(bias), tt)
    nb = 4 if DB % 4 == 0 else 1
    x1s, hs, recs, rects, cnts, nks, nvs, ncs = _sample_mixer(
        x_sample, cache_k[l].reshape(DB, WINDOW, KV_W), cache_v[l].reshape(DB, WINDOW, KV_W), state_conv[l],
        mixer_weights(bias[2:3]), nb)

    recp, recs = recp.reshape(Np, REC_W), recs.reshape(Ns, REC_W)
    rectp = jnp.transpose(rectp, (1, 0, 2)).reshape(REC_W, Np)
    rects = jnp.transpose(rects, (1, 0, 2)).reshape(REC_W, Ns)
    cnt_p = cntp[0, :N_EXPERTS].astype(I32)
    cnt = cnt_p + cnts[0, :N_EXPERTS].astype(I32)
    n_blocks = ((Np + Ns) * TOP_K + N_EXPERTS * (blk - 1)) // blk
    padded = (cnt + blk - 1) // blk * blk
    pend = jnp.cumsum(padded)
    pstart = pend - padded
    dest_p = _slots(rectp, pstart, jnp.zeros_like(cnt_p))
    dest_s = _slots(rects, pstart, cnt_p)
    block_e = jnp.minimum(
        jnp.sum((pend[None, :] <= (jnp.arange(n_blocks, dtype=I32) * blk)[:, None]).astype(I32), axis=1),
        N_EXPERTS - 1).astype(I32)
    n_used = (pend[-1:] // blk).astype(I32)
    pad_start = pstart + cnt
    pad_head = pad_start % 2
    meta = jnp.concatenate([pad_start, pad_head, padded - cnt - pad_head, n_used]).astype(I32)

    xn = D // 2 // LANES
    yn = D // LANES
    xs = _dispatch(hp.reshape(Np * xn, LANES), hs.reshape(Ns * xn, LANES),
                   jnp.concatenate([dest_p, dest_s], axis=1) * xn, meta, n_blocks, blk, td)
    ys = _experts(xs, block_e, n_used, w_e_gate[l], w_e_up[l], w_e_down[l], blk)

    gp = row(g_ple)
    w_gate = w_ple_gate[l].astype(BF16)
    w_proj = w_ple_proj[l].astype(BF16)
    yp = _final(x1p.reshape(Np, D), recp, p_prompt[l].reshape(Np, -1), dest_p * yn, ys, gp, w_gate, w_proj, tf)
    ysm = _final(x1s.reshape(Ns, D), recs, p_sample[l].reshape(Ns, -1), dest_s * yn, ys, gp, w_gate, w_proj, tf)

    kv5 = lambda a, nbat: a.reshape(1, nbat, WINDOW, N_KV_HEADS, HEAD_DIM)
    return (yp.reshape(B, T, D), ysm.reshape(DB, DS, D),
            kv5(nkp, B), kv5(nvp, B), ncp[None],
            kv5(nks, DB), kv5(nvs, DB), ncs[None])
```

```python
import functools

import jax
import jax.numpy as jnp
from jax import lax
from jax.experimental import pallas as pl
from jax.experimental.pallas import tpu as pltpu

CHUNK = 64
HEAD_DIM = 64
N_HEADS = 8
N_KV_HEADS = 2
GQ = N_HEADS // N_KV_HEADS
Q_W = N_HEADS * HEAD_DIM
KV_W = N_KV_HEADS * HEAD_DIM
WINDOW = 128
BAND = WINDOW + CHUNK
BAND_PAD = 256
CONV_WIDTH = 31
CONV_HIST = CONV_WIDTH - 1
CONV_PAD = 32
CONV_BLOCK = 32
SUBLANES = 8
LANES = 128
N_GROUPS = 4
EXPERTS_PER_GROUP = 8
N_EXPERTS = N_GROUPS * EXPERTS_PER_GROUP
TOP_K = 2
EXPERT_BLOCK = 512
EPS = 1e-6
REC_W = 8
MASKED = 1e30
DMA_UNROLL = 8

F32 = jnp.float32
BF16 = jnp.bfloat16
U32 = jnp.uint32
I32 = jnp.int32

VMEM_LIMIT = 56 * 1024 * 1024


def _tiles(n_prompt_seq, n_prompt, n_sample):
    tt = next((t for t in (512, 256) if n_prompt_seq % t == 0), n_prompt_seq)
    td = next(t for t in (1024, 512, 256, 128, 64) if n_prompt % t == 0 and n_sample % t == 0)
    tf = min(td, 256)
    return tt, td, tf


def _rms(xf, g):
    return xf * lax.rsqrt(jnp.mean(xf * xf, axis=-1, keepdims=True) + EPS) * g


def _dot(a, b):
    return jnp.dot(a, b, preferred_element_type=F32)


def _dot_nt(a, b):
    return lax.dot_general(a, b, (((1,), (1,)), ((), ())), preferred_element_type=F32)


def _store_tile_rows(ref, lead, value):
    rows, width = value.shape
    n = width // LANES
    for c in range(n):
        ref[lead + (pl.ds(c, rows, stride=n), slice(None))] = value[:, c * LANES:(c + 1) * LANES]


def _load_tile_rows(ref, lead, rows, n):
    return jnp.concatenate([ref[lead + (pl.ds(c, rows, stride=n), slice(None))] for c in range(n)], axis=1)


def _project(x, g_mix, w_in_ref, bd_ref, gain_qk):
    xn = _rms(x, g_mix).astype(BF16)
    z = _dot(xn, w_in_ref[...])
    qk = z[:, 0:Q_W + KV_W]
    sq = (qk * qk).astype(BF16)
    bd = bd_ref[...]
    two = 2 * LANES
    ms = jnp.concatenate(
        [_dot(sq[:, 0:two], bd), _dot(sq[:, two:2 * two], bd), _dot(sq[:, 2 * two:], bd[0:KV_W, 0:KV_W])], axis=1)
    qkn = qk * lax.rsqrt(ms + EPS) * gain_qk
    v = z[:, Q_W + KV_W:Q_W + 2 * KV_W]
    c0 = Q_W + 2 * KV_W
    c_conv = (z.shape[1] - c0) // 2
    u = z[:, c0:c0 + c_conv] * jax.nn.sigmoid(z[:, c0 + c_conv:])
    return qkn, v, u


def _attend_chunk(q_groups, k_band, v_band, bias):
    lane = lax.broadcasted_iota(I32, (CHUNK, LANES), 1)
    low = lane < HEAD_DIM
    zero = jnp.zeros((CHUNK, LANES), F32)
    q_all = jnp.concatenate(
        [jnp.where(low, qg, zero) for qg in q_groups] + [jnp.where(low, zero, qg) for qg in q_groups],
        axis=0).astype(BF16)
    pad = jnp.zeros((BAND_PAD - BAND, LANES), BF16)
    s = _dot_nt(q_all, jnp.concatenate([k_band, pad], axis=0)) - bias
    m = jnp.max(s, axis=-1, keepdims=True)
    e = jnp.exp(s - m)
    denom = jnp.sum(e, axis=-1, keepdims=True)
    o = _dot(e.astype(BF16), jnp.concatenate([v_band, pad], axis=0)) * (1.0 / denom)
    half = GQ * CHUNK
    return [jnp.where(low, o[g * CHUNK:(g + 1) * CHUNK, :], o[half + g * CHUNK:half + (g + 1) * CHUNK, :])
            for g in range(GQ)]


def _shift_copies(ush, rows):
    for n in range(1, SUBLANES):
        ush[n, 0:rows, :] = ush[0, n:n + rows, :]


def _conv_rows(ush, r0, rows, w_dw_ref, b_dw):
    ch = b_dw.shape[-1]
    groups = rows // SUBLANES
    acc = jnp.broadcast_to(b_dw, (groups, SUBLANES, ch))
    for j in range(CONV_WIDTH):
        off = CONV_PAD - CONV_HIST + j
        taps = ush[off % SUBLANES, pl.ds(r0 + off - off % SUBLANES, rows), :].reshape(groups, SUBLANES, ch)
        acc = acc + w_dw_ref[j * SUBLANES:(j + 1) * SUBLANES, :][None] * taps
    return acc.reshape(rows, ch)


def _finish(x, oa, conv, refs):
    (g_cn, b_cn, w_pw, b_pw, g_oa, g_oc, w_out, g_ffn, w_r2, w_rh, b_r) = refs
    mu = jnp.mean(conv, axis=-1, keepdims=True)
    cen = conv - mu
    var = jnp.mean(cen * cen, axis=-1, keepdims=True)
    ln = cen * lax.rsqrt(var + EPS) * g_cn[...] + b_cn[...]
    act = (ln * jax.nn.sigmoid(ln)).astype(BF16)
    c = _dot(act, w_pw[...]) + b_pw[...]
    half = oa.shape[-1]
    mixed = (_dot(_rms(oa, g_oa[...]).astype(BF16), w_out[0:half, :])
             + _dot(_rms(c, g_oc[...]).astype(BF16), w_out[half:, :]))
    x1 = x + mixed
    h = _rms(x1, g_ffn[...])
    h_hi = h.astype(BF16)
    h_lo = (h - h_hi.astype(F32)).astype(BF16)
    a = _dot(h_hi, w_r2[...])
    logits = a[:, 0:LANES] + a[:, LANES:] + _dot(h_lo, w_rh[...]) + b_r[...]
    return x1, h, logits


def _pack_rows(h):
    half = h.shape[-1] // 2
    hb = h.astype(BF16).astype(F32)
    lo = lax.bitcast_convert_type(hb[:, 0:half], U32)
    hi = lax.bitcast_convert_type(hb[:, half:], U32)
    return (lo >> 16) | (hi & jnp.uint32(0xFFFF0000))


def _unpack_rows(w):
    lo = lax.bitcast_convert_type(w << 16, F32).astype(BF16)
    hi = lax.bitcast_convert_type(w & jnp.uint32(0xFFFF0000), F32).astype(BF16)
    return lo, hi


def _route(logits, cnt_ref):
    rows = logits.shape[0]
    lane = lax.broadcasted_iota(I32, logits.shape, 1).astype(F32)
    big = float(LANES)
    ninf = -jnp.inf
    lc = jnp.where(lane < N_GROUPS, logits, ninf)
    mc = jnp.max(lc, axis=-1, keepdims=True)
    grp = jnp.min(jnp.where(lc == mc, lane, big), axis=-1, keepdims=True)
    g1 = 1.0 / jnp.sum(jnp.exp(lc - mc), axis=-1, keepdims=True)
    lo = N_GROUPS + grp * EXPERTS_PER_GROUP
    lf = jnp.where((lane >= lo) & (lane < lo + EXPERTS_PER_GROUP), logits, ninf)
    t1 = jnp.max(lf, axis=-1, keepdims=True)
    i1 = jnp.min(jnp.where(lf == t1, lane, big), axis=-1, keepdims=True)
    lf2 = jnp.where(lane == i1, ninf, lf)
    t2 = jnp.max(lf2, axis=-1, keepdims=True)
    i2 = jnp.min(jnp.where(lf2 == t2, lane, big), axis=-1, keepdims=True)
    e2x = jnp.exp(t2 - t1)
    inv = 1.0 / (1.0 + e2x)
    w1 = g1 * inv
    w2 = g1 * (e2x * inv)
    e1 = i1 - N_GROUPS
    e2 = i2 - N_GROUPS
    oh1 = (lane == e1).astype(F32)
    oh2 = (lane == e2).astype(F32)
    oh = oh1 + oh2
    ri = lax.broadcasted_iota(I32, (rows, rows), 0)
    ci = lax.broadcasted_iota(I32, (rows, rows), 1)
    tri = (ci < ri).astype(BF16)
    tot = _dot(tri, oh.astype(BF16)) + cnt_ref[...]
    r1 = jnp.sum(oh1 * tot, axis=-1, keepdims=True)
    r2 = jnp.sum(oh2 * tot, axis=-1, keepdims=True)
    cnt_ref[...] = cnt_ref[...] + jnp.sum(oh, axis=0, keepdims=True)
    rec = jnp.where(lane == 0, e1, 0.0)
    rec = jnp.where(lane == 1, e2, rec)
    rec = jnp.where(lane == 2, w1, rec)
    rec = jnp.where(lane == 3, w2, rec)
    rec = jnp.where(lane == 4, r1, rec)
    rec = jnp.where(lane == 5, r2, rec)
    return rec[:, 0:REC_W], jnp.transpose(rec)[0:REC_W, :]


def _prompt_mixer_kernel(x_ref, g_mix, w_in, bd, gain_qk, bias_ref, w_dw, b_dw,
                         g_cn, b_cn, w_pw, b_pw, g_oa, g_oc, w_out, g_ffn, w_r2, w_rh, b_r,
                         x1_ref, h_ref, rec_ref, rect_ref, cnt_out, nk_ref, nv_ref, nc_ref,
                         kx, vx, ush, oa_s, conv_s, cnt_s, *, tt):
    b = pl.program_id(0)
    t = pl.program_id(1)
    n_chunks = tt // CHUNK

    @pl.when((b == 0) & (t == 0))
    def _():
        cnt_s[...] = jnp.zeros_like(cnt_s)

    @pl.when(t == 0)
    def _():
        kx[0:WINDOW, :] = jnp.zeros((WINDOW, LANES), BF16)
        vx[0:WINDOW, :] = jnp.zeros((WINDOW, LANES), BF16)
        ush[0, 0:CONV_PAD, :] = jnp.zeros((CONV_PAD, ush.shape[-1]), F32)

    x = x_ref[0]
    qkn, v, u = _project(x, g_mix[...], w_in, bd, gain_qk[...])
    k = qkn[:, Q_W:]
    kx[WINDOW:WINDOW + tt, :] = k.astype(BF16)
    vx[WINDOW:WINDOW + tt, :] = v.astype(BF16)
    ush[0, CONV_PAD:CONV_PAD + tt, :] = u
    _shift_copies(ush, tt + CONV_PAD - SUBLANES)

    for c in range(n_chunks):
        variant = jnp.minimum(t * n_chunks + c, WINDOW // CHUNK)
        rows = slice(c * CHUNK, (c + 1) * CHUNK)
        q_groups = [qkn[rows, m * LANES:(m + 1) * LANES] for m in range(GQ)]
        o_groups = _attend_chunk(q_groups, kx[c * CHUNK:c * CHUNK + BAND, :], vx[c * CHUNK:c * CHUNK + BAND, :],
                                 bias_ref[variant])
        for m in range(GQ):
            oa_s[rows, m * LANES:(m + 1) * LANES] = o_groups[m]

    for rb in range(tt // CONV_BLOCK):
        conv_s[rb * CONV_BLOCK:(rb + 1) * CONV_BLOCK, :] = _conv_rows(ush, rb * CONV_BLOCK, CONV_BLOCK, w_dw, b_dw[...])

    x1, hh, logits = _finish(x, oa_s[...], conv_s[...],
                             (g_cn, b_cn, w_pw, b_pw, g_oa, g_oc, w_out, g_ffn, w_r2, w_rh, b_r))
    x1_ref[0] = x1
    _store_tile_rows(h_ref, (0,), _pack_rows(hh))
    rec_ref[0], rect_ref[0] = _route(logits, cnt_s)
    cnt_out[...] = cnt_s[...]

    nk_ref[0] = k[tt - WINDOW:, :]
    nv_ref[0] = v[tt - WINDOW:, :]
    nc_ref[0] = u[tt - CONV_HIST:, :]

    kx[0:WINDOW, :] = kx[tt:tt + WINDOW, :]
    vx[0:WINDOW, :] = vx[tt:tt + WINDOW, :]
    ush[0, 0:CONV_PAD, :] = ush[0, tt:tt + CONV_PAD, :]


def _sample_mixer_kernel(x_ref, ck_ref, cv_ref, sc_ref, g_mix, w_in, bd, gain_qk, bias_ref, w_dw, b_dw,
                         g_cn, b_cn, w_pw, b_pw, g_oa, g_oc, w_out, g_ffn, w_r2, w_rh, b_r,
                         x1_ref, h_ref, rec_ref, rect_ref, cnt_out, nk_ref, nv_ref, nc_ref,
                         ush, oa_s, conv_s, cnt_s, *, nb):
    i = pl.program_id(0)

    @pl.when(i == 0)
    def _():
        cnt_s[...] = jnp.zeros_like(cnt_s)

    rows_all = nb * CHUNK
    x = x_ref[...].reshape(rows_all, x_ref.shape[-1])
    qkn, v, u = _project(x, g_mix[...], w_in, bd, gain_qk[...])
    k = qkn[:, Q_W:]
    for j in range(nb):
        rows = slice(j * CHUNK, (j + 1) * CHUNK)
        ck = ck_ref[j]
        cv = cv_ref[j]
        k_band = jnp.concatenate([ck, k[rows, :]], axis=0)
        v_band = jnp.concatenate([cv, v[rows, :]], axis=0)
        q_groups = [qkn[rows, m * LANES:(m + 1) * LANES] for m in range(GQ)]
        o_groups = _attend_chunk(q_groups, k_band.astype(BF16), v_band.astype(BF16), bias_ref[0])
        for m in range(GQ):
            oa_s[rows, m * LANES:(m + 1) * LANES] = o_groups[m]
        ush[0, 0:CONV_PAD, :] = jnp.zeros((CONV_PAD, ush.shape[-1]), F32)
        ush[0, CONV_PAD - CONV_HIST:CONV_PAD, :] = sc_ref[j]
        ush[0, CONV_PAD:CONV_PAD + CHUNK, :] = u[rows, :]
        _shift_copies(ush, CHUNK + CONV_PAD - SUBLANES)
        for rb in range(CHUNK // CONV_BLOCK):
            r0 = rb * CONV_BLOCK
            conv_s[j * CHUNK + r0:j * CHUNK + r0 + CONV_BLOCK, :] = _conv_rows(ush, r0, CONV_BLOCK, w_dw, b_dw[...])
        nk_ref[j] = k_band[CHUNK:, :]
        nv_ref[j] = v_band[CHUNK:, :]
        nc_ref[j] = ush[0, CONV_PAD + CHUNK - CONV_HIST:CONV_PAD + CHUNK, :]

    x1, hh, logits = _finish(x, oa_s[...], conv_s[...],
                             (g_cn, b_cn, w_pw, b_pw, g_oa, g_oc, w_out, g_ffn, w_r2, w_rh, b_r))
    x1_ref[...] = x1.reshape(x1_ref.shape)
    packed = _pack_rows(hh)
    for j in range(nb):
        _store_tile_rows(h_ref, (j,), packed[j * CHUNK:(j + 1) * CHUNK, :])
    rec, rec_t = _route(logits, cnt_s)
    rec_ref[...] = rec.reshape(rec_ref.shape)
    rect_ref[0] = rec_t
    cnt_out[...] = cnt_s[...]


def _full(shape):
    nd = len(shape)
    return pl.BlockSpec(shape, lambda *_: (0,) * nd)


def _prompt_mixer(x, wts, tt):
    B, T, D = x.shape
    c_conv = wts[6].shape[-1]
    xs_rows = D // 2 // LANES
    nt = T // tt
    tok = lambda rows, last: pl.BlockSpec((1, rows, last), lambda b, t: (b, t, 0))
    per_batch = lambda rows, last: pl.BlockSpec((1, rows, last), lambda b, t: (b, 0, 0))
    return pl.pallas_call(
        functools.partial(_prompt_mixer_kernel, tt=tt),
        grid=(B, nt),
        in_specs=[tok(tt, D)] + [_full(w.shape) for w in wts],
        out_specs=[tok(tt, D), tok(tt * xs_rows, LANES), tok(tt, REC_W),
                   pl.BlockSpec((1, REC_W, tt), lambda b, t: (b, 0, t)), pl.BlockSpec((1, LANES), lambda b, t: (0, 0)),
                   per_batch(WINDOW, KV_W), per_batch(WINDOW, KV_W), per_batch(CONV_HIST, c_conv)],
        out_shape=[
            jax.ShapeDtypeStruct((B, T, D), F32),
            jax.ShapeDtypeStruct((B, T * xs_rows, LANES), U32),
            jax.ShapeDtypeStruct((B, T, REC_W), F32),
            jax.ShapeDtypeStruct((B, REC_W, T), F32),
            jax.ShapeDtypeStruct((1, LANES), F32),
            jax.ShapeDtypeStruct((B, WINDOW, KV_W), F32),
            jax.ShapeDtypeStruct((B, WINDOW, KV_W), F32),
            jax.ShapeDtypeStruct((B, CONV_HIST, c_conv), F32),
        ],
        scratch_shapes=[
            pltpu.VMEM((WINDOW + tt, LANES), BF16),
            pltpu.VMEM((WINDOW + tt, LANES), BF16),
            pltpu.VMEM((SUBLANES, CONV_PAD + tt, c_conv), F32),
            pltpu.VMEM((tt, Q_W), F32),
            pltpu.VMEM((tt, c_conv), F32),
            pltpu.VMEM((1, LANES), F32),
        ],
        compiler_params=pltpu.CompilerParams(
            dimension_semantics=("arbitrary", "arbitrary"), vmem_limit_bytes=VMEM_LIMIT),
        name="prompt_mixer",
    )(x, *wts)


def _sample_mixer(x, ck, cv, sc, wts, nb):
    B, T, D = x.shape
    assert T == CHUNK and B % nb == 0
    c_conv = wts[6].shape[-1]
    xs_rows = D // 2 // LANES
    blk3 = lambda rows, last: pl.BlockSpec((nb, rows, last), lambda i: (i, 0, 0))
    return pl.pallas_call(
        functools.partial(_sample_mixer_kernel, nb=nb),
        grid=(B // nb,),
        in_specs=[blk3(T, D), blk3(WINDOW, KV_W), blk3(WINDOW, KV_W), blk3(CONV_HIST, c_conv)]
                 + [_full(w.shape) for w in wts],
        out_specs=[blk3(T, D), blk3(T * xs_rows, LANES), blk3(T, REC_W),
                   pl.BlockSpec((1, REC_W, nb * T), lambda i: (i, 0, 0)), pl.BlockSpec((1, LANES), lambda i: (0, 0)),
                   blk3(WINDOW, KV_W), blk3(WINDOW, KV_W), blk3(CONV_HIST, c_conv)],
        out_shape=[
            jax.ShapeDtypeStruct((B, T, D), F32),
            jax.ShapeDtypeStruct((B, T * xs_rows, LANES), U32),
            jax.ShapeDtypeStruct((B, T, REC_W), F32),
            jax.ShapeDtypeStruct((B // nb, REC_W, nb * T), F32),
            jax.ShapeDtypeStruct((1, LANES), F32),
            jax.ShapeDtypeStruct((B, WINDOW, KV_W), F32),
            jax.ShapeDtypeStruct((B, WINDOW, KV_W), F32),
            jax.ShapeDtypeStruct((B, CONV_HIST, c_conv), F32),
        ],
        scratch_shapes=[
            pltpu.VMEM((SUBLANES, CONV_PAD + CHUNK, c_conv), F32),
            pltpu.VMEM((nb * CHUNK, Q_W), F32),
            pltpu.VMEM((nb * CHUNK, c_conv), F32),
            pltpu.VMEM((1, LANES), F32),
        ],
        compiler_params=pltpu.CompilerParams(
            dimension_semantics=("arbitrary",), vmem_limit_bytes=VMEM_LIMIT),
        name="sample_mixer",
    )(x, ck, cv, sc, *wts)


def _dispatch_kernel(meta_ref, dest_ref, hp_ref, hs_ref, xs_ref, zbuf, sem, zsem, *, td, n, blk, n_blocks, ntp):
    i = pl.program_id(0)

    def issue_from(h_ref):
        def body(r8, carry):
            for uu in range(DMA_UNROLL):
                r = r8 * DMA_UNROLL + uu
                src = h_ref.at[pl.ds(pl.multiple_of(r * n, n), n)]
                for kk in range(TOP_K):
                    d = pl.multiple_of(dest_ref[kk, 0, 0, r], n)
                    pltpu.make_async_copy(src, xs_ref.at[pl.ds(d, n)], sem).start(priority=kk % 2)
            return carry
        lax.fori_loop(0, td // DMA_UNROLL, body, 0)

    @pl.when(i < ntp)
    def _():
        issue_from(hp_ref)

    @pl.when(i >= ntp)
    def _():
        issue_from(hs_ref)

    def pad_pass(act):
        def pad_expert(e, carry):
            start = meta_ref[e]
            head = meta_ref[N_EXPERTS + e]
            body = meta_ref[2 * N_EXPERTS + e]

            @pl.when(head == 1)
            def _():
                act(pltpu.make_async_copy(zbuf.at[pl.ds(0, n)], xs_ref.at[pl.ds(pl.multiple_of(start * n, n), n)], zsem))
            bit = blk // 2
            while bit >= 2:
                off = pl.multiple_of((start + head + (body // (2 * bit)) * (2 * bit)) * n, 2 * n)

                @pl.when((body // bit) % 2 == 1)
                def _(bit=bit, off=off):
                    act(pltpu.make_async_copy(zbuf.at[pl.ds(0, bit * n)], xs_ref.at[pl.ds(off, bit * n)], zsem))
                bit //= 2
            return carry

        lax.fori_loop(0, N_EXPERTS, pad_expert, 0)

        def pad_block(j, carry):
            act(pltpu.make_async_copy(zbuf, xs_ref.at[pl.ds(pl.multiple_of(j * (blk * n), blk * n), blk * n)], zsem))
            return carry

        lax.fori_loop(meta_ref[3 * N_EXPERTS], n_blocks, pad_block, 0)

    @pl.when(i == pl.num_programs(0) - 1)
    def _():
        zbuf[...] = jnp.zeros_like(zbuf)
        pad_pass(lambda cp: cp.start())
        pad_pass(lambda cp: cp.wait())

    for _ in range(TOP_K):
        pltpu.make_async_copy(hp_ref, xs_ref.at[pl.ds(0, td * n)], sem).wait()


def _dispatch(hp, hs, dest_rows, meta, n_blocks, blk, td):
    rows_p, rows_s = hp.shape[0], hs.shape[0]
    n_tok = dest_rows.shape[1]
    n = (rows_p + rows_s) // n_tok
    ntp, nts = rows_p // (td * n), rows_s // (td * n)
    dest3 = dest_rows.reshape(TOP_K, ntp + nts, 1, td)
    grid_spec = pltpu.PrefetchScalarGridSpec(
        num_scalar_prefetch=1,
        grid=(ntp + nts,),
        in_specs=[
            pl.BlockSpec((TOP_K, 1, 1, td), lambda i, m: (0, i, 0, 0), memory_space=pltpu.SMEM),
            pl.BlockSpec((td * n, LANES), lambda i, m: (jnp.minimum(i, ntp - 1), 0)),
            pl.BlockSpec((td * n, LANES), lambda i, m: (jnp.maximum(i - ntp, 0), 0)),
        ],
        out_specs=pl.BlockSpec(memory_space=pl.ANY),
        scratch_shapes=[pltpu.VMEM((blk * n, LANES), U32), pltpu.SemaphoreType.DMA(()), pltpu.SemaphoreType.DMA(())],
    )
    return pl.pallas_call(
        functools.partial(_dispatch_kernel, td=td, n=n, blk=blk, n_blocks=n_blocks, ntp=ntp),
        grid_spec=grid_spec,
        out_shape=jax.ShapeDtypeStruct((n_blocks * blk * n, LANES), U32),
        compiler_params=pltpu.CompilerParams(
            dimension_semantics=("arbitrary",), vmem_limit_bytes=VMEM_LIMIT),
        name="moe_dispatch",
    )(meta, dest3, hp, hs)


def _expert_kernel(block_e_ref, n_used_ref, x_ref, wg_ref, wu_ref, wd_ref, y_ref, wgb, wub, wdb, *, blk):
    i = pl.program_id(0)
    used = i < n_used_ref[0]
    new_expert = (i == 0) | (block_e_ref[i] != block_e_ref[jnp.maximum(i - 1, 0)])

    @pl.when(used & new_expert)
    def _():
        wgb[...] = wg_ref[0].astype(BF16)
        wub[...] = wu_ref[0].astype(BF16)
        wdb[...] = wd_ref[0].astype(BF16)

    @pl.when(used)
    def _():
        lo, hi = _unpack_rows(_load_tile_rows(x_ref, (), blk, x_ref.shape[0] // blk))
        half = lo.shape[-1]
        g = _dot(lo, wgb[0:half, :]) + _dot(hi, wgb[half:, :])
        u = _dot(lo, wub[0:half, :]) + _dot(hi, wub[half:, :])
        a = (g * jax.nn.sigmoid(g) * u).astype(BF16)
        _store_tile_rows(y_ref, (), _dot(a, wdb[...]))

    @pl.when(jnp.logical_not(used))
    def _():
        y_ref[...] = jnp.zeros_like(y_ref)


def _experts(xs, block_e, n_used, wg, wu, wd, blk):
    n_blocks = block_e.shape[0]
    _, D, de = wg.shape
    xn = xs.shape[0] // (n_blocks * blk)
    yn = D // LANES
    grid_spec = pltpu.PrefetchScalarGridSpec(
        num_scalar_prefetch=2,
        grid=(n_blocks,),
        in_specs=[
            pl.BlockSpec((blk * xn, LANES), lambda i, be, nu: (i, 0)),
            pl.BlockSpec((1, D, de), lambda i, be, nu: (be[i], 0, 0)),
            pl.BlockSpec((1, D, de), lambda i, be, nu: (be[i], 0, 0)),
            pl.BlockSpec((1, de, D), lambda i, be, nu: (be[i], 0, 0)),
        ],
        out_specs=pl.BlockSpec((blk * yn, LANES), lambda i, be, nu: (i, 0)),
        scratch_shapes=[pltpu.VMEM((D, de), BF16), pltpu.VMEM((D, de), BF16), pltpu.VMEM((de, D), BF16)],
    )
    return pl.pallas_call(
        functools.partial(_expert_kernel, blk=blk),
        grid_spec=grid_spec,
        out_shape=jax.ShapeDtypeStruct((n_blocks * blk * yn, LANES), F32),
        compiler_params=pltpu.CompilerParams(
            dimension_semantics=("arbitrary",), vmem_limit_bytes=VMEM_LIMIT),
        name="moe_experts",
    )(block_e, n_used, xs, wg, wu, wd)


def _final_kernel(dest_cur, dest_nxt, x1_ref, rec_ref, p_ref, g_ple, w_gate, w_proj, ys_ref, y_ref,
                  buf_a, buf_b, sem, *, tf, n):
    i = pl.program_id(0)
    bufs = (buf_a, buf_b)

    def issue(dref, half, s):
        for r in range(tf):
            for kk in range(TOP_K):
                d = pl.multiple_of(dref[kk, 0, 0, half * tf + r], n)
                pltpu.make_async_copy(ys_ref.at[pl.ds(d, n)], bufs[s].at[kk, pl.ds(r * n, n)],
                                      sem.at[s]).start(priority=kk % 2)

    def drain(s):
        for kk in range(TOP_K):
            pltpu.make_async_copy(ys_ref.at[pl.ds(0, tf * n)], bufs[s].at[kk], sem.at[s]).wait()

    def compute(half, s):
        rows = slice(half * tf, (half + 1) * tf)
        rec = rec_ref[rows, :]
        y1 = _load_tile_rows(bufs[s], (0,), tf, n)
        y2 = _load_tile_rows(bufs[s], (1,), tf, n)
        x2 = x1_ref[rows, :] + (rec[:, 2:3] * y1 + rec[:, 3:4] * y2)
        gate = jax.nn.sigmoid(_dot(_rms(x2, g_ple[...]).astype(BF16), w_gate[...]))
        y_ref[rows, :] = x2 + gate * _dot(p_ref[rows, :].astype(BF16), w_proj[...])

    @pl.when(i == 0)
    def _():
        issue(dest_cur, 0, 0)

    drain(0)
    issue(dest_cur, 1, 1)
    compute(0, 0)
    drain(1)
    issue(dest_nxt, 0, 0)
    compute(1, 1)

    @pl.when(i == pl.num_programs(0) - 1)
    def _():
        drain(0)


def _final(x1, rec, p, dest_rows, ys, g_ple, w_gate, w_proj, tf):
    N, D = x1.shape
    n = D // LANES
    pair = 2 * tf
    assert N % pair == 0
    nt = N // pair
    dest3 = dest_rows.reshape(TOP_K, nt, 1, pair)
    smem_blk = lambda fn: pl.BlockSpec((TOP_K, 1, 1, pair), fn, memory_space=pltpu.SMEM)
    return pl.pallas_call(
        functools.partial(_final_kernel, tf=tf, n=n),
        grid=(nt,),
        in_specs=[
            smem_blk(lambda i: (0, i, 0, 0)),
            smem_blk(lambda i: (0, jnp.minimum(i + 1, nt - 1), 0, 0)),
            pl.BlockSpec((pair, D), lambda i: (i, 0)),
            pl.BlockSpec((pair, REC_W), lambda i: (i, 0)),
            pl.BlockSpec((pair, p.shape[-1]), lambda i: (i, 0)),
            _full(g_ple.shape), _full(w_gate.shape), _full(w_proj.shape),
            pl.BlockSpec(memory_space=pl.ANY),
        ],
        out_specs=pl.BlockSpec((pair, D), lambda i: (i, 0)),
        out_shape=jax.ShapeDtypeStruct((N, D), F32),
        scratch_shapes=[pltpu.VMEM((TOP_K, tf * n, LANES), F32), pltpu.VMEM((TOP_K, tf * n, LANES), F32),
                        pltpu.SemaphoreType.DMA((2,))],
        compiler_params=pltpu.CompilerParams(
            dimension_semantics=("arbitrary",), vmem_limit_bytes=VMEM_LIMIT),
        name="moe_combine_ple",
    )(dest3, dest3, x1, rec, p, g_ple, w_gate, w_proj, ys)


def _slots(rec, pstart, offset):
    e = rec[0:TOP_K, :].astype(I32)
    rank = rec[4:4 + TOP_K, :].astype(I32)
    onehot = e[None, :, :] == jnp.arange(N_EXPERTS, dtype=I32)[:, None, None]
    return rank + jnp.sum(jnp.where(onehot, (pstart + offset)[:, None, None], 0), axis=0)


def _bias_table(sinks):
    slopes = jnp.array([2.0 ** (-8.0 * (i + 1) / N_HEADS) for i in range(N_HEADS)], F32)
    qpos = WINDOW + jnp.arange(CHUNK, dtype=I32)
    col = jnp.arange(BAND_PAD, dtype=I32)
    dist = jnp.abs(qpos[:, None] - col[None, :]).astype(F32)
    core = slopes[:, None, None] * dist[None]
    sink = jnp.broadcast_to(-sinks.astype(F32)[:, None, None], core.shape)
    table = jnp.where(col == BAND, sink, jnp.where(col < BAND, core, MASKED))
    first_valid = jnp.array([WINDOW, WINDOW - CHUNK, 0], I32)
    table = jnp.where(col[None, None, None, :] < first_valid[:, None, None, None], MASKED, table[None])
    return table.reshape(3, N_HEADS * CHUNK, BAND_PAD)


def kernel(x_prompt, x_sample, p_prompt, p_sample, cache_k, cache_v, state_conv, g_mix, w_in, g_q, g_k, sinks,
           w_dw, b_dw, g_cn, b_cn, w_pw, b_pw, g_oa, g_oc, w_out, g_ffn, w_coarse, b_coarse, w_fine, b_fine,
           w_e_gate, w_e_up, w_e_down, g_ple, w_ple_gate, w_ple_proj):
    assert g_mix.shape[0] == 1
    l = 0
    B, T, D = x_prompt.shape
    DB, DS, _ = x_sample.shape
    assert cache_k.shape[2] == WINDOW and DS == CHUNK
    Np, Ns = B * T, DB * DS
    tt, td, tf = _tiles(T, Np, Ns)
    blk = EXPERT_BLOCK

    row = lambda a: a[l].reshape(1, -1)
    perm = jnp.concatenate([jnp.concatenate([jnp.arange(HEAD_DIM) + m * HEAD_DIM,
                                             jnp.arange(HEAD_DIM) + (GQ + m) * HEAD_DIM]) for m in range(GQ)])
    w_in_p = jnp.concatenate([w_in[l][:, perm], w_in[l][:, Q_W:]], axis=1).astype(BF16)
    w_out_p = jnp.concatenate([w_out[l][perm, :], w_out[l][Q_W:, :]], axis=0).astype(BF16)
    g_oa_p = g_oa[l][perm].reshape(1, -1)
    gain_qk = jnp.concatenate([jnp.tile(g_q[l] * (HEAD_DIM ** -0.5), N_HEADS), jnp.tile(g_k[l], N_KV_HEADS)]).reshape(1, -1)
    blk_id = jnp.arange(2 * LANES) // HEAD_DIM
    bd = jnp.where(blk_id[:, None] == blk_id[None, :], 1.0 / HEAD_DIM, 0.0).astype(BF16)
    w_r = jnp.concatenate(
        [w_coarse[l], jnp.transpose(w_fine[l], (1, 0, 2)).reshape(D, N_EXPERTS),
         jnp.zeros((D, LANES - N_GROUPS - N_EXPERTS), F32)], axis=1)
    w_rh = w_r.astype(BF16)
    w_rl = (w_r - w_rh.astype(F32)).astype(BF16)
    w_r2 = jnp.concatenate([w_rh, w_rl], axis=1)
    b_r = jnp.concatenate(
        [b_coarse[l], b_fine[l].reshape(-1), jnp.zeros((LANES - N_GROUPS - N_EXPERTS,), F32)]).reshape(1, LANES)
    bias = _bias_table(sinks[l])
    w_dw_rep = jnp.repeat(w_dw[l], SUBLANES, axis=0)

    def mixer_weights(bias_tbl):
        return (row(g_mix), w_in_p, bd, gain_qk, bias_tbl, w_dw_rep, row(b_dw),
                row(g_cn), row(b_cn), w_pw[l].astype(BF16), row(b_pw), g_oa_p, row(g_oc),
                w_out_p, row(g_ffn), w_r2, w_rh, b_r)

    x1p, hp, recp, rectp, cntp, nkp, nvp, ncp = _prompt_mixer(x_prompt, mixer_weights(bias), tt)
    nb = 4 if DB % 4 == 0 else 1
    x1s, hs, recs, rects, cnts, nks, nvs, ncs = _sample_mixer(
        x_sample, cache_k[l].reshape(DB, WINDOW, KV_W), cache_v[l].reshape(DB, WINDOW, KV_W), state_conv[l],
        mixer_weights(bias[2:3]), nb)

    recp, recs = recp.reshape(Np, REC_W), recs.reshape(Ns, REC_W)
    rectp = jnp.transpose(rectp, (1, 0, 2)).reshape(REC_W, Np)
    rects = jnp.transpose(rects, (1, 0, 2)).reshape(REC_W, Ns)
    cnt_p = cntp[0, :N_EXPERTS].astype(I32)
    cnt = cnt_p + cnts[0, :N_EXPERTS].astype(I32)
    n_blocks = ((Np + Ns) * TOP_K + N_EXPERTS * (blk - 1)) // blk
    padded = (cnt + blk - 1) // blk * blk
    pend = jnp.cumsum(padded)
    pstart = pend - padded
    dest_p = _slots(rectp, pstart, jnp.zeros_like(cnt_p))
    dest_s = _slots(rects, pstart, cnt_p)
    block_e = jnp.minimum(
        jnp.sum((pend[None, :] <= (jnp.arange(n_blocks, dtype=I32) * blk)[:, None]).astype(I32), axis=1),
        N_EXPERTS - 1).astype(I32)
    n_used = (pend[-1:] // blk).astype(I32)
    pad_start = pstart + cnt
    pad_head = pad_start % 2
    meta = jnp.concatenate([pad_start, pad_head, padded - cnt - pad_head, n_used]).astype(I32)

    xn = D // 2 // LANES
    yn = D // LANES
    xs = _dispatch(hp.reshape(Np * xn, LANES), hs.reshape(Ns * xn, LANES),
                   jnp.concatenate([dest_p, dest_s], axis=1) * xn, meta, n_blocks, blk, td)
    ys = _experts(xs, block_e, n_used, w_e_gate[l], w_e_up[l], w_e_down[l], blk)

    gp = row(g_ple)
    w_gate = w_ple_gate[l].astype(BF16)
    w_proj = w_ple_proj[l].astype(BF16)
    yp = _final(x1p.reshape(Np, D), recp, p_prompt[l].reshape(Np, -1), dest_p * yn, ys, gp, w_gate, w_proj, tf)
    ysm = _final(x1s.reshape(Ns, D), recs, p_sample[l].reshape(Ns, -1), dest_s * yn, ys, gp, w_gate, w_proj, tf)

    kv5 = lambda a, nbat: a.reshape(1, nbat, WINDOW, N_KV_HEADS, HEAD_DIM)
    return (yp.reshape(B, T, D), ysm.reshape(DB, DS, D),
            kv5(nkp, B), kv5(nvp, B), ncp[None],
            kv5(nks, DB), kv5(nvs, DB), ncs[None])
```

```python
import functools

import jax
import jax.numpy as jnp
from jax import lax
from jax.experimental import pallas as pl
from jax.experimental.pallas import tpu as pltpu

CHUNK = 64
HEAD_DIM = 64
N_HEADS = 8
N_KV_HEADS = 2
GQ = N_HEADS // N_KV_HEADS
Q_W = N_HEADS * HEAD_DIM
KV_W = N_KV_HEADS * HEAD_DIM
WINDOW = 128
BAND = WINDOW + CHUNK
BAND_PAD = 256
CONV_WIDTH = 31
CONV_HIST = CONV_WIDTH - 1
CONV_PAD = 32
CONV_BLOCK = 32
SUBLANES = 8
LANES = 128
N_GROUPS = 4
EXPERTS_PER_GROUP = 8
N_EXPERTS = N_GROUPS * EXPERTS_PER_GROUP
TOP_K = 2
EXPERT_BLOCK = 512
EPS = 1e-6
REC_W = 8
MASKED = 1e30
DMA_UNROLL = 8

F32 = jnp.float32
BF16 = jnp.bfloat16
U32 = jnp.uint32
I32 = jnp.int32

VMEM_LIMIT = 56 * 1024 * 1024


def _tiles(n_prompt_seq, n_prompt, n_sample):
    tt = next((t for t in (512, 256) if n_prompt_seq % t == 0), n_prompt_seq)
    td = next(t for t in (1024, 512, 256, 128, 64) if n_prompt % t == 0 and n_sample % t == 0)
    tf = min(td, 512)
    return tt, td, tf


def _rms(xf, g):
    return xf * lax.rsqrt(jnp.mean(xf * xf, axis=-1, keepdims=True) + EPS) * g


def _dot(a, b):
    return jnp.dot(a, b, preferred_element_type=F32)


def _dot_nt(a, b):
    return lax.dot_general(a, b, (((1,), (1,)), ((), ())), preferred_element_type=F32)


def _store_tile_rows(ref, lead, value):
    rows, width = value.shape
    n = width // LANES
    for c in range(n):
        ref[lead + (pl.ds(c, rows, stride=n), slice(None))] = value[:, c * LANES:(c + 1) * LANES]


def _load_tile_rows(ref, lead, rows, n):
    return jnp.concatenate([ref[lead + (pl.ds(c, rows, stride=n), slice(None))] for c in range(n)], axis=1)


def _project(x, g_mix, w_in_ref, bd_ref, gain_qk):
    xn = _rms(x, g_mix).astype(BF16)
    z = _dot(xn, w_in_ref[...])
    qk = z[:, 0:Q_W + KV_W]
    sq = (qk * qk).astype(BF16)
    bd = bd_ref[...]
    two = 2 * LANES
    ms = jnp.concatenate(
        [_dot(sq[:, 0:two], bd), _dot(sq[:, two:2 * two], bd), _dot(sq[:, 2 * two:], bd[0:KV_W, 0:KV_W])], axis=1)
    qkn = qk * lax.rsqrt(ms + EPS) * gain_qk
    v = z[:, Q_W + KV_W:Q_W + 2 * KV_W]
    c0 = Q_W + 2 * KV_W
    c_conv = (z.shape[1] - c0) // 2
    u = z[:, c0:c0 + c_conv] * jax.nn.sigmoid(z[:, c0 + c_conv:])
    return qkn, v, u


def _attend_chunk(q_groups, k_band, v_band, bias):
    lane = lax.broadcasted_iota(I32, (CHUNK, LANES), 1)
    low = lane < HEAD_DIM
    zero = jnp.zeros((CHUNK, LANES), F32)
    q_all = jnp.concatenate(
        [jnp.where(low, qg, zero) for qg in q_groups] + [jnp.where(low, zero, qg) for qg in q_groups],
        axis=0).astype(BF16)
    pad = jnp.zeros((BAND_PAD - BAND, LANES), BF16)
    s = _dot_nt(q_all, jnp.concatenate([k_band, pad], axis=0)) - bias
    m = jnp.max(s, axis=-1, keepdims=True)
    e = jnp.exp(s - m)
    denom = jnp.sum(e, axis=-1, keepdims=True)
    o = _dot(e.astype(BF16), jnp.concatenate([v_band, pad], axis=0)) * (1.0 / denom)
    half = GQ * CHUNK
    return [jnp.where(low, o[g * CHUNK:(g + 1) * CHUNK, :], o[half + g * CHUNK:half + (g + 1) * CHUNK, :])
            for g in range(GQ)]


def _shift_copies(ush, rows):
    for n in range(1, SUBLANES):
        ush[n, 0:rows, :] = ush[0, n:n + rows, :]


def _conv_rows(ush, r0, rows, w_dw_ref, b_dw):
    ch = b_dw.shape[-1]
    groups = rows // SUBLANES
    acc = jnp.broadcast_to(b_dw, (groups, SUBLANES, ch))
    for j in range(CONV_WIDTH):
        off = CONV_PAD - CONV_HIST + j
        taps = ush[off % SUBLANES, pl.ds(r0 + off - off % SUBLANES, rows), :].reshape(groups, SUBLANES, ch)
        acc = acc + w_dw_ref[j * SUBLANES:(j + 1) * SUBLANES, :][None] * taps
    return acc.reshape(rows, ch)


def _finish(x, oa, conv, refs):
    (g_cn, b_cn, w_pw, b_pw, g_oa, g_oc, w_out, g_ffn, w_r2, w_rh, b_r) = refs
    mu = jnp.mean(conv, axis=-1, keepdims=True)
    cen = conv - mu
    var = jnp.mean(cen * cen, axis=-1, keepdims=True)
    ln = cen * lax.rsqrt(var + EPS) * g_cn[...] + b_cn[...]
    act = (ln * jax.nn.sigmoid(ln)).astype(BF16)
    c = _dot(act, w_pw[...]) + b_pw[...]
    half = oa.shape[-1]
    mixed = (_dot(_rms(oa, g_oa[...]).astype(BF16), w_out[0:half, :])
             + _dot(_rms(c, g_oc[...]).astype(BF16), w_out[half:, :]))
    x1 = x + mixed
    h = _rms(x1, g_ffn[...])
    h_hi = h.astype(BF16)
    h_lo = (h - h_hi.astype(F32)).astype(BF16)
    a = _dot(h_hi, w_r2[...])
    logits = a[:, 0:LANES] + a[:, LANES:] + _dot(h_lo, w_rh[...]) + b_r[...]
    return x1, h, logits


def _pack_rows(h):
    half = h.shape[-1] // 2
    hb = h.astype(BF16).astype(F32)
    lo = lax.bitcast_convert_type(hb[:, 0:half], U32)
    hi = lax.bitcast_convert_type(hb[:, half:], U32)
    return (lo >> 16) | (hi & jnp.uint32(0xFFFF0000))


def _unpack_rows(w, dtype):
    lo = lax.bitcast_convert_type(w << 16, F32).astype(dtype)
    hi = lax.bitcast_convert_type(w & jnp.uint32(0xFFFF0000), F32).astype(dtype)
    return lo, hi


def _route(logits, cnt_ref):
    rows = logits.shape[0]
    lane = lax.broadcasted_iota(I32, logits.shape, 1).astype(F32)
    big = float(LANES)
    ninf = -jnp.inf
    lc = jnp.where(lane < N_GROUPS, logits, ninf)
    mc = jnp.max(lc, axis=-1, keepdims=True)
    grp = jnp.min(jnp.where(lc == mc, lane, big), axis=-1, keepdims=True)
    g1 = 1.0 / jnp.sum(jnp.exp(lc - mc), axis=-1, keepdims=True)
    lo = N_GROUPS + grp * EXPERTS_PER_GROUP
    lf = jnp.where((lane >= lo) & (lane < lo + EXPERTS_PER_GROUP), logits, ninf)
    t1 = jnp.max(lf, axis=-1, keepdims=True)
    i1 = jnp.min(jnp.where(lf == t1, lane, big), axis=-1, keepdims=True)
    lf2 = jnp.where(lane == i1, ninf, lf)
    t2 = jnp.max(lf2, axis=-1, keepdims=True)
    i2 = jnp.min(jnp.where(lf2 == t2, lane, big), axis=-1, keepdims=True)
    e2x = jnp.exp(t2 - t1)
    inv = 1.0 / (1.0 + e2x)
    w1 = g1 * inv
    w2 = g1 * (e2x * inv)
    e1 = i1 - N_GROUPS
    e2 = i2 - N_GROUPS
    oh1 = (lane == e1).astype(F32)
    oh2 = (lane == e2).astype(F32)
    oh = oh1 + oh2
    ri = lax.broadcasted_iota(I32, (rows, rows), 0)
    ci = lax.broadcasted_iota(I32, (rows, rows), 1)
    tri = (ci < ri).astype(BF16)
    tot = _dot(tri, oh.astype(BF16)) + cnt_ref[...]
    r1 = jnp.sum(oh1 * tot, axis=-1, keepdims=True)
    r2 = jnp.sum(oh2 * tot, axis=-1, keepdims=True)
    cnt_ref[...] = cnt_ref[...] + jnp.sum(oh, axis=0, keepdims=True)
    rec = jnp.where(lane == 0, e1, 0.0)
    rec = jnp.where(lane == 1, e2, rec)
    rec = jnp.where(lane == 2, w1, rec)
    rec = jnp.where(lane == 3, w2, rec)
    rec = jnp.where(lane == 4, r1, rec)
    rec = jnp.where(lane == 5, r2, rec)
    return rec[:, 0:REC_W], jnp.transpose(rec)[0:REC_W, :]


def _prompt_mixer_kernel(x_ref, g_mix, w_in, bd, gain_qk, bias_ref, w_dw, b_dw,
                         g_cn, b_cn, w_pw, b_pw, g_oa, g_oc, w_out, g_ffn, w_r2, w_rh, b_r,
                         x1_ref, h_ref, rec_ref, rect_ref, cnt_out, nk_ref, nv_ref, nc_ref,
                         kx, vx, ush, oa_s, conv_s, cnt_s, *, tt):
    b = pl.program_id(0)
    t = pl.program_id(1)
    n_chunks = tt // CHUNK

    @pl.when((b == 0) & (t == 0))
    def _():
        cnt_s[...] = jnp.zeros_like(cnt_s)

    @pl.when(t == 0)
    def _():
        kx[0:WINDOW, :] = jnp.zeros((WINDOW, LANES), BF16)
        vx[0:WINDOW, :] = jnp.zeros((WINDOW, LANES), BF16)
        ush[0, 0:CONV_PAD, :] = jnp.zeros((CONV_PAD, ush.shape[-1]), F32)

    x = x_ref[0]
    qkn, v, u = _project(x, g_mix[...], w_in, bd, gain_qk[...])
    k = qkn[:, Q_W:]
    kx[WINDOW:WINDOW + tt, :] = k.astype(BF16)
    vx[WINDOW:WINDOW + tt, :] = v.astype(BF16)
    ush[0, CONV_PAD:CONV_PAD + tt, :] = u
    _shift_copies(ush, tt + CONV_PAD - SUBLANES)

    for c in range(n_chunks):
        variant = jnp.minimum(t * n_chunks + c, WINDOW // CHUNK)
        rows = slice(c * CHUNK, (c + 1) * CHUNK)
        q_groups = [qkn[rows, m * LANES:(m + 1) * LANES] for m in range(GQ)]
        o_groups = _attend_chunk(q_groups, kx[c * CHUNK:c * CHUNK + BAND, :], vx[c * CHUNK:c * CHUNK + BAND, :],
                                 bias_ref[variant])
        for m in range(GQ):
            oa_s[rows, m * LANES:(m + 1) * LANES] = o_groups[m]

    for rb in range(tt // CONV_BLOCK):
        conv_s[rb * CONV_BLOCK:(rb + 1) * CONV_BLOCK, :] = _conv_rows(ush, rb * CONV_BLOCK, CONV_BLOCK, w_dw, b_dw[...])

    x1, hh, logits = _finish(x, oa_s[...], conv_s[...],
                             (g_cn, b_cn, w_pw, b_pw, g_oa, g_oc, w_out, g_ffn, w_r2, w_rh, b_r))
    x1_ref[0] = x1
    _store_tile_rows(h_ref, (0,), _pack_rows(hh))
    rec_ref[0], rect_ref[0] = _route(logits, cnt_s)
    cnt_out[...] = cnt_s[...]

    nk_ref[0] = k[tt - WINDOW:, :]
    nv_ref[0] = v[tt - WINDOW:, :]
    nc_ref[0] = u[tt - CONV_HIST:, :]

    kx[0:WINDOW, :] = kx[tt:tt + WINDOW, :]
    vx[0:WINDOW, :] = vx[tt:tt + WINDOW, :]
    ush[0, 0:CONV_PAD, :] = ush[0, tt:tt + CONV_PAD, :]


def _sample_mixer_kernel(x_ref, ck_ref, cv_ref, sc_ref, g_mix, w_in, bd, gain_qk, bias_ref, w_dw, b_dw,
                         g_cn, b_cn, w_pw, b_pw, g_oa, g_oc, w_out, g_ffn, w_r2, w_rh, b_r,
                         x1_ref, h_ref, rec_ref, rect_ref, cnt_out, nk_ref, nv_ref, nc_ref,
                         ush, oa_s, conv_s, cnt_s, *, nb):
    i = pl.program_id(0)

    @pl.when(i == 0)
    def _():
        cnt_s[...] = jnp.zeros_like(cnt_s)

    rows_all = nb * CHUNK
    x = x_ref[...].reshape(rows_all, x_ref.shape[-1])
    qkn, v, u = _project(x, g_mix[...], w_in, bd, gain_qk[...])
    k = qkn[:, Q_W:]
    for j in range(nb):
        rows = slice(j * CHUNK, (j + 1) * CHUNK)
        ck = ck_ref[j]
        cv = cv_ref[j]
        k_band = jnp.concatenate([ck, k[rows, :]], axis=0)
        v_band = jnp.concatenate([cv, v[rows, :]], axis=0)
        q_groups = [qkn[rows, m * LANES:(m + 1) * LANES] for m in range(GQ)]
        o_groups = _attend_chunk(q_groups, k_band.astype(BF16), v_band.astype(BF16), bias_ref[0])
        for m in range(GQ):
            oa_s[rows, m * LANES:(m + 1) * LANES] = o_groups[m]
        ush[0, 0:CONV_PAD, :] = jnp.zeros((CONV_PAD, ush.shape[-1]), F32)
        ush[0, CONV_PAD - CONV_HIST:CONV_PAD, :] = sc_ref[j]
        ush[0, CONV_PAD:CONV_PAD + CHUNK, :] = u[rows, :]
        _shift_copies(ush, CHUNK + CONV_PAD - SUBLANES)
        for rb in range(CHUNK // CONV_BLOCK):
            r0 = rb * CONV_BLOCK
            conv_s[j * CHUNK + r0:j * CHUNK + r0 + CONV_BLOCK, :] = _conv_rows(ush, r0, CONV_BLOCK, w_dw, b_dw[...])
        nk_ref[j] = k_band[CHUNK:, :]
        nv_ref[j] = v_band[CHUNK:, :]
        nc_ref[j] = ush[0, CONV_PAD + CHUNK - CONV_HIST:CONV_PAD + CHUNK, :]

    x1, hh, logits = _finish(x, oa_s[...], conv_s[...],
                             (g_cn, b_cn, w_pw, b_pw, g_oa, g_oc, w_out, g_ffn, w_r2, w_rh, b_r))
    x1_ref[...] = x1.reshape(x1_ref.shape)
    packed = _pack_rows(hh)
    for j in range(nb):
        _store_tile_rows(h_ref, (j,), packed[j * CHUNK:(j + 1) * CHUNK, :])
    rec, rec_t = _route(logits, cnt_s)
    rec_ref[...] = rec.reshape(rec_ref.shape)
    rect_ref[0] = rec_t
    cnt_out[...] = cnt_s[...]


def _full(shape):
    nd = len(shape)
    return pl.BlockSpec(shape, lambda *_: (0,) * nd)


def _prompt_mixer(x, wts, tt):
    B, T, D = x.shape
    c_conv = wts[6].shape[-1]
    xs_rows = D // 2 // LANES
    nt = T // tt
    tok = lambda rows, last: pl.BlockSpec((1, rows, last), lambda b, t: (b, t, 0))
    per_batch = lambda rows, last: pl.BlockSpec((1, rows, last), lambda b, t: (b, 0, 0))
    return pl.pallas_call(
        functools.partial(_prompt_mixer_kernel, tt=tt),
        grid=(B, nt),
        in_specs=[tok(tt, D)] + [_full(w.shape) for w in wts],
        out_specs=[tok(tt, D), tok(tt * xs_rows, LANES), tok(tt, REC_W),
                   pl.BlockSpec((1, REC_W, tt), lambda b, t: (b, 0, t)), pl.BlockSpec((1, LANES), lambda b, t: (0, 0)),
                   per_batch(WINDOW, KV_W), per_batch(WINDOW, KV_W), per_batch(CONV_HIST, c_conv)],
        out_shape=[
            jax.ShapeDtypeStruct((B, T, D), F32),
            jax.ShapeDtypeStruct((B, T * xs_rows, LANES), U32),
            jax.ShapeDtypeStruct((B, T, REC_W), F32),
            jax.ShapeDtypeStruct((B, REC_W, T), F32),
            jax.ShapeDtypeStruct((1, LANES), F32),
            jax.ShapeDtypeStruct((B, WINDOW, KV_W), F32),
            jax.ShapeDtypeStruct((B, WINDOW, KV_W), F32),
            jax.ShapeDtypeStruct((B, CONV_HIST, c_conv), F32),
        ],
        scratch_shapes=[
            pltpu.VMEM((WINDOW + tt, LANES), BF16),
            pltpu.VMEM((WINDOW + tt, LANES), BF16),
            pltpu.VMEM((SUBLANES, CONV_PAD + tt, c_conv), F32),
            pltpu.VMEM((tt, Q_W), F32),
            pltpu.VMEM((tt, c_conv), F32),
            pltpu.VMEM((1, LANES), F32),
        ],
        compiler_params=pltpu.CompilerParams(
            dimension_semantics=("arbitrary", "arbitrary"), vmem_limit_bytes=VMEM_LIMIT),
        name="prompt_mixer",
    )(x, *wts)


def _sample_mixer(x, ck, cv, sc, wts, nb):
    B, T, D = x.shape
    assert T == CHUNK and B % nb == 0
    c_conv = wts[6].shape[-1]
    xs_rows = D // 2 // LANES
    blk3 = lambda rows, last: pl.BlockSpec((nb, rows, last), lambda i: (i, 0, 0))
    return pl.pallas_call(
        functools.partial(_sample_mixer_kernel, nb=nb),
        grid=(B // nb,),
        in_specs=[blk3(T, D), blk3(WINDOW, KV_W), blk3(WINDOW, KV_W), blk3(CONV_HIST, c_conv)]
                 + [_full(w.shape) for w in wts],
        out_specs=[blk3(T, D), blk3(T * xs_rows, LANES), blk3(T, REC_W),
                   pl.BlockSpec((1, REC_W, nb * T), lambda i: (i, 0, 0)), pl.BlockSpec((1, LANES), lambda i: (0, 0)),
                   blk3(WINDOW, KV_W), blk3(WINDOW, KV_W), blk3(CONV_HIST, c_conv)],
        out_shape=[
            jax.ShapeDtypeStruct((B, T, D), F32),
            jax.ShapeDtypeStruct((B, T * xs_rows, LANES), U32),
            jax.ShapeDtypeStruct((B, T, REC_W), F32),
            jax.ShapeDtypeStruct((B // nb, REC_W, nb * T), F32),
            jax.ShapeDtypeStruct((1, LANES), F32),
            jax.ShapeDtypeStruct((B, WINDOW, KV_W), F32),
            jax.ShapeDtypeStruct((B, WINDOW, KV_W), F32),
            jax.ShapeDtypeStruct((B, CONV_HIST, c_conv), F32),
        ],
        scratch_shapes=[
            pltpu.VMEM((SUBLANES, CONV_PAD + CHUNK, c_conv), F32),
            pltpu.VMEM((nb * CHUNK, Q_W), F32),
            pltpu.VMEM((nb * CHUNK, c_conv), F32),
            pltpu.VMEM((1, LANES), F32),
        ],
        compiler_params=pltpu.CompilerParams(
            dimension_semantics=("arbitrary",), vmem_limit_bytes=VMEM_LIMIT),
        name="sample_mixer",
    )(x, ck, cv, sc, *wts)


def _dispatch_kernel(meta_ref, dest_ref, hp_ref, hs_ref, xs_ref, zbuf, sem, zsem, *, td, n, blk, n_blocks, ntp):
    i = pl.program_id(0)

    def issue_from(h_ref):
        def body(r8, carry):
            for uu in range(DMA_UNROLL):
                r = r8 * DMA_UNROLL + uu
                src = h_ref.at[pl.ds(pl.multiple_of(r * n, n), n)]
                for kk in range(TOP_K):
                    d = pl.multiple_of(dest_ref[kk, 0, 0, r], n)
                    pltpu.make_async_copy(src, xs_ref.at[pl.ds(d, n)], sem).start(priority=kk % 2)
            return carry
        lax.fori_loop(0, td // DMA_UNROLL, body, 0)

    @pl.when(i < ntp)
    def _():
        issue_from(hp_ref)

    @pl.when(i >= ntp)
    def _():
        issue_from(hs_ref)

    def pad_pass(act):
        def pad_expert(e, carry):
            start = meta_ref[e]
            head = meta_ref[N_EXPERTS + e]
            body = meta_ref[2 * N_EXPERTS + e]

            @pl.when(head == 1)
            def _():
                act(pltpu.make_async_copy(zbuf.at[pl.ds(0, n)], xs_ref.at[pl.ds(pl.multiple_of(start * n, n), n)], zsem))
            bit = blk // 2
            while bit >= 2:
                off = pl.multiple_of((start + head + (body // (2 * bit)) * (2 * bit)) * n, 2 * n)

                @pl.when((body // bit) % 2 == 1)
                def _(bit=bit, off=off):
                    act(pltpu.make_async_copy(zbuf.at[pl.ds(0, bit * n)], xs_ref.at[pl.ds(off, bit * n)], zsem))
                bit //= 2
            return carry

        lax.fori_loop(0, N_EXPERTS, pad_expert, 0)

        def pad_block(j, carry):
            act(pltpu.make_async_copy(zbuf, xs_ref.at[pl.ds(pl.multiple_of(j * (blk * n), blk * n), blk * n)], zsem))
            return carry

        lax.fori_loop(meta_ref[3 * N_EXPERTS], n_blocks, pad_block, 0)

    @pl.when(i == pl.num_programs(0) - 1)
    def _():
        zbuf[...] = jnp.zeros_like(zbuf)
        pad_pass(lambda cp: cp.start())
        pad_pass(lambda cp: cp.wait())

    for _ in range(TOP_K):
        pltpu.make_async_copy(hp_ref, xs_ref.at[pl.ds(0, td * n)], sem).wait()


def _dispatch(hp, hs, dest_rows, meta, n_blocks, blk, td):
    rows_p, rows_s = hp.shape[0], hs.shape[0]
    n_tok = dest_rows.shape[1]
    n = (rows_p + rows_s) // n_tok
    ntp, nts = rows_p // (td * n), rows_s // (td * n)
    dest3 = dest_rows.reshape(TOP_K, ntp + nts, 1, td)
    grid_spec = pltpu.PrefetchScalarGridSpec(
        num_scalar_prefetch=1,
        grid=(ntp + nts,),
        in_specs=[
            pl.BlockSpec((TOP_K, 1, 1, td), lambda i, m: (0, i, 0, 0), memory_space=pltpu.SMEM),
            pl.BlockSpec((td * n, LANES), lambda i, m: (jnp.minimum(i, ntp - 1), 0)),
            pl.BlockSpec((td * n, LANES), lambda i, m: (jnp.maximum(i - ntp, 0), 0)),
        ],
        out_specs=pl.BlockSpec(memory_space=pl.ANY),
        scratch_shapes=[pltpu.VMEM((blk * n, LANES), U32), pltpu.SemaphoreType.DMA(()), pltpu.SemaphoreType.DMA(())],
    )
    return pl.pallas_call(
        functools.partial(_dispatch_kernel, td=td, n=n, blk=blk, n_blocks=n_blocks, ntp=ntp),
        grid_spec=grid_spec,
        out_shape=jax.ShapeDtypeStruct((n_blocks * blk * n, LANES), U32),
        compiler_params=pltpu.CompilerParams(
            dimension_semantics=("arbitrary",), vmem_limit_bytes=VMEM_LIMIT),
        name="moe_dispatch",
    )(meta, dest3, hp, hs)


def _expert_kernel(block_e_ref, n_used_ref, x_ref, wg_ref, wu_ref, wd_ref, y_ref, wgb, wub, wdb, *, blk):
    i = pl.program_id(0)
    used = i < n_used_ref[0]
    new_expert = (i == 0) | (block_e_ref[i] != block_e_ref[jnp.maximum(i - 1, 0)])

    @pl.when(used & new_expert)
    def _():
        wgb[...] = wg_ref[0].astype(BF16)
        wub[...] = wu_ref[0].astype(BF16)
        wdb[...] = wd_ref[0].astype(BF16)

    @pl.when(used)
    def _():
        lo, hi = _unpack_rows(_load_tile_rows(x_ref, (), blk, x_ref.shape[0] // blk), BF16)
        half = lo.shape[-1]
        g = _dot(lo, wgb[0:half, :]) + _dot(hi, wgb[half:, :])
        u = _dot(lo, wub[0:half, :]) + _dot(hi, wub[half:, :])
        a = (g * jax.nn.sigmoid(g) * u).astype(BF16)
        _store_tile_rows(y_ref, (), _pack_rows(_dot(a, wdb[...])))

    @pl.when(jnp.logical_not(used))
    def _():
        y_ref[...] = jnp.zeros_like(y_ref)


def _experts(xs, block_e, n_used, wg, wu, wd, blk):
    n_blocks = block_e.shape[0]
    _, D, de = wg.shape
    xn = xs.shape[0] // (n_blocks * blk)
    yn = D // 2 // LANES
    grid_spec = pltpu.PrefetchScalarGridSpec(
        num_scalar_prefetch=2,
        grid=(n_blocks,),
        in_specs=[
            pl.BlockSpec((blk * xn, LANES), lambda i, be, nu: (i, 0)),
            pl.BlockSpec((1, D, de), lambda i, be, nu: (be[i], 0, 0)),
            pl.BlockSpec((1, D, de), lambda i, be, nu: (be[i], 0, 0)),
            pl.BlockSpec((1, de, D), lambda i, be, nu: (be[i], 0, 0)),
        ],
        out_specs=pl.BlockSpec((blk * yn, LANES), lambda i, be, nu: (i, 0)),
        scratch_shapes=[pltpu.VMEM((D, de), BF16), pltpu.VMEM((D, de), BF16), pltpu.VMEM((de, D), BF16)],
    )
    return pl.pallas_call(
        functools.partial(_expert_kernel, blk=blk),
        grid_spec=grid_spec,
        out_shape=jax.ShapeDtypeStruct((n_blocks * blk * yn, LANES), U32),
        compiler_params=pltpu.CompilerParams(
            dimension_semantics=("arbitrary",), vmem_limit_bytes=VMEM_LIMIT),
        name="moe_experts",
    )(block_e, n_used, xs, wg, wu, wd)


def _final_kernel(dest_cur, dest_nxt, x1_ref, rec_ref, p_ref, g_ple, w_gate, w_proj, ys_ref, y_ref,
                  buf_a, buf_b, sem, *, tf, n):
    i = pl.program_id(0)
    bufs = (buf_a, buf_b)

    def issue(dref, half, s):
        for r in range(tf):
            for kk in range(TOP_K):
                d = pl.multiple_of(dref[kk, 0, 0, half * tf + r], n)
                pltpu.make_async_copy(ys_ref.at[pl.ds(d, n)], bufs[s].at[kk, pl.ds(r * n, n)],
                                      sem.at[s]).start(priority=kk % 2)

    def drain(s):
        for kk in range(TOP_K):
            pltpu.make_async_copy(ys_ref.at[pl.ds(0, tf * n)], bufs[s].at[kk], sem.at[s]).wait()

    def compute(half, s):
        rows = slice(half * tf, (half + 1) * tf)
        rec = rec_ref[rows, :]
        y1 = jnp.concatenate(_unpack_rows(_load_tile_rows(bufs[s], (0,), tf, n), F32), axis=1)
        y2 = jnp.concatenate(_unpack_rows(_load_tile_rows(bufs[s], (1,), tf, n), F32), axis=1)
        x2 = x1_ref[rows, :] + (rec[:, 2:3] * y1 + rec[:, 3:4] * y2)
        gate = jax.nn.sigmoid(_dot(_rms(x2, g_ple[...]).astype(BF16), w_gate[...]))
        y_ref[rows, :] = x2 + gate * _dot(p_ref[rows, :].astype(BF16), w_proj[...])

    @pl.when(i == 0)
    def _():
        issue(dest_cur, 0, 0)

    drain(0)
    issue(dest_cur, 1, 1)
    compute(0, 0)
    drain(1)
    issue(dest_nxt, 0, 0)
    compute(1, 1)

    @pl.when(i == pl.num_programs(0) - 1)
    def _():
        drain(0)


def _final(x1, rec, p, dest_rows, ys, g_ple, w_gate, w_proj, tf):
    N, D = x1.shape
    n = D // 2 // LANES
    pair = 2 * tf
    assert N % pair == 0
    nt = N // pair
    dest3 = dest_rows.reshape(TOP_K, nt, 1, pair)
    smem_blk = lambda fn: pl.BlockSpec((TOP_K, 1, 1, pair), fn, memory_space=pltpu.SMEM)
    return pl.pallas_call(
        functools.partial(_final_kernel, tf=tf, n=n),
        grid=(nt,),
        in_specs=[
            smem_blk(lambda i: (0, i, 0, 0)),
            smem_blk(lambda i: (0, jnp.minimum(i + 1, nt - 1), 0, 0)),
            pl.BlockSpec((pair, D), lambda i: (i, 0)),
            pl.BlockSpec((pair, REC_W), lambda i: (i, 0)),
            pl.BlockSpec((pair, p.shape[-1]), lambda i: (i, 0)),
            _full(g_ple.shape), _full(w_gate.shape), _full(w_proj.shape),
            pl.BlockSpec(memory_space=pl.ANY),
        ],
        out_specs=pl.BlockSpec((pair, D), lambda i: (i, 0)),
        out_shape=jax.ShapeDtypeStruct((N, D), F32),
        scratch_shapes=[pltpu.VMEM((TOP_K, tf * n, LANES), U32), pltpu.VMEM((TOP_K, tf * n, LANES), U32),
                        pltpu.SemaphoreType.DMA((2,))],
        compiler_params=pltpu.CompilerParams(
            dimension_semantics=("arbitrary",), vmem_limit_bytes=VMEM_LIMIT),
        name="moe_combine_ple",
    )(dest3, dest3, x1, rec, p, g_ple, w_gate, w_proj, ys)


def _slots(rec, pstart, offset):
    e = rec[0:TOP_K, :].astype(I32)
    rank = rec[4:4 + TOP_K, :].astype(I32)
    onehot = e[None, :, :] == jnp.arange(N_EXPERTS, dtype=I32)[:, None, None]
    return rank + jnp.sum(jnp.where(onehot, (pstart + offset)[:, None, None], 0), axis=0)


def _bias_table(sinks):
    slopes = jnp.array([2.0 ** (-8.0 * (i + 1) / N_HEADS) for i in range(N_HEADS)], F32)
    qpos = WINDOW + jnp.arange(CHUNK, dtype=I32)
    col = jnp.arange(BAND_PAD, dtype=I32)
    dist = jnp.abs(qpos[:, None] - col[None, :]).astype(F32)
    core = slopes[:, None, None] * dist[None]
    sink = jnp.broadcast_to(-sinks.astype(F32)[:, None, None], core.shape)
    table = jnp.where(col == BAND, sink, jnp.where(col < BAND, core, MASKED))
    first_valid = jnp.array([WINDOW, WINDOW - CHUNK, 0], I32)
    table = jnp.where(col[None, None, None, :] < first_valid[:, None, None, None], MASKED, table[None])
    return table.reshape(3, N_HEADS * CHUNK, BAND_PAD)


def kernel(x_prompt, x_sample, p_prompt, p_sample, cache_k, cache_v, state_conv, g_mix, w_in, g_q, g_k, sinks,
           w_dw, b_dw, g_cn, b_cn, w_pw, b_pw, g_oa, g_oc, w_out, g_ffn, w_coarse, b_coarse, w_fine, b_fine,
           w_e_gate, w_e_up, w_e_down, g_ple, w_ple_gate, w_ple_proj):
    assert g_mix.shape[0] == 1
    l = 0
    B, T, D = x_prompt.shape
    DB, DS, _ = x_sample.shape
    assert cache_k.shape[2] == WINDOW and DS == CHUNK
    Np, Ns = B * T, DB * DS
    tt, td, tf = _tiles(T, Np, Ns)
    blk = EXPERT_BLOCK

    row = lambda a: a[l].reshape(1, -1)
    perm = jnp.concatenate([jnp.concatenate([jnp.arange(HEAD_DIM) + m * HEAD_DIM,
                                             jnp.arange(HEAD_DIM) + (GQ + m) * HEAD_DIM]) for m in range(GQ)])
    w_in_p = jnp.concatenate([w_in[l][:, perm], w_in[l][:, Q_W:]], axis=1).astype(BF16)
    w_out_p = jnp.concatenate([w_out[l][perm, :], w_out[l][Q_W:, :]], axis=0).astype(BF16)
    g_oa_p = g_oa[l][perm].reshape(1, -1)
    gain_qk = jnp.concatenate([jnp.tile(g_q[l] * (HEAD_DIM ** -0.5), N_HEADS), jnp.tile(g_k[l], N_KV_HEADS)]).reshape(1, -1)
    blk_id = jnp.arange(2 * LANES) // HEAD_DIM
    bd = jnp.where(blk_id[:, None] == blk_id[None, :], 1.0 / HEAD_DIM, 0.0).astype(BF16)
    w_r = jnp.concatenate(
        [w_coarse[l], jnp.transpose(w_fine[l], (1, 0, 2)).reshape(D, N_EXPERTS),
         jnp.zeros((D, LANES - N_GROUPS - N_EXPERTS), F32)], axis=1)
    w_rh = w_r.astype(BF16)
    w_rl = (w_r - w_rh.astype(F32)).astype(BF16)
    w_r2 = jnp.concatenate([w_rh, w_rl], axis=1)
    b_r = jnp.concatenate(
        [b_coarse[l], b_fine[l].reshape(-1), jnp.zeros((LANES - N_GROUPS - N_EXPERTS,), F32)]).reshape(1, LANES)
    bias = _bias_table(sinks[l])
    w_dw_rep = jnp.repeat(w_dw[l], SUBLANES, axis=0)

    def mixer_weights(bias_tbl):
        return (row(g_mix), w_in_p, bd, gain_qk, bias_tbl, w_dw_rep, row(b_dw),
                row(g_cn), row(b_cn), w_pw[l].astype(BF16), row(b_pw), g_oa_p, row(g_oc),
                w_out_p, row(g_ffn), w_r2, w_rh, b_r)

    x1p, hp, recp, rectp, cntp, nkp, nvp, ncp = _prompt_mixer(x_prompt, mixer_weights(bias), tt)
    nb = 4 if DB % 4 == 0 else 1
    x1s, hs, recs, rects, cnts, nks, nvs, ncs = _sample_mixer(
        x_sample, cache_k[l].reshape(DB, WINDOW, KV_W), cache_v[l].reshape(DB, WINDOW, KV_W), state_conv[l],
        mixer_weights(bias[2:3]), nb)

    recp, recs = recp.reshape(Np, REC_W), recs.reshape(Ns, REC_W)
    rectp = jnp.transpose(rectp, (1, 0, 2)).reshape(REC_W, Np)
    rects = jnp.transpose(rects, (1, 0, 2)).reshape(REC_W, Ns)
    cnt_p = cntp[0, :N_EXPERTS].astype(I32)
    cnt = cnt_p + cnts[0, :N_EXPERTS].astype(I32)
    n_blocks = ((Np + Ns) * TOP_K + N_EXPERTS * (blk - 1)) // blk
    padded = (cnt + blk - 1) // blk * blk
    pend = jnp.cumsum(padded)
    pstart = pend - padded
    dest_p = _slots(rectp, pstart, jnp.zeros_like(cnt_p))
    dest_s = _slots(rects, pstart, cnt_p)
    block_e = jnp.minimum(
        jnp.sum((pend[None, :] <= (jnp.arange(n_blocks, dtype=I32) * blk)[:, None]).astype(I32), axis=1),
        N_EXPERTS - 1).astype(I32)
    n_used = (pend[-1:] // blk).astype(I32)
    pad_start = pstart + cnt
    pad_head = pad_start % 2
    meta = jnp.concatenate([pad_start, pad_head, padded - cnt - pad_head, n_used]).astype(I32)

    xn = D // 2 // LANES
    yn = D // 2 // LANES
    xs = _dispatch(hp.reshape(Np * xn, LANES), hs.reshape(Ns * xn, LANES),
                   jnp.concatenate([dest_p, dest_s], axis=1) * xn, meta, n_blocks, blk, td)
    ys = _experts(xs, block_e, n_used, w_e_gate[l], w_e_up[l], w_e_down[l], blk)

    gp = row(g_ple)
    w_gate = w_ple_gate[l].astype(BF16)
    w_proj = w_ple_proj[l].astype(BF16)
    yp = _final(x1p.reshape(Np, D), recp, p_prompt[l].reshape(Np, -1), dest_p * yn, ys, gp, w_gate, w_proj, tf)
    ysm = _final(x1s.reshape(Ns, D), recs, p_sample[l].reshape(Ns, -1), dest_s * yn, ys, gp, w_gate, w_proj, tf)

    kv5 = lambda a, nbat: a.reshape(1, nbat, WINDOW, N_KV_HEADS, HEAD_DIM)
    return (yp.reshape(B, T, D), ysm.reshape(DB, DS, D),
            kv5(nkp, B), kv5(nvp, B), ncp[None],
            kv5(nks, DB), kv5(nvs, DB), ncs[None])
```

```python
import functools

import jax
import jax.numpy as jnp
from jax import lax
from jax.experimental import pallas as pl
from jax.experimental.pallas import tpu as pltpu

CHUNK = 64
HEAD_DIM = 64
N_HEADS = 8
N_KV_HEADS = 2
GQ = N_HEADS // N_KV_HEADS
Q_W = N_HEADS * HEAD_DIM
KV_W = N_KV_HEADS * HEAD_DIM
WINDOW = 128
BAND = WINDOW + CHUNK
BAND_PAD = 256
CONV_WIDTH = 31
CONV_HIST = CONV_WIDTH - 1
CONV_PAD = 32
CONV_BLOCK = 32
SUBLANES = 8
LANES = 128
N_GROUPS = 4
EXPERTS_PER_GROUP = 8
N_EXPERTS = N_GROUPS * EXPERTS_PER_GROUP
TOP_K = 2
EXPERT_BLOCK = 512
EPS = 1e-6
REC_W = 8
MASKED = 1e30
DMA_UNROLL = 8

F32 = jnp.float32
BF16 = jnp.bfloat16
U32 = jnp.uint32
I32 = jnp.int32

VMEM_LIMIT = 56 * 1024 * 1024


def _tiles(n_prompt_seq, n_prompt, n_sample):
    tt = next((t for t in (512, 256) if n_prompt_seq % t == 0), n_prompt_seq)
    td = next(t for t in (2048, 1024, 512, 256, 128, 64) if n_prompt % t == 0 and n_sample % t == 0)
    tf = min(td, 512)
    return tt, td, tf


def _rms(xf, g):
    return xf * lax.rsqrt(jnp.mean(xf * xf, axis=-1, keepdims=True) + EPS) * g


def _dot(a, b):
    return jnp.dot(a, b, preferred_element_type=F32)


def _dot_nt(a, b):
    return lax.dot_general(a, b, (((1,), (1,)), ((), ())), preferred_element_type=F32)


def _store_tile_rows(ref, lead, value):
    rows, width = value.shape
    n = width // LANES
    for c in range(n):
        ref[lead + (pl.ds(c, rows, stride=n), slice(None))] = value[:, c * LANES:(c + 1) * LANES]


def _load_tile_rows(ref, lead, rows, n):
    return jnp.concatenate([ref[lead + (pl.ds(c, rows, stride=n), slice(None))] for c in range(n)], axis=1)


def _project(x, g_mix, w_in_ref, bd_ref, gain_qk):
    xn = _rms(x, g_mix).astype(BF16)
    z = _dot(xn, w_in_ref[...])
    qk = z[:, 0:Q_W + KV_W]
    sq = (qk * qk).astype(BF16)
    bd = bd_ref[...]
    two = 2 * LANES
    ms = jnp.concatenate(
        [_dot(sq[:, 0:two], bd), _dot(sq[:, two:2 * two], bd), _dot(sq[:, 2 * two:], bd[0:KV_W, 0:KV_W])], axis=1)
    qkn = qk * lax.rsqrt(ms + EPS) * gain_qk
    v = z[:, Q_W + KV_W:Q_W + 2 * KV_W]
    c0 = Q_W + 2 * KV_W
    c_conv = (z.shape[1] - c0) // 2
    u = z[:, c0:c0 + c_conv] * jax.nn.sigmoid(z[:, c0 + c_conv:])
    return qkn, v, u


def _attend_chunk(q_groups, k_band, v_band, bias):
    lane = lax.broadcasted_iota(I32, (CHUNK, LANES), 1)
    low = lane < HEAD_DIM
    zero = jnp.zeros((CHUNK, LANES), F32)
    q_all = jnp.concatenate(
        [jnp.where(low, qg, zero) for qg in q_groups] + [jnp.where(low, zero, qg) for qg in q_groups],
        axis=0).astype(BF16)
    pad = jnp.zeros((BAND_PAD - BAND, LANES), BF16)
    s = _dot_nt(q_all, jnp.concatenate([k_band, pad], axis=0)) - bias
    m = jnp.max(s, axis=-1, keepdims=True)
    e = jnp.exp(s - m)
    denom = jnp.sum(e, axis=-1, keepdims=True)
    o = _dot(e.astype(BF16), jnp.concatenate([v_band, pad], axis=0)) * (1.0 / denom)
    half = GQ * CHUNK
    return [jnp.where(low, o[g * CHUNK:(g + 1) * CHUNK, :], o[half + g * CHUNK:half + (g + 1) * CHUNK, :])
            for g in range(GQ)]


def _shift_copies(ush, rows):
    for n in range(1, SUBLANES):
        ush[n, 0:rows, :] = ush[0, n:n + rows, :]


def _conv_rows(ush, r0, rows, w_dw_ref, b_dw):
    ch = b_dw.shape[-1]
    groups = rows // SUBLANES
    acc = jnp.broadcast_to(b_dw, (groups, SUBLANES, ch))
    for j in range(CONV_WIDTH):
        off = CONV_PAD - CONV_HIST + j
        taps = ush[off % SUBLANES, pl.ds(r0 + off - off % SUBLANES, rows), :].reshape(groups, SUBLANES, ch)
        acc = acc + w_dw_ref[j * SUBLANES:(j + 1) * SUBLANES, :][None] * taps
    return acc.reshape(rows, ch)


def _finish(x, oa, conv, refs):
    (g_cn, b_cn, w_pw, b_pw, g_oa, g_oc, w_out, g_ffn, w_r2, w_rh, b_r) = refs
    mu = jnp.mean(conv, axis=-1, keepdims=True)
    cen = conv - mu
    var = jnp.mean(cen * cen, axis=-1, keepdims=True)
    ln = cen * lax.rsqrt(var + EPS) * g_cn[...] + b_cn[...]
    act = (ln * jax.nn.sigmoid(ln)).astype(BF16)
    c = _dot(act, w_pw[...]) + b_pw[...]
    half = oa.shape[-1]
    mixed = (_dot(_rms(oa, g_oa[...]).astype(BF16), w_out[0:half, :])
             + _dot(_rms(c, g_oc[...]).astype(BF16), w_out[half:, :]))
    x1 = x + mixed
    h = _rms(x1, g_ffn[...])
    h_hi = h.astype(BF16)
    h_lo = (h - h_hi.astype(F32)).astype(BF16)
    a = _dot(h_hi, w_r2[...])
    logits = a[:, 0:LANES] + a[:, LANES:] + _dot(h_lo, w_rh[...]) + b_r[...]
    return x1, h, logits


def _pack_rows(h):
    half = h.shape[-1] // 2
    hb = h.astype(BF16).astype(F32)
    lo = lax.bitcast_convert_type(hb[:, 0:half], U32)
    hi = lax.bitcast_convert_type(hb[:, half:], U32)
    return (lo >> 16) | (hi & jnp.uint32(0xFFFF0000))


def _unpack_rows(w, dtype):
    lo = lax.bitcast_convert_type(w << 16, F32).astype(dtype)
    hi = lax.bitcast_convert_type(w & jnp.uint32(0xFFFF0000), F32).astype(dtype)
    return lo, hi


def _route(logits, cnt_ref):
    rows = logits.shape[0]
    lane = lax.broadcasted_iota(I32, logits.shape, 1).astype(F32)
    big = float(LANES)
    ninf = -jnp.inf
    lc = jnp.where(lane < N_GROUPS, logits, ninf)
    mc = jnp.max(lc, axis=-1, keepdims=True)
    grp = jnp.min(jnp.where(lc == mc, lane, big), axis=-1, keepdims=True)
    g1 = 1.0 / jnp.sum(jnp.exp(lc - mc), axis=-1, keepdims=True)
    lo = N_GROUPS + grp * EXPERTS_PER_GROUP
    lf = jnp.where((lane >= lo) & (lane < lo + EXPERTS_PER_GROUP), logits, ninf)
    t1 = jnp.max(lf, axis=-1, keepdims=True)
    i1 = jnp.min(jnp.where(lf == t1, lane, big), axis=-1, keepdims=True)
    lf2 = jnp.where(lane == i1, ninf, lf)
    t2 = jnp.max(lf2, axis=-1, keepdims=True)
    i2 = jnp.min(jnp.where(lf2 == t2, lane, big), axis=-1, keepdims=True)
    e2x = jnp.exp(t2 - t1)
    inv = 1.0 / (1.0 + e2x)
    w1 = g1 * inv
    w2 = g1 * (e2x * inv)
    e1 = i1 - N_GROUPS
    e2 = i2 - N_GROUPS
    oh1 = (lane == e1).astype(F32)
    oh2 = (lane == e2).astype(F32)
    oh = oh1 + oh2
    ri = lax.broadcasted_iota(I32, (rows, rows), 0)
    ci = lax.broadcasted_iota(I32, (rows, rows), 1)
    tri = (ci < ri).astype(BF16)
    tot = _dot(tri, oh.astype(BF16)) + cnt_ref[...]
    r1 = jnp.sum(oh1 * tot, axis=-1, keepdims=True)
    r2 = jnp.sum(oh2 * tot, axis=-1, keepdims=True)
    cnt_ref[...] = cnt_ref[...] + jnp.sum(oh, axis=0, keepdims=True)
    rec = jnp.where(lane == 0, e1, 0.0)
    rec = jnp.where(lane == 1, e2, rec)
    rec = jnp.where(lane == 2, w1, rec)
    rec = jnp.where(lane == 3, w2, rec)
    rec = jnp.where(lane == 4, r1, rec)
    rec = jnp.where(lane == 5, r2, rec)
    return rec[:, 0:REC_W], jnp.transpose(rec)[0:REC_W, :]


def _prompt_mixer_kernel(x_ref, g_mix, w_in, bd, gain_qk, bias_ref, w_dw, b_dw,
                         g_cn, b_cn, w_pw, b_pw, g_oa, g_oc, w_out, g_ffn, w_r2, w_rh, b_r,
                         x1_ref, h_ref, rec_ref, rect_ref, cnt_out, nk_ref, nv_ref, nc_ref,
                         kx, vx, ush, oa_s, conv_s, cnt_s, *, tt):
    b = pl.program_id(0)
    t = pl.program_id(1)
    n_chunks = tt // CHUNK

    @pl.when((b == 0) & (t == 0))
    def _():
        cnt_s[...] = jnp.zeros_like(cnt_s)

    @pl.when(t == 0)
    def _():
        kx[0:WINDOW, :] = jnp.zeros((WINDOW, LANES), BF16)
        vx[0:WINDOW, :] = jnp.zeros((WINDOW, LANES), BF16)
        ush[0, 0:CONV_PAD, :] = jnp.zeros((CONV_PAD, ush.shape[-1]), F32)

    x = x_ref[0]
    qkn, v, u = _project(x, g_mix[...], w_in, bd, gain_qk[...])
    k = qkn[:, Q_W:]
    kx[WINDOW:WINDOW + tt, :] = k.astype(BF16)
    vx[WINDOW:WINDOW + tt, :] = v.astype(BF16)
    ush[0, CONV_PAD:CONV_PAD + tt, :] = u
    _shift_copies(ush, tt + CONV_PAD - SUBLANES)

    for c in range(n_chunks):
        variant = jnp.minimum(t * n_chunks + c, WINDOW // CHUNK)
        rows = slice(c * CHUNK, (c + 1) * CHUNK)
        q_groups = [qkn[rows, m * LANES:(m + 1) * LANES] for m in range(GQ)]
        o_groups = _attend_chunk(q_groups, kx[c * CHUNK:c * CHUNK + BAND, :], vx[c * CHUNK:c * CHUNK + BAND, :],
                                 bias_ref[variant])
        for m in range(GQ):
            oa_s[rows, m * LANES:(m + 1) * LANES] = o_groups[m]

    for rb in range(tt // CONV_BLOCK):
        conv_s[rb * CONV_BLOCK:(rb + 1) * CONV_BLOCK, :] = _conv_rows(ush, rb * CONV_BLOCK, CONV_BLOCK, w_dw, b_dw[...])

    x1, hh, logits = _finish(x, oa_s[...], conv_s[...],
                             (g_cn, b_cn, w_pw, b_pw, g_oa, g_oc, w_out, g_ffn, w_r2, w_rh, b_r))
    x1_ref[0] = x1
    _store_tile_rows(h_ref, (0,), _pack_rows(hh))
    rec_ref[0], rect_ref[0] = _route(logits, cnt_s)
    cnt_out[...] = cnt_s[...]

    nk_ref[0] = k[tt - WINDOW:, :]
    nv_ref[0] = v[tt - WINDOW:, :]
    nc_ref[0] = u[tt - CONV_HIST:, :]

    kx[0:WINDOW, :] = kx[tt:tt + WINDOW, :]
    vx[0:WINDOW, :] = vx[tt:tt + WINDOW, :]
    ush[0, 0:CONV_PAD, :] = ush[0, tt:tt + CONV_PAD, :]


def _sample_mixer_kernel(x_ref, ck_ref, cv_ref, sc_ref, g_mix, w_in, bd, gain_qk, bias_ref, w_dw, b_dw,
                         g_cn, b_cn, w_pw, b_pw, g_oa, g_oc, w_out, g_ffn, w_r2, w_rh, b_r,
                         x1_ref, h_ref, rec_ref, rect_ref, cnt_out, nk_ref, nv_ref, nc_ref,
                         ush, oa_s, conv_s, cnt_s, *, nb):
    i = pl.program_id(0)

    @pl.when(i == 0)
    def _():
        cnt_s[...] = jnp.zeros_like(cnt_s)

    rows_all = nb * CHUNK
    x = x_ref[...].reshape(rows_all, x_ref.shape[-1])
    qkn, v, u = _project(x, g_mix[...], w_in, bd, gain_qk[...])
    k = qkn[:, Q_W:]
    for j in range(nb):
        rows = slice(j * CHUNK, (j + 1) * CHUNK)
        ck = ck_ref[j]
        cv = cv_ref[j]
        k_band = jnp.concatenate([ck, k[rows, :]], axis=0)
        v_band = jnp.concatenate([cv, v[rows, :]], axis=0)
        q_groups = [qkn[rows, m * LANES:(m + 1) * LANES] for m in range(GQ)]
        o_groups = _attend_chunk(q_groups, k_band.astype(BF16), v_band.astype(BF16), bias_ref[0])
        for m in range(GQ):
            oa_s[rows, m * LANES:(m + 1) * LANES] = o_groups[m]
        ush[0, 0:CONV_PAD, :] = jnp.zeros((CONV_PAD, ush.shape[-1]), F32)
        ush[0, CONV_PAD - CONV_HIST:CONV_PAD, :] = sc_ref[j]
        ush[0, CONV_PAD:CONV_PAD + CHUNK, :] = u[rows, :]
        _shift_copies(ush, CHUNK + CONV_PAD - SUBLANES)
        for rb in range(CHUNK // CONV_BLOCK):
            r0 = rb * CONV_BLOCK
            conv_s[j * CHUNK + r0:j * CHUNK + r0 + CONV_BLOCK, :] = _conv_rows(ush, r0, CONV_BLOCK, w_dw, b_dw[...])
        nk_ref[j] = k_band[CHUNK:, :]
        nv_ref[j] = v_band[CHUNK:, :]
        nc_ref[j] = ush[0, CONV_PAD + CHUNK - CONV_HIST:CONV_PAD + CHUNK, :]

    x1, hh, logits = _finish(x, oa_s[...], conv_s[...],
                             (g_cn, b_cn, w_pw, b_pw, g_oa, g_oc, w_out, g_ffn, w_r2, w_rh, b_r))
    x1_ref[...] = x1.reshape(x1_ref.shape)
    packed = _pack_rows(hh)
    for j in range(nb):
        _store_tile_rows(h_ref, (j,), packed[j * CHUNK:(j + 1) * CHUNK, :])
    rec, rec_t = _route(logits, cnt_s)
    rec_ref[...] = rec.reshape(rec_ref.shape)
    rect_ref[0] = rec_t
    cnt_out[...] = cnt_s[...]


def _full(shape):
    nd = len(shape)
    return pl.BlockSpec(shape, lambda *_: (0,) * nd)


def _prompt_mixer(x, wts, tt):
    B, T, D = x.shape
    c_conv = wts[6].shape[-1]
    xs_rows = D // 2 // LANES
    nt = T // tt
    tok = lambda rows, last: pl.BlockSpec((1, rows, last), lambda b, t: (b, t, 0))
    per_batch = lambda rows, last: pl.BlockSpec((1, rows, last), lambda b, t: (b, 0, 0))
    return pl.pallas_call(
        functools.partial(_prompt_mixer_kernel, tt=tt),
        grid=(B, nt),
        in_specs=[tok(tt, D)] + [_full(w.shape) for w in wts],
        out_specs=[tok(tt, D), tok(tt * xs_rows, LANES), tok(tt, REC_W),
                   pl.BlockSpec((1, REC_W, tt), lambda b, t: (b, 0, t)), pl.BlockSpec((1, LANES), lambda b, t: (0, 0)),
                   per_batch(WINDOW, KV_W), per_batch(WINDOW, KV_W), per_batch(CONV_HIST, c_conv)],
        out_shape=[
            jax.ShapeDtypeStruct((B, T, D), F32),
            jax.ShapeDtypeStruct((B, T * xs_rows, LANES), U32),
            jax.ShapeDtypeStruct((B, T, REC_W), F32),
            jax.ShapeDtypeStruct((B, REC_W, T), F32),
            jax.ShapeDtypeStruct((1, LANES), F32),
            jax.ShapeDtypeStruct((B, WINDOW, KV_W), F32),
            jax.ShapeDtypeStruct((B, WINDOW, KV_W), F32),
            jax.ShapeDtypeStruct((B, CONV_HIST, c_conv), F32),
        ],
        scratch_shapes=[
            pltpu.VMEM((WINDOW + tt, LANES), BF16),
            pltpu.VMEM((WINDOW + tt, LANES), BF16),
            pltpu.VMEM((SUBLANES, CONV_PAD + tt, c_conv), F32),
            pltpu.VMEM((tt, Q_W), F32),
            pltpu.VMEM((tt, c_conv), F32),
            pltpu.VMEM((1, LANES), F32),
        ],
        compiler_params=pltpu.CompilerParams(
            dimension_semantics=("arbitrary", "arbitrary"), vmem_limit_bytes=VMEM_LIMIT),
        name="prompt_mixer",
    )(x, *wts)


def _sample_mixer(x, ck, cv, sc, wts, nb):
    B, T, D = x.shape
    assert T == CHUNK and B % nb == 0
    c_conv = wts[6].shape[-1]
    xs_rows = D // 2 // LANES
    blk3 = lambda rows, last: pl.BlockSpec((nb, rows, last), lambda i: (i, 0, 0))
    return pl.pallas_call(
        functools.partial(_sample_mixer_kernel, nb=nb),
        grid=(B // nb,),
        in_specs=[blk3(T, D), blk3(WINDOW, KV_W), blk3(WINDOW, KV_W), blk3(CONV_HIST, c_conv)]
                 + [_full(w.shape) for w in wts],
        out_specs=[blk3(T, D), blk3(T * xs_rows, LANES), blk3(T, REC_W),
                   pl.BlockSpec((1, REC_W, nb * T), lambda i: (i, 0, 0)), pl.BlockSpec((1, LANES), lambda i: (0, 0)),
                   blk3(WINDOW, KV_W), blk3(WINDOW, KV_W), blk3(CONV_HIST, c_conv)],
        out_shape=[
            jax.ShapeDtypeStruct((B, T, D), F32),
            jax.ShapeDtypeStruct((B, T * xs_rows, LANES), U32),
            jax.ShapeDtypeStruct((B, T, REC_W), F32),
            jax.ShapeDtypeStruct((B // nb, REC_W, nb * T), F32),
            jax.ShapeDtypeStruct((1, LANES), F32),
            jax.ShapeDtypeStruct((B, WINDOW, KV_W), F32),
            jax.ShapeDtypeStruct((B, WINDOW, KV_W), F32),
            jax.ShapeDtypeStruct((B, CONV_HIST, c_conv), F32),
        ],
        scratch_shapes=[
            pltpu.VMEM((SUBLANES, CONV_PAD + CHUNK, c_conv), F32),
            pltpu.VMEM((nb * CHUNK, Q_W), F32),
            pltpu.VMEM((nb * CHUNK, c_conv), F32),
            pltpu.VMEM((1, LANES), F32),
        ],
        compiler_params=pltpu.CompilerParams(
            dimension_semantics=("arbitrary",), vmem_limit_bytes=VMEM_LIMIT),
        name="sample_mixer",
    )(x, ck, cv, sc, *wts)


def _dispatch_kernel(meta_ref, dest_ref, hp_ref, hs_ref, xs_ref, zbuf, sem, zsem, *, td, n, blk, n_blocks, ntp):
    i = pl.program_id(0)

    def issue_from(h_ref):
        def body(r8, carry):
            for uu in range(DMA_UNROLL):
                r = r8 * DMA_UNROLL + uu
                src = h_ref.at[pl.ds(pl.multiple_of(r * n, n), n)]
                for kk in range(TOP_K):
                    d = pl.multiple_of(dest_ref[kk, 0, 0, r], n)
                    pltpu.make_async_copy(src, xs_ref.at[pl.ds(d, n)], sem).start(priority=kk % 2)
            return carry
        lax.fori_loop(0, td // DMA_UNROLL, body, 0)

    @pl.when(i < ntp)
    def _():
        issue_from(hp_ref)

    @pl.when(i >= ntp)
    def _():
        issue_from(hs_ref)

    def pad_pass(act):
        def pad_expert(e, carry):
            start = meta_ref[e]
            head = meta_ref[N_EXPERTS + e]
            body = meta_ref[2 * N_EXPERTS + e]

            @pl.when(head == 1)
            def _():
                act(pltpu.make_async_copy(zbuf.at[pl.ds(0, n)], xs_ref.at[pl.ds(pl.multiple_of(start * n, n), n)], zsem))
            bit = blk // 2
            while bit >= 2:
                off = pl.multiple_of((start + head + (body // (2 * bit)) * (2 * bit)) * n, 2 * n)

                @pl.when((body // bit) % 2 == 1)
                def _(bit=bit, off=off):
                    act(pltpu.make_async_copy(zbuf.at[pl.ds(0, bit * n)], xs_ref.at[pl.ds(off, bit * n)], zsem))
                bit //= 2
            return carry

        lax.fori_loop(0, N_EXPERTS, pad_expert, 0)

        def pad_block(j, carry):
            act(pltpu.make_async_copy(zbuf, xs_ref.at[pl.ds(pl.multiple_of(j * (blk * n), blk * n), blk * n)], zsem))
            return carry

        lax.fori_loop(meta_ref[3 * N_EXPERTS], n_blocks, pad_block, 0)

    @pl.when(i == pl.num_programs(0) - 1)
    def _():
        zbuf[...] = jnp.zeros_like(zbuf)
        pad_pass(lambda cp: cp.start())
        pad_pass(lambda cp: cp.wait())

    for _ in range(TOP_K):
        pltpu.make_async_copy(hp_ref, xs_ref.at[pl.ds(0, td * n)], sem).wait()


def _dispatch(hp, hs, dest_rows, meta, n_blocks, blk, td):
    rows_p, rows_s = hp.shape[0], hs.shape[0]
    n_tok = dest_rows.shape[1]
    n = (rows_p + rows_s) // n_tok
    ntp, nts = rows_p // (td * n), rows_s // (td * n)
    dest3 = dest_rows.reshape(TOP_K, ntp + nts, 1, td)
    grid_spec = pltpu.PrefetchScalarGridSpec(
        num_scalar_prefetch=1,
        grid=(ntp + nts,),
        in_specs=[
            pl.BlockSpec((TOP_K, 1, 1, td), lambda i, m: (0, i, 0, 0), memory_space=pltpu.SMEM),
            pl.BlockSpec((td * n, LANES), lambda i, m: (jnp.minimum(i, ntp - 1), 0)),
            pl.BlockSpec((td * n, LANES), lambda i, m: (jnp.maximum(i - ntp, 0), 0)),
        ],
        out_specs=pl.BlockSpec(memory_space=pl.ANY),
        scratch_shapes=[pltpu.VMEM((blk * n, LANES), U32), pltpu.SemaphoreType.DMA(()), pltpu.SemaphoreType.DMA(())],
    )
    return pl.pallas_call(
        functools.partial(_dispatch_kernel, td=td, n=n, blk=blk, n_blocks=n_blocks, ntp=ntp),
        grid_spec=grid_spec,
        out_shape=jax.ShapeDtypeStruct((n_blocks * blk * n, LANES), U32),
        compiler_params=pltpu.CompilerParams(
            dimension_semantics=("arbitrary",), vmem_limit_bytes=VMEM_LIMIT),
        name="moe_dispatch",
    )(meta, dest3, hp, hs)


def _expert_kernel(block_e_ref, n_used_ref, x_ref, wg_ref, wu_ref, wd_ref, y_ref, wgb, wub, wdb, *, blk):
    i = pl.program_id(0)
    used = i < n_used_ref[0]
    new_expert = (i == 0) | (block_e_ref[i] != block_e_ref[jnp.maximum(i - 1, 0)])

    @pl.when(used & new_expert)
    def _():
        wgb[...] = wg_ref[0].astype(BF16)
        wub[...] = wu_ref[0].astype(BF16)
        wdb[...] = wd_ref[0].astype(BF16)

    @pl.when(used)
    def _():
        lo, hi = _unpack_rows(_load_tile_rows(x_ref, (), blk, x_ref.shape[0] // blk), BF16)
        half = lo.shape[-1]
        g = _dot(lo, wgb[0:half, :]) + _dot(hi, wgb[half:, :])
        u = _dot(lo, wub[0:half, :]) + _dot(hi, wub[half:, :])
        a = (g * jax.nn.sigmoid(g) * u).astype(BF16)
        _store_tile_rows(y_ref, (), _pack_rows(_dot(a, wdb[...])))

    @pl.when(jnp.logical_not(used))
    def _():
        y_ref[...] = jnp.zeros_like(y_ref)


def _experts(xs, block_e, n_used, wg, wu, wd, blk):
    n_blocks = block_e.shape[0]
    _, D, de = wg.shape
    xn = xs.shape[0] // (n_blocks * blk)
    yn = D // 2 // LANES
    grid_spec = pltpu.PrefetchScalarGridSpec(
        num_scalar_prefetch=2,
        grid=(n_blocks,),
        in_specs=[
            pl.BlockSpec((blk * xn, LANES), lambda i, be, nu: (i, 0)),
            pl.BlockSpec((1, D, de), lambda i, be, nu: (be[i], 0, 0)),
            pl.BlockSpec((1, D, de), lambda i, be, nu: (be[i], 0, 0)),
            pl.BlockSpec((1, de, D), lambda i, be, nu: (be[i], 0, 0)),
        ],
        out_specs=pl.BlockSpec((blk * yn, LANES), lambda i, be, nu: (i, 0)),
        scratch_shapes=[pltpu.VMEM((D, de), BF16), pltpu.VMEM((D, de), BF16), pltpu.VMEM((de, D), BF16)],
    )
    return pl.pallas_call(
        functools.partial(_expert_kernel, blk=blk),
        grid_spec=grid_spec,
        out_shape=jax.ShapeDtypeStruct((n_blocks * blk * yn, LANES), U32),
        compiler_params=pltpu.CompilerParams(
            dimension_semantics=("arbitrary",), vmem_limit_bytes=VMEM_LIMIT),
        name="moe_experts",
    )(block_e, n_used, xs, wg, wu, wd)


def _final_kernel(dest_cur, dest_nxt, x1_ref, rec_ref, p_ref, g_ple, w_gate, w_proj, ys_ref, y_ref,
                  buf_a, buf_b, sem, *, tf, n):
    i = pl.program_id(0)
    bufs = (buf_a, buf_b)

    def issue(dref, half, s):
        for r in range(tf):
            for kk in range(TOP_K):
                d = pl.multiple_of(dref[kk, 0, 0, half * tf + r], n)
                pltpu.make_async_copy(ys_ref.at[pl.ds(d, n)], bufs[s].at[kk, pl.ds(r * n, n)],
                                      sem.at[s]).start(priority=kk % 2)

    def drain(s):
        for kk in range(TOP_K):
            pltpu.make_async_copy(ys_ref.at[pl.ds(0, tf * n)], bufs[s].at[kk], sem.at[s]).wait()

    def compute(half, s):
        rows = slice(half * tf, (half + 1) * tf)
        rec = rec_ref[rows, :]
        y1 = jnp.concatenate(_unpack_rows(_load_tile_rows(bufs[s], (0,), tf, n), F32), axis=1)
        y2 = jnp.concatenate(_unpack_rows(_load_tile_rows(bufs[s], (1,), tf, n), F32), axis=1)
        x2 = x1_ref[rows, :] + (rec[:, 2:3] * y1 + rec[:, 3:4] * y2)
        gate = jax.nn.sigmoid(_dot(_rms(x2, g_ple[...]).astype(BF16), w_gate[...]))
        y_ref[rows, :] = x2 + gate * _dot(p_ref[rows, :].astype(BF16), w_proj[...])

    @pl.when(i == 0)
    def _():
        issue(dest_cur, 0, 0)

    drain(0)
    issue(dest_cur, 1, 1)
    compute(0, 0)
    drain(1)
    issue(dest_nxt, 0, 0)
    compute(1, 1)

    @pl.when(i == pl.num_programs(0) - 1)
    def _():
        drain(0)


def _final(x1, rec, p, dest_rows, ys, g_ple, w_gate, w_proj, tf):
    N, D = x1.shape
    n = D // 2 // LANES
    pair = 2 * tf
    assert N % pair == 0
    nt = N // pair
    dest3 = dest_rows.reshape(TOP_K, nt, 1, pair)
    smem_blk = lambda fn: pl.BlockSpec((TOP_K, 1, 1, pair), fn, memory_space=pltpu.SMEM)
    return pl.pallas_call(
        functools.partial(_final_kernel, tf=tf, n=n),
        grid=(nt,),
        in_specs=[
            smem_blk(lambda i: (0, i, 0, 0)),
            smem_blk(lambda i: (0, jnp.minimum(i + 1, nt - 1), 0, 0)),
            pl.BlockSpec((pair, D), lambda i: (i, 0)),
            pl.BlockSpec((pair, REC_W), lambda i: (i, 0)),
            pl.BlockSpec((pair, p.shape[-1]), lambda i: (i, 0)),
            _full(g_ple.shape), _full(w_gate.shape), _full(w_proj.shape),
            pl.BlockSpec(memory_space=pl.ANY),
        ],
        out_specs=pl.BlockSpec((pair, D), lambda i: (i, 0)),
        out_shape=jax.ShapeDtypeStruct((N, D), F32),
        scratch_shapes=[pltpu.VMEM((TOP_K, tf * n, LANES), U32), pltpu.VMEM((TOP_K, tf * n, LANES), U32),
                        pltpu.SemaphoreType.DMA((2,))],
        compiler_params=pltpu.CompilerParams(
            dimension_semantics=("arbitrary",), vmem_limit_bytes=VMEM_LIMIT),
        name="moe_combine_ple",
    )(dest3, dest3, x1, rec, p, g_ple, w_gate, w_proj, ys)


def _slots(rec, pstart, offset):
    e = rec[0:TOP_K, :].astype(I32)
    rank = rec[4:4 + TOP_K, :].astype(I32)
    onehot = e[None, :, :] == jnp.arange(N_EXPERTS, dtype=I32)[:, None, None]
    return rank + jnp.sum(jnp.where(onehot, (pstart + offset)[:, None, None], 0), axis=0)


def _bias_table(sinks):
    slopes = jnp.array([2.0 ** (-8.0 * (i + 1) / N_HEADS) for i in range(N_HEADS)], F32)
    qpos = WINDOW + jnp.arange(CHUNK, dtype=I32)
    col = jnp.arange(BAND_PAD, dtype=I32)
    dist = jnp.abs(qpos[:, None] - col[None, :]).astype(F32)
    core = slopes[:, None, None] * dist[None]
    sink = jnp.broadcast_to(-sinks.astype(F32)[:, None, None], core.shape)
    table = jnp.where(col == BAND, sink, jnp.where(col < BAND, core, MASKED))
    first_valid = jnp.array([WINDOW, WINDOW - CHUNK, 0], I32)
    table = jnp.where(col[None, None, None, :] < first_valid[:, None, None, None], MASKED, table[None])
    return table.reshape(3, N_HEADS * CHUNK, BAND_PAD)


def kernel(x_prompt, x_sample, p_prompt, p_sample, cache_k, cache_v, state_conv, g_mix, w_in, g_q, g_k, sinks,
           w_dw, b_dw, g_cn, b_cn, w_pw, b_pw, g_oa, g_oc, w_out, g_ffn, w_coarse, b_coarse, w_fine, b_fine,
           w_e_gate, w_e_up, w_e_down, g_ple, w_ple_gate, w_ple_proj):
    assert g_mix.shape[0] == 1
    l = 0
    B, T, D = x_prompt.shape
    DB, DS, _ = x_sample.shape
    assert cache_k.shape[2] == WINDOW and DS == CHUNK
    Np, Ns = B * T, DB * DS
    tt, td, tf = _tiles(T, Np, Ns)
    blk = EXPERT_BLOCK

    row = lambda a: a[l].reshape(1, -1)
    perm = jnp.concatenate([jnp.concatenate([jnp.arange(HEAD_DIM) + m * HEAD_DIM,
                                             jnp.arange(HEAD_DIM) + (GQ + m) * HEAD_DIM]) for m in range(GQ)])
    w_in_p = jnp.concatenate([w_in[l][:, perm], w_in[l][:, Q_W:]], axis=1).astype(BF16)
    w_out_p = jnp.concatenate([w_out[l][perm, :], w_out[l][Q_W:, :]], axis=0).astype(BF16)
    g_oa_p = g_oa[l][perm].reshape(1, -1)
    gain_qk = jnp.concatenate([jnp.tile(g_q[l] * (HEAD_DIM ** -0.5), N_HEADS), jnp.tile(g_k[l], N_KV_HEADS)]).reshape(1, -1)
    blk_id = jnp.arange(2 * LANES) // HEAD_DIM
    bd = jnp.where(blk_id[:, None] == blk_id[None, :], 1.0 / HEAD_DIM, 0.0).astype(BF16)
    w_r = jnp.concatenate(
        [w_coarse[l], jnp.transpose(w_fine[l], (1, 0, 2)).reshape(D, N_EXPERTS),
         jnp.zeros((D, LANES - N_GROUPS - N_EXPERTS), F32)], axis=1)
    w_rh = w_r.astype(BF16)
    w_rl = (w_r - w_rh.astype(F32)).astype(BF16)
    w_r2 = jnp.concatenate([w_rh, w_rl], axis=1)
    b_r = jnp.concatenate(
        [b_coarse[l], b_fine[l].reshape(-1), jnp.zeros((LANES - N_GROUPS - N_EXPERTS,), F32)]).reshape(1, LANES)
    bias = _bias_table(sinks[l])
    w_dw_rep = jnp.repeat(w_dw[l], SUBLANES, axis=0)

    def mixer_weights(bias_tbl):
        return (row(g_mix), w_in_p, bd, gain_qk, bias_tbl, w_dw_rep, row(b_dw),
                row(g_cn), row(b_cn), w_pw[l].astype(BF16), row(b_pw), g_oa_p, row(g_oc),
                w_out_p, row(g_ffn), w_r2, w_rh, b_r)

    x1p, hp, recp, rectp, cntp, nkp, nvp, ncp = _prompt_mixer(x_prompt, mixer_weights(bias), tt)
    nb = next(n for n in (8, 4, 2, 1) if DB % n == 0)
    x1s, hs, recs, rects, cnts, nks, nvs, ncs = _sample_mixer(
        x_sample, cache_k[l].reshape(DB, WINDOW, KV_W), cache_v[l].reshape(DB, WINDOW, KV_W), state_conv[l],
        mixer_weights(bias[2:3]), nb)

    recp, recs = recp.reshape(Np, REC_W), recs.reshape(Ns, REC_W)
    rectp = jnp.transpose(rectp, (1, 0, 2)).reshape(REC_W, Np)
    rects = jnp.transpose(rects, (1, 0, 2)).reshape(REC_W, Ns)
    cnt_p = cntp[0, :N_EXPERTS].astype(I32)
    cnt = cnt_p + cnts[0, :N_EXPERTS].astype(I32)
    n_blocks = ((Np + Ns) * TOP_K + N_EXPERTS * (blk - 1)) // blk
    padded = (cnt + blk - 1) // blk * blk
    pend = jnp.cumsum(padded)
    pstart = pend - padded
    dest_p = _slots(rectp, pstart, jnp.zeros_like(cnt_p))
    dest_s = _slots(rects, pstart, cnt_p)
    block_e = jnp.minimum(
        jnp.sum((pend[None, :] <= (jnp.arange(n_blocks, dtype=I32) * blk)[:, None]).astype(I32), axis=1),
        N_EXPERTS - 1).astype(I32)
    n_used = (pend[-1:] // blk).astype(I32)
    pad_start = pstart + cnt
    pad_head = pad_start % 2
    meta = jnp.concatenate([pad_start, pad_head, padded - cnt - pad_head, n_used]).astype(I32)

    xn = D // 2 // LANES
    yn = D // 2 // LANES
    xs = _dispatch(hp.reshape(Np * xn, LANES), hs.reshape(Ns * xn, LANES),
                   jnp.concatenate([dest_p, dest_s], axis=1) * xn, meta, n_blocks, blk, td)
    ys = _experts(xs, block_e, n_used, w_e_gate[l], w_e_up[l], w_e_down[l], blk)

    gp = row(g_ple)
    w_gate = w_ple_gate[l].astype(BF16)
    w_proj = w_ple_proj[l].astype(BF16)
    yp = _final(x1p.reshape(Np, D), recp, p_prompt[l].reshape(Np, -1), dest_p * yn, ys, gp, w_gate, w_proj, tf)
    ysm = _final(x1s.reshape(Ns, D), recs, p_sample[l].reshape(Ns, -1), dest_s * yn, ys, gp, w_gate, w_proj, tf)

    kv5 = lambda a, nbat: a.reshape(1, nbat, WINDOW, N_KV_HEADS, HEAD_DIM)
    return (yp.reshape(B, T, D), ysm.reshape(DB, DS, D),
            kv5(nkp, B), kv5(nvp, B), ncp[None],
            kv5(nks, DB), kv5(nvs, DB), ncs[None])
```

```python
import functools

import jax
import jax.numpy as jnp
from jax import lax
from jax.experimental import pallas as pl
from jax.experimental.pallas import tpu as pltpu

CHUNK = 64
HEAD_DIM = 64
N_HEADS = 8
N_KV_HEADS = 2
GQ = N_HEADS // N_KV_HEADS
Q_W = N_HEADS * HEAD_DIM
KV_W = N_KV_HEADS * HEAD_DIM
WINDOW = 128
BAND = WINDOW + CHUNK
BAND_PAD = 256
CONV_WIDTH = 31
CONV_HIST = CONV_WIDTH - 1
CONV_PAD = 32
CONV_BLOCK = 32
SUBLANES = 8
LANES = 128
N_GROUPS = 4
EXPERTS_PER_GROUP = 8
N_EXPERTS = N_GROUPS * EXPERTS_PER_GROUP
TOP_K = 2
EXPERT_BLOCK = 1024
EPS = 1e-6
REC_W = 8
MASKED = 1e30
DMA_UNROLL = 8

F32 = jnp.float32
BF16 = jnp.bfloat16
U32 = jnp.uint32
I32 = jnp.int32

VMEM_LIMIT = 56 * 1024 * 1024


def _tiles(n_prompt_seq, n_prompt, n_sample):
    tt = next((t for t in (512, 256) if n_prompt_seq % t == 0), n_prompt_seq)
    td = next(t for t in (2048, 1024, 512, 256, 128, 64) if n_prompt % t == 0 and n_sample % t == 0)
    tf = min(td, 512)
    return tt, td, tf


def _rms(xf, g):
    return xf * lax.rsqrt(jnp.mean(xf * xf, axis=-1, keepdims=True) + EPS) * g


def _dot(a, b):
    return jnp.dot(a, b, preferred_element_type=F32)


def _dot_nt(a, b):
    return lax.dot_general(a, b, (((1,), (1,)), ((), ())), preferred_element_type=F32)


def _store_tile_rows(ref, lead, value):
    rows, width = value.shape
    n = width // LANES
    for c in range(n):
        ref[lead + (pl.ds(c, rows, stride=n), slice(None))] = value[:, c * LANES:(c + 1) * LANES]


def _load_tile_rows(ref, lead, rows, n):
    return jnp.concatenate([ref[lead + (pl.ds(c, rows, stride=n), slice(None))] for c in range(n)], axis=1)


def _project(x, g_mix, w_in_ref, bd_ref, gain_qk):
    xn = _rms(x, g_mix).astype(BF16)
    z = _dot(xn, w_in_ref[...])
    qk = z[:, 0:Q_W + KV_W]
    sq = (qk * qk).astype(BF16)
    bd = bd_ref[...]
    two = 2 * LANES
    ms = jnp.concatenate(
        [_dot(sq[:, 0:two], bd), _dot(sq[:, two:2 * two], bd), _dot(sq[:, 2 * two:], bd[0:KV_W, 0:KV_W])], axis=1)
    qkn = qk * lax.rsqrt(ms + EPS) * gain_qk
    v = z[:, Q_W + KV_W:Q_W + 2 * KV_W]
    c0 = Q_W + 2 * KV_W
    c_conv = (z.shape[1] - c0) // 2
    u = z[:, c0:c0 + c_conv] * jax.nn.sigmoid(z[:, c0 + c_conv:])
    return qkn, v, u


def _attend_chunk(q_groups, k_band, v_band, bias):
    lane = lax.broadcasted_iota(I32, (CHUNK, LANES), 1)
    low = lane < HEAD_DIM
    zero = jnp.zeros((CHUNK, LANES), F32)
    q_all = jnp.concatenate(
        [jnp.where(low, qg, zero) for qg in q_groups] + [jnp.where(low, zero, qg) for qg in q_groups],
        axis=0).astype(BF16)
    pad = jnp.zeros((BAND_PAD - BAND, LANES), BF16)
    s = _dot_nt(q_all, jnp.concatenate([k_band, pad], axis=0)) - bias
    m = jnp.max(s, axis=-1, keepdims=True)
    e = jnp.exp(s - m)
    denom = jnp.sum(e, axis=-1, keepdims=True)
    o = _dot(e.astype(BF16), jnp.concatenate([v_band, pad], axis=0)) * (1.0 / denom)
    half = GQ * CHUNK
    return [jnp.where(low, o[g * CHUNK:(g + 1) * CHUNK, :], o[half + g * CHUNK:half + (g + 1) * CHUNK, :])
            for g in range(GQ)]


def _shift_copies(ush, rows):
    for n in range(1, SUBLANES):
        ush[n, 0:rows, :] = ush[0, n:n + rows, :]


def _conv_rows(ush, r0, rows, w_dw_ref, b_dw):
    ch = b_dw.shape[-1]
    groups = rows // SUBLANES
    acc = jnp.broadcast_to(b_dw, (groups, SUBLANES, ch))
    for j in range(CONV_WIDTH):
        off = CONV_PAD - CONV_HIST + j
        taps = ush[off % SUBLANES, pl.ds(r0 + off - off % SUBLANES, rows), :].reshape(groups, SUBLANES, ch)
        acc = acc + w_dw_ref[j * SUBLANES:(j + 1) * SUBLANES, :][None] * taps
    return acc.reshape(rows, ch)


def _finish(x, oa, conv, refs):
    (g_cn, b_cn, w_pw, b_pw, g_oa, g_oc, w_out, g_ffn, w_r2, w_rh, b_r) = refs
    mu = jnp.mean(conv, axis=-1, keepdims=True)
    cen = conv - mu
    var = jnp.mean(cen * cen, axis=-1, keepdims=True)
    ln = cen * lax.rsqrt(var + EPS) * g_cn[...] + b_cn[...]
    act = (ln * jax.nn.sigmoid(ln)).astype(BF16)
    c = _dot(act, w_pw[...]) + b_pw[...]
    half = oa.shape[-1]
    mixed = (_dot(_rms(oa, g_oa[...]).astype(BF16), w_out[0:half, :])
             + _dot(_rms(c, g_oc[...]).astype(BF16), w_out[half:, :]))
    x1 = x + mixed
    h = _rms(x1, g_ffn[...])
    h_hi = h.astype(BF16)
    h_lo = (h - h_hi.astype(F32)).astype(BF16)
    a = _dot(h_hi, w_r2[...])
    logits = a[:, 0:LANES] + a[:, LANES:] + _dot(h_lo, w_rh[...]) + b_r[...]
    return x1, h, logits


def _pack_rows(h):
    half = h.shape[-1] // 2
    hb = h.astype(BF16).astype(F32)
    lo = lax.bitcast_convert_type(hb[:, 0:half], U32)
    hi = lax.bitcast_convert_type(hb[:, half:], U32)
    return (lo >> 16) | (hi & jnp.uint32(0xFFFF0000))


def _unpack_rows(w, dtype):
    lo = lax.bitcast_convert_type(w << 16, F32).astype(dtype)
    hi = lax.bitcast_convert_type(w & jnp.uint32(0xFFFF0000), F32).astype(dtype)
    return lo, hi


def _route(logits, cnt_ref):
    rows = logits.shape[0]
    lane = lax.broadcasted_iota(I32, logits.shape, 1).astype(F32)
    big = float(LANES)
    ninf = -jnp.inf
    lc = jnp.where(lane < N_GROUPS, logits, ninf)
    mc = jnp.max(lc, axis=-1, keepdims=True)
    grp = jnp.min(jnp.where(lc == mc, lane, big), axis=-1, keepdims=True)
    g1 = 1.0 / jnp.sum(jnp.exp(lc - mc), axis=-1, keepdims=True)
    lo = N_GROUPS + grp * EXPERTS_PER_GROUP
    lf = jnp.where((lane >= lo) & (lane < lo + EXPERTS_PER_GROUP), logits, ninf)
    t1 = jnp.max(lf, axis=-1, keepdims=True)
    i1 = jnp.min(jnp.where(lf == t1, lane, big), axis=-1, keepdims=True)
    lf2 = jnp.where(lane == i1, ninf, lf)
    t2 = jnp.max(lf2, axis=-1, keepdims=True)
    i2 = jnp.min(jnp.where(lf2 == t2, lane, big), axis=-1, keepdims=True)
    e2x = jnp.exp(t2 - t1)
    inv = 1.0 / (1.0 + e2x)
    w1 = g1 * inv
    w2 = g1 * (e2x * inv)
    e1 = i1 - N_GROUPS
    e2 = i2 - N_GROUPS
    oh1 = (lane == e1).astype(F32)
    oh2 = (lane == e2).astype(F32)
    oh = oh1 + oh2
    ri = lax.broadcasted_iota(I32, (rows, rows), 0)
    ci = lax.broadcasted_iota(I32, (rows, rows), 1)
    tri = (ci < ri).astype(BF16)
    tot = _dot(tri, oh.astype(BF16)) + cnt_ref[...]
    r1 = jnp.sum(oh1 * tot, axis=-1, keepdims=True)
    r2 = jnp.sum(oh2 * tot, axis=-1, keepdims=True)
    cnt_ref[...] = cnt_ref[...] + jnp.sum(oh, axis=0, keepdims=True)
    rec = jnp.where(lane == 0, e1, 0.0)
    rec = jnp.where(lane == 1, e2, rec)
    rec = jnp.where(lane == 2, w1, rec)
    rec = jnp.where(lane == 3, w2, rec)
    rec = jnp.where(lane == 4, r1, rec)
    rec = jnp.where(lane == 5, r2, rec)
    return rec[:, 0:REC_W], jnp.transpose(rec)[0:REC_W, :]


def _prompt_mixer_kernel(x_ref, g_mix, w_in, bd, gain_qk, bias_ref, w_dw, b_dw,
                         g_cn, b_cn, w_pw, b_pw, g_oa, g_oc, w_out, g_ffn, w_r2, w_rh, b_r,
                         x1_ref, h_ref, rec_ref, rect_ref, cnt_out, nk_ref, nv_ref, nc_ref,
                         kx, vx, ush, oa_s, conv_s, cnt_s, *, tt):
    b = pl.program_id(0)
    t = pl.program_id(1)
    n_chunks = tt // CHUNK

    @pl.when((b == 0) & (t == 0))
    def _():
        cnt_s[...] = jnp.zeros_like(cnt_s)

    @pl.when(t == 0)
    def _():
        kx[0:WINDOW, :] = jnp.zeros((WINDOW, LANES), BF16)
        vx[0:WINDOW, :] = jnp.zeros((WINDOW, LANES), BF16)
        ush[0, 0:CONV_PAD, :] = jnp.zeros((CONV_PAD, ush.shape[-1]), F32)

    x = x_ref[0]
    qkn, v, u = _project(x, g_mix[...], w_in, bd, gain_qk[...])
    k = qkn[:, Q_W:]
    kx[WINDOW:WINDOW + tt, :] = k.astype(BF16)
    vx[WINDOW:WINDOW + tt, :] = v.astype(BF16)
    ush[0, CONV_PAD:CONV_PAD + tt, :] = u
    _shift_copies(ush, tt + CONV_PAD - SUBLANES)

    for c in range(n_chunks):
        variant = jnp.minimum(t * n_chunks + c, WINDOW // CHUNK)
        rows = slice(c * CHUNK, (c + 1) * CHUNK)
        q_groups = [qkn[rows, m * LANES:(m + 1) * LANES] for m in range(GQ)]
        o_groups = _attend_chunk(q_groups, kx[c * CHUNK:c * CHUNK + BAND, :], vx[c * CHUNK:c * CHUNK + BAND, :],
                                 bias_ref[variant])
        for m in range(GQ):
            oa_s[rows, m * LANES:(m + 1) * LANES] = o_groups[m]

    for rb in range(tt // CONV_BLOCK):
        conv_s[rb * CONV_BLOCK:(rb + 1) * CONV_BLOCK, :] = _conv_rows(ush, rb * CONV_BLOCK, CONV_BLOCK, w_dw, b_dw[...])

    x1, hh, logits = _finish(x, oa_s[...], conv_s[...],
                             (g_cn, b_cn, w_pw, b_pw, g_oa, g_oc, w_out, g_ffn, w_r2, w_rh, b_r))
    x1_ref[0] = x1
    _store_tile_rows(h_ref, (0,), _pack_rows(hh))
    rec_ref[0], rect_ref[0] = _route(logits, cnt_s)
    cnt_out[...] = cnt_s[...]

    nk_ref[0] = k[tt - WINDOW:, :]
    nv_ref[0] = v[tt - WINDOW:, :]
    nc_ref[0] = u[tt - CONV_HIST:, :]

    kx[0:WINDOW, :] = kx[tt:tt + WINDOW, :]
    vx[0:WINDOW, :] = vx[tt:tt + WINDOW, :]
    ush[0, 0:CONV_PAD, :] = ush[0, tt:tt + CONV_PAD, :]


def _sample_mixer_kernel(x_ref, ck_ref, cv_ref, sc_ref, g_mix, w_in, bd, gain_qk, bias_ref, w_dw, b_dw,
                         g_cn, b_cn, w_pw, b_pw, g_oa, g_oc, w_out, g_ffn, w_r2, w_rh, b_r,
                         x1_ref, h_ref, rec_ref, rect_ref, cnt_out, nk_ref, nv_ref, nc_ref,
                         ush, oa_s, conv_s, cnt_s, *, nb):
    i = pl.program_id(0)

    @pl.when(i == 0)
    def _():
        cnt_s[...] = jnp.zeros_like(cnt_s)

    rows_all = nb * CHUNK
    x = x_ref[...].reshape(rows_all, x_ref.shape[-1])
    qkn, v, u = _project(x, g_mix[...], w_in, bd, gain_qk[...])
    k = qkn[:, Q_W:]
    for j in range(nb):
        rows = slice(j * CHUNK, (j + 1) * CHUNK)
        ck = ck_ref[j]
        cv = cv_ref[j]
        k_band = jnp.concatenate([ck, k[rows, :]], axis=0)
        v_band = jnp.concatenate([cv, v[rows, :]], axis=0)
        q_groups = [qkn[rows, m * LANES:(m + 1) * LANES] for m in range(GQ)]
        o_groups = _attend_chunk(q_groups, k_band.astype(BF16), v_band.astype(BF16), bias_ref[0])
        for m in range(GQ):
            oa_s[rows, m * LANES:(m + 1) * LANES] = o_groups[m]
        ush[0, 0:CONV_PAD, :] = jnp.zeros((CONV_PAD, ush.shape[-1]), F32)
        ush[0, CONV_PAD - CONV_HIST:CONV_PAD, :] = sc_ref[j]
        ush[0, CONV_PAD:CONV_PAD + CHUNK, :] = u[rows, :]
        _shift_copies(ush, CHUNK + CONV_PAD - SUBLANES)
        for rb in range(CHUNK // CONV_BLOCK):
            r0 = rb * CONV_BLOCK
            conv_s[j * CHUNK + r0:j * CHUNK + r0 + CONV_BLOCK, :] = _conv_rows(ush, r0, CONV_BLOCK, w_dw, b_dw[...])
        nk_ref[j] = k_band[CHUNK:, :]
        nv_ref[j] = v_band[CHUNK:, :]
        nc_ref[j] = ush[0, CONV_PAD + CHUNK - CONV_HIST:CONV_PAD + CHUNK, :]

    x1, hh, logits = _finish(x, oa_s[...], conv_s[...],
                             (g_cn, b_cn, w_pw, b_pw, g_oa, g_oc, w_out, g_ffn, w_r2, w_rh, b_r))
    x1_ref[...] = x1.reshape(x1_ref.shape)
    packed = _pack_rows(hh)
    for j in range(nb):
        _store_tile_rows(h_ref, (j,), packed[j * CHUNK:(j + 1) * CHUNK, :])
    rec, rec_t = _route(logits, cnt_s)
    rec_ref[...] = rec.reshape(rec_ref.shape)
    rect_ref[0] = rec_t
    cnt_out[...] = cnt_s[...]


def _full(shape):
    nd = len(shape)
    return pl.BlockSpec(shape, lambda *_: (0,) * nd)


def _prompt_mixer(x, wts, tt):
    B, T, D = x.shape
    c_conv = wts[6].shape[-1]
    xs_rows = D // 2 // LANES
    nt = T // tt
    tok = lambda rows, last: pl.BlockSpec((1, rows, last), lambda b, t: (b, t, 0))
    per_batch = lambda rows, last: pl.BlockSpec((1, rows, last), lambda b, t: (b, 0, 0))
    return pl.pallas_call(
        functools.partial(_prompt_mixer_kernel, tt=tt),
        grid=(B, nt),
        in_specs=[tok(tt, D)] + [_full(w.shape) for w in wts],
        out_specs=[tok(tt, D), tok(tt * xs_rows, LANES), tok(tt, REC_W),
                   pl.BlockSpec((1, REC_W, tt), lambda b, t: (b, 0, t)), pl.BlockSpec((1, LANES), lambda b, t: (0, 0)),
                   per_batch(WINDOW, KV_W), per_batch(WINDOW, KV_W), per_batch(CONV_HIST, c_conv)],
        out_shape=[
            jax.ShapeDtypeStruct((B, T, D), F32),
            jax.ShapeDtypeStruct((B, T * xs_rows, LANES), U32),
            jax.ShapeDtypeStruct((B, T, REC_W), F32),
            jax.ShapeDtypeStruct((B, REC_W, T), F32),
            jax.ShapeDtypeStruct((1, LANES), F32),
            jax.ShapeDtypeStruct((B, WINDOW, KV_W), F32),
            jax.ShapeDtypeStruct((B, WINDOW, KV_W), F32),
            jax.ShapeDtypeStruct((B, CONV_HIST, c_conv), F32),
        ],
        scratch_shapes=[
            pltpu.VMEM((WINDOW + tt, LANES), BF16),
            pltpu.VMEM((WINDOW + tt, LANES), BF16),
            pltpu.VMEM((SUBLANES, CONV_PAD + tt, c_conv), F32),
            pltpu.VMEM((tt, Q_W), F32),
            pltpu.VMEM((tt, c_conv), F32),
            pltpu.VMEM((1, LANES), F32),
        ],
        compiler_params=pltpu.CompilerParams(
            dimension_semantics=("arbitrary", "arbitrary"), vmem_limit_bytes=VMEM_LIMIT),
        name="prompt_mixer",
    )(x, *wts)


def _sample_mixer(x, ck, cv, sc, wts, nb):
    B, T, D = x.shape
    assert T == CHUNK and B % nb == 0
    c_conv = wts[6].shape[-1]
    xs_rows = D // 2 // LANES
    blk3 = lambda rows, last: pl.BlockSpec((nb, rows, last), lambda i: (i, 0, 0))
    return pl.pallas_call(
        functools.partial(_sample_mixer_kernel, nb=nb),
        grid=(B // nb,),
        in_specs=[blk3(T, D), blk3(WINDOW, KV_W), blk3(WINDOW, KV_W), blk3(CONV_HIST, c_conv)]
                 + [_full(w.shape) for w in wts],
        out_specs=[blk3(T, D), blk3(T * xs_rows, LANES), blk3(T, REC_W),
                   pl.BlockSpec((1, REC_W, nb * T), lambda i: (i, 0, 0)), pl.BlockSpec((1, LANES), lambda i: (0, 0)),
                   blk3(WINDOW, KV_W), blk3(WINDOW, KV_W), blk3(CONV_HIST, c_conv)],
        out_shape=[
            jax.ShapeDtypeStruct((B, T, D), F32),
            jax.ShapeDtypeStruct((B, T * xs_rows, LANES), U32),
            jax.ShapeDtypeStruct((B, T, REC_W), F32),
            jax.ShapeDtypeStruct((B // nb, REC_W, nb * T), F32),
            jax.ShapeDtypeStruct((1, LANES), F32),
            jax.ShapeDtypeStruct((B, WINDOW, KV_W), F32),
            jax.ShapeDtypeStruct((B, WINDOW, KV_W), F32),
            jax.ShapeDtypeStruct((B, CONV_HIST, c_conv), F32),
        ],
        scratch_shapes=[
            pltpu.VMEM((SUBLANES, CONV_PAD + CHUNK, c_conv), F32),
            pltpu.VMEM((nb * CHUNK, Q_W), F32),
            pltpu.VMEM((nb * CHUNK, c_conv), F32),
            pltpu.VMEM((1, LANES), F32),
        ],
        compiler_params=pltpu.CompilerParams(
            dimension_semantics=("arbitrary",), vmem_limit_bytes=VMEM_LIMIT),
        name="sample_mixer",
    )(x, ck, cv, sc, *wts)


def _dispatch_kernel(meta_ref, dest_ref, hp_ref, hs_ref, xs_ref, zbuf, sem, zsem, *, td, n, blk, n_blocks, ntp):
    i = pl.program_id(0)

    def issue_from(h_ref):
        def body(r8, carry):
            for uu in range(DMA_UNROLL):
                r = r8 * DMA_UNROLL + uu
                src = h_ref.at[pl.ds(pl.multiple_of(r * n, n), n)]
                for kk in range(TOP_K):
                    d = pl.multiple_of(dest_ref[kk, 0, 0, r], n)
                    pltpu.make_async_copy(src, xs_ref.at[pl.ds(d, n)], sem).start(priority=kk % 2)
            return carry
        lax.fori_loop(0, td // DMA_UNROLL, body, 0)

    @pl.when(i < ntp)
    def _():
        issue_from(hp_ref)

    @pl.when(i >= ntp)
    def _():
        issue_from(hs_ref)

    def pad_pass(act):
        def pad_expert(e, carry):
            start = meta_ref[e]
            head = meta_ref[N_EXPERTS + e]
            body = meta_ref[2 * N_EXPERTS + e]

            @pl.when(head == 1)
            def _():
                act(pltpu.make_async_copy(zbuf.at[pl.ds(0, n)], xs_ref.at[pl.ds(pl.multiple_of(start * n, n), n)], zsem))
            bit = blk // 2
            while bit >= 2:
                off = pl.multiple_of((start + head + (body // (2 * bit)) * (2 * bit)) * n, 2 * n)

                @pl.when((body // bit) % 2 == 1)
                def _(bit=bit, off=off):
                    act(pltpu.make_async_copy(zbuf.at[pl.ds(0, bit * n)], xs_ref.at[pl.ds(off, bit * n)], zsem))
                bit //= 2
            return carry

        lax.fori_loop(0, N_EXPERTS, pad_expert, 0)

        def pad_block(j, carry):
            act(pltpu.make_async_copy(zbuf, xs_ref.at[pl.ds(pl.multiple_of(j * (blk * n), blk * n), blk * n)], zsem))
            return carry

        lax.fori_loop(meta_ref[3 * N_EXPERTS], n_blocks, pad_block, 0)

    @pl.when(i == pl.num_programs(0) - 1)
    def _():
        zbuf[...] = jnp.zeros_like(zbuf)
        pad_pass(lambda cp: cp.start())
        pad_pass(lambda cp: cp.wait())

    for _ in range(TOP_K):
        pltpu.make_async_copy(hp_ref, xs_ref.at[pl.ds(0, td * n)], sem).wait()


def _dispatch(hp, hs, dest_rows, meta, n_blocks, blk, td):
    rows_p, rows_s = hp.shape[0], hs.shape[0]
    n_tok = dest_rows.shape[1]
    n = (rows_p + rows_s) // n_tok
    ntp, nts = rows_p // (td * n), rows_s // (td * n)
    dest3 = dest_rows.reshape(TOP_K, ntp + nts, 1, td)
    grid_spec = pltpu.PrefetchScalarGridSpec(
        num_scalar_prefetch=1,
        grid=(ntp + nts,),
        in_specs=[
            pl.BlockSpec((TOP_K, 1, 1, td), lambda i, m: (0, i, 0, 0), memory_space=pltpu.SMEM),
            pl.BlockSpec((td * n, LANES), lambda i, m: (jnp.minimum(i, ntp - 1), 0)),
            pl.BlockSpec((td * n, LANES), lambda i, m: (jnp.maximum(i - ntp, 0), 0)),
        ],
        out_specs=pl.BlockSpec(memory_space=pl.ANY),
        scratch_shapes=[pltpu.VMEM((blk * n, LANES), U32), pltpu.SemaphoreType.DMA(()), pltpu.SemaphoreType.DMA(())],
    )
    return pl.pallas_call(
        functools.partial(_dispatch_kernel, td=td, n=n, blk=blk, n_blocks=n_blocks, ntp=ntp),
        grid_spec=grid_spec,
        out_shape=jax.ShapeDtypeStruct((n_blocks * blk * n, LANES), U32),
        compiler_params=pltpu.CompilerParams(
            dimension_semantics=("arbitrary",), vmem_limit_bytes=VMEM_LIMIT),
        name="moe_dispatch",
    )(meta, dest3, hp, hs)


def _expert_kernel(block_e_ref, n_used_ref, x_ref, wg_ref, wu_ref, wd_ref, y_ref, wgb, wub, wdb, *, blk):
    i = pl.program_id(0)
    used = i < n_used_ref[0]
    new_expert = (i == 0) | (block_e_ref[i] != block_e_ref[jnp.maximum(i - 1, 0)])

    @pl.when(used & new_expert)
    def _():
        wgb[...] = wg_ref[0].astype(BF16)
        wub[...] = wu_ref[0].astype(BF16)
        wdb[...] = wd_ref[0].astype(BF16)

    @pl.when(used)
    def _():
        lo, hi = _unpack_rows(_load_tile_rows(x_ref, (), blk, x_ref.shape[0] // blk), BF16)
        half = lo.shape[-1]
        g = _dot(lo, wgb[0:half, :]) + _dot(hi, wgb[half:, :])
        u = _dot(lo, wub[0:half, :]) + _dot(hi, wub[half:, :])
        a = (g * jax.nn.sigmoid(g) * u).astype(BF16)
        _store_tile_rows(y_ref, (), _pack_rows(_dot(a, wdb[...])))

    @pl.when(jnp.logical_not(used))
    def _():
        y_ref[...] = jnp.zeros_like(y_ref)


def _experts(xs, block_e, n_used, wg, wu, wd, blk):
    n_blocks = block_e.shape[0]
    _, D, de = wg.shape
    xn = xs.shape[0] // (n_blocks * blk)
    yn = D // 2 // LANES
    grid_spec = pltpu.PrefetchScalarGridSpec(
        num_scalar_prefetch=2,
        grid=(n_blocks,),
        in_specs=[
            pl.BlockSpec((blk * xn, LANES), lambda i, be, nu: (i, 0)),
            pl.BlockSpec((1, D, de), lambda i, be, nu: (be[i], 0, 0)),
            pl.BlockSpec((1, D, de), lambda i, be, nu: (be[i], 0, 0)),
            pl.BlockSpec((1, de, D), lambda i, be, nu: (be[i], 0, 0)),
        ],
        out_specs=pl.BlockSpec((blk * yn, LANES), lambda i, be, nu: (i, 0)),
        scratch_shapes=[pltpu.VMEM((D, de), BF16), pltpu.VMEM((D, de), BF16), pltpu.VMEM((de, D), BF16)],
    )
    return pl.pallas_call(
        functools.partial(_expert_kernel, blk=blk),
        grid_spec=grid_spec,
        out_shape=jax.ShapeDtypeStruct((n_blocks * blk * yn, LANES), U32),
        compiler_params=pltpu.CompilerParams(
            dimension_semantics=("arbitrary",), vmem_limit_bytes=VMEM_LIMIT),
        name="moe_experts",
    )(block_e, n_used, xs, wg, wu, wd)


def _final_kernel(dest_cur, dest_nxt, x1_ref, rec_ref, p_ref, g_ple, w_gate, w_proj, ys_ref, y_ref,
                  buf_a, buf_b, sem, *, tf, n):
    i = pl.program_id(0)
    bufs = (buf_a, buf_b)

    def issue(dref, half, s):
        for r in range(tf):
            for kk in range(TOP_K):
                d = pl.multiple_of(dref[kk, 0, 0, half * tf + r], n)
                pltpu.make_async_copy(ys_ref.at[pl.ds(d, n)], bufs[s].at[kk, pl.ds(r * n, n)],
                                      sem.at[s]).start(priority=kk % 2)

    def drain(s):
        for kk in range(TOP_K):
            pltpu.make_async_copy(ys_ref.at[pl.ds(0, tf * n)], bufs[s].at[kk], sem.at[s]).wait()

    def compute(half, s):
        rows = slice(half * tf, (half + 1) * tf)
        rec = rec_ref[rows, :]
        y1 = jnp.concatenate(_unpack_rows(_load_tile_rows(bufs[s], (0,), tf, n), F32), axis=1)
        y2 = jnp.concatenate(_unpack_rows(_load_tile_rows(bufs[s], (1,), tf, n), F32), axis=1)
        x2 = x1_ref[rows, :] + (rec[:, 2:3] * y1 + rec[:, 3:4] * y2)
        gate = jax.nn.sigmoid(_dot(_rms(x2, g_ple[...]).astype(BF16), w_gate[...]))
        y_ref[rows, :] = x2 + gate * _dot(p_ref[rows, :].astype(BF16), w_proj[...])

    @pl.when(i == 0)
    def _():
        issue(dest_cur, 0, 0)

    drain(0)
    issue(dest_cur, 1, 1)
    compute(0, 0)
    drain(1)
    issue(dest_nxt, 0, 0)
    compute(1, 1)

    @pl.when(i == pl.num_programs(0) - 1)
    def _():
        drain(0)


def _final(x1, rec, p, dest_rows, ys, g_ple, w_gate, w_proj, tf):
    N, D = x1.shape
    n = D // 2 // LANES
    pair = 2 * tf
    assert N % pair == 0
    nt = N // pair
    dest3 = dest_rows.reshape(TOP_K, nt, 1, pair)
    smem_blk = lambda fn: pl.BlockSpec((TOP_K, 1, 1, pair), fn, memory_space=pltpu.SMEM)
    return pl.pallas_call(
        functools.partial(_final_kernel, tf=tf, n=n),
        grid=(nt,),
        in_specs=[
            smem_blk(lambda i: (0, i, 0, 0)),
            smem_blk(lambda i: (0, jnp.minimum(i + 1, nt - 1), 0, 0)),
            pl.BlockSpec((pair, D), lambda i: (i, 0)),
            pl.BlockSpec((pair, REC_W), lambda i: (i, 0)),
            pl.BlockSpec((pair, p.shape[-1]), lambda i: (i, 0)),
            _full(g_ple.shape), _full(w_gate.shape), _full(w_proj.shape),
            pl.BlockSpec(memory_space=pl.ANY),
        ],
        out_specs=pl.BlockSpec((pair, D), lambda i: (i, 0)),
        out_shape=jax.ShapeDtypeStruct((N, D), F32),
        scratch_shapes=[pltpu.VMEM((TOP_K, tf * n, LANES), U32), pltpu.VMEM((TOP_K, tf * n, LANES), U32),
                        pltpu.SemaphoreType.DMA((2,))],
        compiler_params=pltpu.CompilerParams(
            dimension_semantics=("arbitrary",), vmem_limit_bytes=VMEM_LIMIT),
        name="moe_combine_ple",
    )(dest3, dest3, x1, rec, p, g_ple, w_gate, w_proj, ys)


def _slots(rec, pstart, offset):
    e = rec[0:TOP_K, :].astype(I32)
    rank = rec[4:4 + TOP_K, :].astype(I32)
    onehot = e[None, :, :] == jnp.arange(N_EXPERTS, dtype=I32)[:, None, None]
    return rank + jnp.sum(jnp.where(onehot, (pstart + offset)[:, None, None], 0), axis=0)


def _bias_table(sinks):
    slopes = jnp.array([2.0 ** (-8.0 * (i + 1) / N_HEADS) for i in range(N_HEADS)], F32)
    qpos = WINDOW + jnp.arange(CHUNK, dtype=I32)
    col = jnp.arange(BAND_PAD, dtype=I32)
    dist = jnp.abs(qpos[:, None] - col[None, :]).astype(F32)
    core = slopes[:, None, None] * dist[None]
    sink = jnp.broadcast_to(-sinks.astype(F32)[:, None, None], core.shape)
    table = jnp.where(col == BAND, sink, jnp.where(col < BAND, core, MASKED))
    first_valid = jnp.array([WINDOW, WINDOW - CHUNK, 0], I32)
    table = jnp.where(col[None, None, None, :] < first_valid[:, None, None, None], MASKED, table[None])
    return table.reshape(3, N_HEADS * CHUNK, BAND_PAD)


def kernel(x_prompt, x_sample, p_prompt, p_sample, cache_k, cache_v, state_conv, g_mix, w_in, g_q, g_k, sinks,
           w_dw, b_dw, g_cn, b_cn, w_pw, b_pw, g_oa, g_oc, w_out, g_ffn, w_coarse, b_coarse, w_fine, b_fine,
           w_e_gate, w_e_up, w_e_down, g_ple, w_ple_gate, w_ple_proj):
    assert g_mix.shape[0] == 1
    l = 0
    B, T, D = x_prompt.shape
    DB, DS, _ = x_sample.shape
    assert cache_k.shape[2] == WINDOW and DS == CHUNK
    Np, Ns = B * T, DB * DS
    tt, td, tf = _tiles(T, Np, Ns)
    blk = EXPERT_BLOCK

    row = lambda a: a[l].reshape(1, -1)
    perm = jnp.concatenate([jnp.concatenate([jnp.arange(HEAD_DIM) + m * HEAD_DIM,
                                             jnp.arange(HEAD_DIM) + (GQ + m) * HEAD_DIM]) for m in range(GQ)])
    w_in_p = jnp.concatenate([w_in[l][:, perm], w_in[l][:, Q_W:]], axis=1).astype(BF16)
    w_out_p = jnp.concatenate([w_out[l][perm, :], w_out[l][Q_W:, :]], axis=0).astype(BF16)
    g_oa_p = g_oa[l][perm].reshape(1, -1)
    gain_qk = jnp.concatenate([jnp.tile(g_q[l] * (HEAD_DIM ** -0.5), N_HEADS), jnp.tile(g_k[l], N_KV_HEADS)]).reshape(1, -1)
    blk_id = jnp.arange(2 * LANES) // HEAD_DIM
    bd = jnp.where(blk_id[:, None] == blk_id[None, :], 1.0 / HEAD_DIM, 0.0).astype(BF16)
    w_r = jnp.concatenate(
        [w_coarse[l], jnp.transpose(w_fine[l], (1, 0, 2)).reshape(D, N_EXPERTS),
         jnp.zeros((D, LANES - N_GROUPS - N_EXPERTS), F32)], axis=1)
    w_rh = w_r.astype(BF16)
    w_rl = (w_r - w_rh.astype(F32)).astype(BF16)
    w_r2 = jnp.concatenate([w_rh, w_rl], axis=1)
    b_r = jnp.concatenate(
        [b_coarse[l], b_fine[l].reshape(-1), jnp.zeros((LANES - N_GROUPS - N_EXPERTS,), F32)]).reshape(1, LANES)
    bias = _bias_table(sinks[l])
    w_dw_rep = jnp.repeat(w_dw[l], SUBLANES, axis=0)

    def mixer_weights(bias_tbl):
        return (row(g_mix), w_in_p, bd, gain_qk, bias_tbl, w_dw_rep, row(b_dw),
                row(g_cn), row(b_cn), w_pw[l].astype(BF16), row(b_pw), g_oa_p, row(g_oc),
                w_out_p, row(g_ffn), w_r2, w_rh, b_r)

    x1p, hp, recp, rectp, cntp, nkp, nvp, ncp = _prompt_mixer(x_prompt, mixer_weights(bias), tt)
    nb = next(n for n in (8, 4, 2, 1) if DB % n == 0)
    x1s, hs, recs, rects, cnts, nks, nvs, ncs = _sample_mixer(
        x_sample, cache_k[l].reshape(DB, WINDOW, KV_W), cache_v[l].reshape(DB, WINDOW, KV_W), state_conv[l],
        mixer_weights(bias[2:3]), nb)

    recp, recs = recp.reshape(Np, REC_W), recs.reshape(Ns, REC_W)
    rectp = jnp.transpose(rectp, (1, 0, 2)).reshape(REC_W, Np)
    rects = jnp.transpose(rects, (1, 0, 2)).reshape(REC_W, Ns)
    cnt_p = cntp[0, :N_EXPERTS].astype(I32)
    cnt = cnt_p + cnts[0, :N_EXPERTS].astype(I32)
    n_blocks = ((Np + Ns) * TOP_K + N_EXPERTS * (blk - 1)) // blk
    padded = (cnt + blk - 1) // blk * blk
    pend = jnp.cumsum(padded)
    pstart = pend - padded
    dest_p = _slots(rectp, pstart, jnp.zeros_like(cnt_p))
    dest_s = _slots(rects, pstart, cnt_p)
    block_e = jnp.minimum(
        jnp.sum((pend[None, :] <= (jnp.arange(n_blocks, dtype=I32) * blk)[:, None]).astype(I32), axis=1),
        N_EXPERTS - 1).astype(I32)
    n_used = (pend[-1:] // blk).astype(I32)
    pad_start = pstart + cnt
    pad_head = pad_start % 2
    meta = jnp.concatenate([pad_start, pad_head, padded - cnt - pad_head, n_used]).astype(I32)

    xn = D // 2 // LANES
    yn = D // 2 // LANES
    xs = _dispatch(hp.reshape(Np * xn, LANES), hs.reshape(Ns * xn, LANES),
                   jnp.concatenate([dest_p, dest_s], axis=1) * xn, meta, n_blocks, blk, td)
    ys = _experts(xs, block_e, n_used, w_e_gate[l], w_e_up[l], w_e_down[l], blk)

    gp = row(g_ple)
    w_gate = w_ple_gate[l].astype(BF16)
    w_proj = w_ple_proj[l].astype(BF16)
    yp = _final(x1p.reshape(Np, D), recp, p_prompt[l].reshape(Np, -1), dest_p * yn, ys, gp, w_gate, w_proj, tf)
    ysm = _final(x1s.reshape(Ns, D), recs, p_sample[l].reshape(Ns, -1), dest_s * yn, ys, gp, w_gate, w_proj, tf)

    kv5 = lambda a, nbat: a.reshape(1, nbat, WINDOW, N_KV_HEADS, HEAD_DIM)
    return (yp.reshape(B, T, D), ysm.reshape(DB, DS, D),
            kv5(nkp, B), kv5(nvp, B), ncp[None],
            kv5(nks, DB), kv5(nvs, DB), ncs[None])
```

```python
import functools

import jax
import jax.numpy as jnp
from jax import lax
from jax.experimental import pallas as pl
from jax.experimental.pallas import tpu as pltpu

CHUNK = 64
HEAD_DIM = 64
N_HEADS = 8
N_KV_HEADS = 2
GQ = N_HEADS // N_KV_HEADS
Q_W = N_HEADS * HEAD_DIM
KV_W = N_KV_HEADS * HEAD_DIM
WINDOW = 128
BAND = WINDOW + CHUNK
BAND_PAD = 256
CONV_WIDTH = 31
CONV_HIST = CONV_WIDTH - 1
CONV_PAD = 32
CONV_BLOCK = 32
SUBLANES = 8
LANES = 128
N_GROUPS = 4
EXPERTS_PER_GROUP = 8
N_EXPERTS = N_GROUPS * EXPERTS_PER_GROUP
TOP_K = 2
EXPERT_BLOCK = 1024
EXPERT_PARTS = 4
EPS = 1e-6
REC_W = 8
MASKED = 1e30
DMA_UNROLL = 8

F32 = jnp.float32
BF16 = jnp.bfloat16
U32 = jnp.uint32
I32 = jnp.int32

VMEM_LIMIT = 56 * 1024 * 1024


def _tiles(n_prompt_seq, n_prompt, n_sample):
    tt = next((t for t in (512, 256) if n_prompt_seq % t == 0), n_prompt_seq)
    td = next(t for t in (2048, 1024, 512, 256, 128, 64) if n_prompt % t == 0 and n_sample % t == 0)
    tf = min(td, 512)
    return tt, td, tf


def _rms(xf, g):
    return xf * lax.rsqrt(jnp.mean(xf * xf, axis=-1, keepdims=True) + EPS) * g


def _dot(a, b):
    return jnp.dot(a, b, preferred_element_type=F32)


def _dot_nt(a, b):
    return lax.dot_general(a, b, (((1,), (1,)), ((), ())), preferred_element_type=F32)


def _store_tile_rows(ref, lead, value):
    rows, width = value.shape
    n = width // LANES
    for c in range(n):
        ref[lead + (pl.ds(c, rows, stride=n), slice(None))] = value[:, c * LANES:(c + 1) * LANES]


def _load_tile_rows(ref, lead, rows, n):
    return jnp.concatenate([ref[lead + (pl.ds(c, rows, stride=n), slice(None))] for c in range(n)], axis=1)


def _project(x, g_mix, w_in_ref, bd_ref, gain_qk):
    xn = _rms(x, g_mix).astype(BF16)
    z = _dot(xn, w_in_ref[...])
    qk = z[:, 0:Q_W + KV_W]
    sq = (qk * qk).astype(BF16)
    bd = bd_ref[...]
    two = 2 * LANES
    ms = jnp.concatenate(
        [_dot(sq[:, 0:two], bd), _dot(sq[:, two:2 * two], bd), _dot(sq[:, 2 * two:], bd[0:KV_W, 0:KV_W])], axis=1)
    qkn = qk * lax.rsqrt(ms + EPS) * gain_qk
    v = z[:, Q_W + KV_W:Q_W + 2 * KV_W]
    c0 = Q_W + 2 * KV_W
    c_conv = (z.shape[1] - c0) // 2
    u = z[:, c0:c0 + c_conv] * jax.nn.sigmoid(z[:, c0 + c_conv:])
    return qkn, v, u


def _attend_chunk(q_groups, k_band, v_band, bias):
    lane = lax.broadcasted_iota(I32, (CHUNK, LANES), 1)
    low = lane < HEAD_DIM
    zero = jnp.zeros((CHUNK, LANES), F32)
    q_all = jnp.concatenate(
        [jnp.where(low, qg, zero) for qg in q_groups] + [jnp.where(low, zero, qg) for qg in q_groups],
        axis=0).astype(BF16)
    pad = jnp.zeros((BAND_PAD - BAND, LANES), BF16)
    s = _dot_nt(q_all, jnp.concatenate([k_band, pad], axis=0)) - bias
    m = jnp.max(s, axis=-1, keepdims=True)
    e = jnp.exp(s - m)
    denom = jnp.sum(e, axis=-1, keepdims=True)
    o = _dot(e.astype(BF16), jnp.concatenate([v_band, pad], axis=0)) * (1.0 / denom)
    half = GQ * CHUNK
    return [jnp.where(low, o[g * CHUNK:(g + 1) * CHUNK, :], o[half + g * CHUNK:half + (g + 1) * CHUNK, :])
            for g in range(GQ)]


def _shift_copies(ush, rows):
    for n in range(1, SUBLANES):
        ush[n, 0:rows, :] = ush[0, n:n + rows, :]


def _conv_rows(ush, r0, rows, w_dw_ref, b_dw):
    ch = b_dw.shape[-1]
    groups = rows // SUBLANES
    acc = jnp.broadcast_to(b_dw, (groups, SUBLANES, ch))
    for j in range(CONV_WIDTH):
        off = CONV_PAD - CONV_HIST + j
        taps = ush[off % SUBLANES, pl.ds(r0 + off - off % SUBLANES, rows), :].reshape(groups, SUBLANES, ch)
        acc = acc + w_dw_ref[j * SUBLANES:(j + 1) * SUBLANES, :][None] * taps
    return acc.reshape(rows, ch)


def _finish(x, oa, conv, refs):
    (g_cn, b_cn, w_pw, b_pw, g_oa, g_oc, w_out, g_ffn, w_r2, w_rh, b_r) = refs
    mu = jnp.mean(conv, axis=-1, keepdims=True)
    cen = conv - mu
    var = jnp.mean(cen * cen, axis=-1, keepdims=True)
    ln = cen * lax.rsqrt(var + EPS) * g_cn[...] + b_cn[...]
    act = (ln * jax.nn.sigmoid(ln)).astype(BF16)
    c = _dot(act, w_pw[...]) + b_pw[...]
    half = oa.shape[-1]
    mixed = (_dot(_rms(oa, g_oa[...]).astype(BF16), w_out[0:half, :])
             + _dot(_rms(c, g_oc[...]).astype(BF16), w_out[half:, :]))
    x1 = x + mixed
    h = _rms(x1, g_ffn[...])
    h_hi = h.astype(BF16)
    h_lo = (h - h_hi.astype(F32)).astype(BF16)
    a = _dot(h_hi, w_r2[...])
    logits = a[:, 0:LANES] + a[:, LANES:] + _dot(h_lo, w_rh[...]) + b_r[...]
    return x1, h, logits


def _pack_rows(h):
    half = h.shape[-1] // 2
    hb = h.astype(BF16).astype(F32)
    lo = lax.bitcast_convert_type(hb[:, 0:half], U32)
    hi = lax.bitcast_convert_type(hb[:, half:], U32)
    return (lo >> 16) | (hi & jnp.uint32(0xFFFF0000))


def _unpack_rows(w, dtype):
    lo = lax.bitcast_convert_type(w << 16, F32).astype(dtype)
    hi = lax.bitcast_convert_type(w & jnp.uint32(0xFFFF0000), F32).astype(dtype)
    return lo, hi


def _route(logits, cnt_ref):
    rows = logits.shape[0]
    lane = lax.broadcasted_iota(I32, logits.shape, 1).astype(F32)
    big = float(LANES)
    ninf = -jnp.inf
    lc = jnp.where(lane < N_GROUPS, logits, ninf)
    mc = jnp.max(lc, axis=-1, keepdims=True)
    grp = jnp.min(jnp.where(lc == mc, lane, big), axis=-1, keepdims=True)
    g1 = 1.0 / jnp.sum(jnp.exp(lc - mc), axis=-1, keepdims=True)
    lo = N_GROUPS + grp * EXPERTS_PER_GROUP
    lf = jnp.where((lane >= lo) & (lane < lo + EXPERTS_PER_GROUP), logits, ninf)
    t1 = jnp.max(lf, axis=-1, keepdims=True)
    i1 = jnp.min(jnp.where(lf == t1, lane, big), axis=-1, keepdims=True)
    lf2 = jnp.where(lane == i1, ninf, lf)
    t2 = jnp.max(lf2, axis=-1, keepdims=True)
    i2 = jnp.min(jnp.where(lf2 == t2, lane, big), axis=-1, keepdims=True)
    e2x = jnp.exp(t2 - t1)
    inv = 1.0 / (1.0 + e2x)
    w1 = g1 * inv
    w2 = g1 * (e2x * inv)
    e1 = i1 - N_GROUPS
    e2 = i2 - N_GROUPS
    oh1 = (lane == e1).astype(F32)
    oh2 = (lane == e2).astype(F32)
    oh = oh1 + oh2
    ri = lax.broadcasted_iota(I32, (rows, rows), 0)
    ci = lax.broadcasted_iota(I32, (rows, rows), 1)
    tri = (ci < ri).astype(BF16)
    tot = _dot(tri, oh.astype(BF16)) + cnt_ref[...]
    r1 = jnp.sum(oh1 * tot, axis=-1, keepdims=True)
    r2 = jnp.sum(oh2 * tot, axis=-1, keepdims=True)
    cnt_ref[...] = cnt_ref[...] + jnp.sum(oh, axis=0, keepdims=True)
    rec = jnp.where(lane == 0, e1, 0.0)
    rec = jnp.where(lane == 1, e2, rec)
    rec = jnp.where(lane == 2, w1, rec)
    rec = jnp.where(lane == 3, w2, rec)
    rec = jnp.where(lane == 4, r1, rec)
    rec = jnp.where(lane == 5, r2, rec)
    return rec[:, 0:REC_W], jnp.transpose(rec)[0:REC_W, :]


def _prompt_mixer_kernel(x_ref, g_mix, w_in, bd, gain_qk, bias_ref, w_dw, b_dw,
                         g_cn, b_cn, w_pw, b_pw, g_oa, g_oc, w_out, g_ffn, w_r2, w_rh, b_r,
                         x1_ref, h_ref, rec_ref, rect_ref, cnt_out, nk_ref, nv_ref, nc_ref,
                         kx, vx, ush, oa_s, conv_s, cnt_s, *, tt):
    b = pl.program_id(0)
    t = pl.program_id(1)
    n_chunks = tt // CHUNK

    @pl.when((b == 0) & (t == 0))
    def _():
        cnt_s[...] = jnp.zeros_like(cnt_s)

    @pl.when(t == 0)
    def _():
        kx[0:WINDOW, :] = jnp.zeros((WINDOW, LANES), BF16)
        vx[0:WINDOW, :] = jnp.zeros((WINDOW, LANES), BF16)
        ush[0, 0:CONV_PAD, :] = jnp.zeros((CONV_PAD, ush.shape[-1]), F32)

    x = x_ref[0]
    qkn, v, u = _project(x, g_mix[...], w_in, bd, gain_qk[...])
    k = qkn[:, Q_W:]
    kx[WINDOW:WINDOW + tt, :] = k.astype(BF16)
    vx[WINDOW:WINDOW + tt, :] = v.astype(BF16)
    ush[0, CONV_PAD:CONV_PAD + tt, :] = u
    _shift_copies(ush, tt + CONV_PAD - SUBLANES)

    for c in range(n_chunks):
        variant = jnp.minimum(t * n_chunks + c, WINDOW // CHUNK)
        rows = slice(c * CHUNK, (c + 1) * CHUNK)
        q_groups = [qkn[rows, m * LANES:(m + 1) * LANES] for m in range(GQ)]
        o_groups = _attend_chunk(q_groups, kx[c * CHUNK:c * CHUNK + BAND, :], vx[c * CHUNK:c * CHUNK + BAND, :],
                                 bias_ref[variant])
        for m in range(GQ):
            oa_s[rows, m * LANES:(m + 1) * LANES] = o_groups[m]

    for rb in range(tt // CONV_BLOCK):
        conv_s[rb * CONV_BLOCK:(rb + 1) * CONV_BLOCK, :] = _conv_rows(ush, rb * CONV_BLOCK, CONV_BLOCK, w_dw, b_dw[...])

    x1, hh, logits = _finish(x, oa_s[...], conv_s[...],
                             (g_cn, b_cn, w_pw, b_pw, g_oa, g_oc, w_out, g_ffn, w_r2, w_rh, b_r))
    x1_ref[0] = x1
    _store_tile_rows(h_ref, (0,), _pack_rows(hh))
    rec_ref[0], rect_ref[0] = _route(logits, cnt_s)
    cnt_out[...] = cnt_s[...]

    nk_ref[0] = k[tt - WINDOW:, :]
    nv_ref[0] = v[tt - WINDOW:, :]
    nc_ref[0] = u[tt - CONV_HIST:, :]

    kx[0:WINDOW, :] = kx[tt:tt + WINDOW, :]
    vx[0:WINDOW, :] = vx[tt:tt + WINDOW, :]
    ush[0, 0:CONV_PAD, :] = ush[0, tt:tt + CONV_PAD, :]


def _sample_mixer_kernel(x_ref, ck_ref, cv_ref, sc_ref, g_mix, w_in, bd, gain_qk, bias_ref, w_dw, b_dw,
                         g_cn, b_cn, w_pw, b_pw, g_oa, g_oc, w_out, g_ffn, w_r2, w_rh, b_r,
                         x1_ref, h_ref, rec_ref, rect_ref, cnt_out, nk_ref, nv_ref, nc_ref,
                         ush, oa_s, conv_s, cnt_s, *, nb):
    i = pl.program_id(0)

    @pl.when(i == 0)
    def _():
        cnt_s[...] = jnp.zeros_like(cnt_s)

    rows_all = nb * CHUNK
    x = x_ref[...].reshape(rows_all, x_ref.shape[-1])
    qkn, v, u = _project(x, g_mix[...], w_in, bd, gain_qk[...])
    k = qkn[:, Q_W:]
    for j in range(nb):
        rows = slice(j * CHUNK, (j + 1) * CHUNK)
        ck = ck_ref[j]
        cv = cv_ref[j]
        k_band = jnp.concatenate([ck, k[rows, :]], axis=0)
        v_band = jnp.concatenate([cv, v[rows, :]], axis=0)
        q_groups = [qkn[rows, m * LANES:(m + 1) * LANES] for m in range(GQ)]
        o_groups = _attend_chunk(q_groups, k_band.astype(BF16), v_band.astype(BF16), bias_ref[0])
        for m in range(GQ):
            oa_s[rows, m * LANES:(m + 1) * LANES] = o_groups[m]
        ush[0, 0:CONV_PAD, :] = jnp.zeros((CONV_PAD, ush.shape[-1]), F32)
        ush[0, CONV_PAD - CONV_HIST:CONV_PAD, :] = sc_ref[j]
        ush[0, CONV_PAD:CONV_PAD + CHUNK, :] = u[rows, :]
        _shift_copies(ush, CHUNK + CONV_PAD - SUBLANES)
        for rb in range(CHUNK // CONV_BLOCK):
            r0 = rb * CONV_BLOCK
            conv_s[j * CHUNK + r0:j * CHUNK + r0 + CONV_BLOCK, :] = _conv_rows(ush, r0, CONV_BLOCK, w_dw, b_dw[...])
        nk_ref[j] = k_band[CHUNK:, :]
        nv_ref[j] = v_band[CHUNK:, :]
        nc_ref[j] = ush[0, CONV_PAD + CHUNK - CONV_HIST:CONV_PAD + CHUNK, :]

    x1, hh, logits = _finish(x, oa_s[...], conv_s[...],
                             (g_cn, b_cn, w_pw, b_pw, g_oa, g_oc, w_out, g_ffn, w_r2, w_rh, b_r))
    x1_ref[...] = x1.reshape(x1_ref.shape)
    packed = _pack_rows(hh)
    for j in range(nb):
        _store_tile_rows(h_ref, (j,), packed[j * CHUNK:(j + 1) * CHUNK, :])
    rec, rec_t = _route(logits, cnt_s)
    rec_ref[...] = rec.reshape(rec_ref.shape)
    rect_ref[0] = rec_t
    cnt_out[...] = cnt_s[...]


def _full(shape):
    nd = len(shape)
    return pl.BlockSpec(shape, lambda *_: (0,) * nd)


def _prompt_mixer(x, wts, tt):
    B, T, D = x.shape
    c_conv = wts[6].shape[-1]
    xs_rows = D // 2 // LANES
    nt = T // tt
    tok = lambda rows, last: pl.BlockSpec((1, rows, last), lambda b, t: (b, t, 0))
    per_batch = lambda rows, last: pl.BlockSpec((1, rows, last), lambda b, t: (b, 0, 0))
    return pl.pallas_call(
        functools.partial(_prompt_mixer_kernel, tt=tt),
        grid=(B, nt),
        in_specs=[tok(tt, D)] + [_full(w.shape) for w in wts],
        out_specs=[tok(tt, D), tok(tt * xs_rows, LANES), tok(tt, REC_W),
                   pl.BlockSpec((1, REC_W, tt), lambda b, t: (b, 0, t)), pl.BlockSpec((1, LANES), lambda b, t: (0, 0)),
                   per_batch(WINDOW, KV_W), per_batch(WINDOW, KV_W), per_batch(CONV_HIST, c_conv)],
        out_shape=[
            jax.ShapeDtypeStruct((B, T, D), F32),
            jax.ShapeDtypeStruct((B, T * xs_rows, LANES), U32),
            jax.ShapeDtypeStruct((B, T, REC_W), F32),
            jax.ShapeDtypeStruct((B, REC_W, T), F32),
            jax.ShapeDtypeStruct((1, LANES), F32),
            jax.ShapeDtypeStruct((B, WINDOW, KV_W), F32),
            jax.ShapeDtypeStruct((B, WINDOW, KV_W), F32),
            jax.ShapeDtypeStruct((B, CONV_HIST, c_conv), F32),
        ],
        scratch_shapes=[
            pltpu.VMEM((WINDOW + tt, LANES), BF16),
            pltpu.VMEM((WINDOW + tt, LANES), BF16),
            pltpu.VMEM((SUBLANES, CONV_PAD + tt, c_conv), F32),
            pltpu.VMEM((tt, Q_W), F32),
            pltpu.VMEM((tt, c_conv), F32),
            pltpu.VMEM((1, LANES), F32),
        ],
        compiler_params=pltpu.CompilerParams(
            dimension_semantics=("arbitrary", "arbitrary"), vmem_limit_bytes=VMEM_LIMIT),
        name="prompt_mixer",
    )(x, *wts)


def _sample_mixer(x, ck, cv, sc, wts, nb):
    B, T, D = x.shape
    assert T == CHUNK and B % nb == 0
    c_conv = wts[6].shape[-1]
    xs_rows = D // 2 // LANES
    blk3 = lambda rows, last: pl.BlockSpec((nb, rows, last), lambda i: (i, 0, 0))
    return pl.pallas_call(
        functools.partial(_sample_mixer_kernel, nb=nb),
        grid=(B // nb,),
        in_specs=[blk3(T, D), blk3(WINDOW, KV_W), blk3(WINDOW, KV_W), blk3(CONV_HIST, c_conv)]
                 + [_full(w.shape) for w in wts],
        out_specs=[blk3(T, D), blk3(T * xs_rows, LANES), blk3(T, REC_W),
                   pl.BlockSpec((1, REC_W, nb * T), lambda i: (i, 0, 0)), pl.BlockSpec((1, LANES), lambda i: (0, 0)),
                   blk3(WINDOW, KV_W), blk3(WINDOW, KV_W), blk3(CONV_HIST, c_conv)],
        out_shape=[
            jax.ShapeDtypeStruct((B, T, D), F32),
            jax.ShapeDtypeStruct((B, T * xs_rows, LANES), U32),
            jax.ShapeDtypeStruct((B, T, REC_W), F32),
            jax.ShapeDtypeStruct((B // nb, REC_W, nb * T), F32),
            jax.ShapeDtypeStruct((1, LANES), F32),
            jax.ShapeDtypeStruct((B, WINDOW, KV_W), F32),
            jax.ShapeDtypeStruct((B, WINDOW, KV_W), F32),
            jax.ShapeDtypeStruct((B, CONV_HIST, c_conv), F32),
        ],
        scratch_shapes=[
            pltpu.VMEM((SUBLANES, CONV_PAD + CHUNK, c_conv), F32),
            pltpu.VMEM((nb * CHUNK, Q_W), F32),
            pltpu.VMEM((nb * CHUNK, c_conv), F32),
            pltpu.VMEM((1, LANES), F32),
        ],
        compiler_params=pltpu.CompilerParams(
            dimension_semantics=("arbitrary",), vmem_limit_bytes=VMEM_LIMIT),
        name="sample_mixer",
    )(x, ck, cv, sc, *wts)


def _dispatch_kernel(meta_ref, dest_ref, hp_ref, hs_ref, xs_ref, zbuf, sem, zsem, *, td, n, blk, n_blocks, ntp):
    i = pl.program_id(0)

    def issue_from(h_ref):
        def body(r8, carry):
            for uu in range(DMA_UNROLL):
                r = r8 * DMA_UNROLL + uu
                src = h_ref.at[pl.ds(pl.multiple_of(r * n, n), n)]
                for kk in range(TOP_K):
                    d = pl.multiple_of(dest_ref[kk, 0, 0, r], n)
                    pltpu.make_async_copy(src, xs_ref.at[pl.ds(d, n)], sem).start(priority=kk % 2)
            return carry
        lax.fori_loop(0, td // DMA_UNROLL, body, 0)

    @pl.when(i < ntp)
    def _():
        issue_from(hp_ref)

    @pl.when(i >= ntp)
    def _():
        issue_from(hs_ref)

    def pad_pass(act):
        def pad_expert(e, carry):
            start = meta_ref[e]
            head = meta_ref[N_EXPERTS + e]
            body = meta_ref[2 * N_EXPERTS + e]

            @pl.when(head == 1)
            def _():
                act(pltpu.make_async_copy(zbuf.at[pl.ds(0, n)], xs_ref.at[pl.ds(pl.multiple_of(start * n, n), n)], zsem))
            bit = blk // 2
            while bit >= 2:
                off = pl.multiple_of((start + head + (body // (2 * bit)) * (2 * bit)) * n, 2 * n)

                @pl.when((body // bit) % 2 == 1)
                def _(bit=bit, off=off):
                    act(pltpu.make_async_copy(zbuf.at[pl.ds(0, bit * n)], xs_ref.at[pl.ds(off, bit * n)], zsem))
                bit //= 2
            return carry

        lax.fori_loop(0, N_EXPERTS, pad_expert, 0)

        def pad_block(j, carry):
            act(pltpu.make_async_copy(zbuf, xs_ref.at[pl.ds(pl.multiple_of(j * (blk * n), blk * n), blk * n)], zsem))
            return carry

        lax.fori_loop(meta_ref[3 * N_EXPERTS], n_blocks, pad_block, 0)

    @pl.when(i == pl.num_programs(0) - 1)
    def _():
        zbuf[...] = jnp.zeros_like(zbuf)
        pad_pass(lambda cp: cp.start())
        pad_pass(lambda cp: cp.wait())

    for _ in range(TOP_K):
        pltpu.make_async_copy(hp_ref, xs_ref.at[pl.ds(0, td * n)], sem).wait()


def _dispatch(hp, hs, dest_rows, meta, n_blocks, blk, td):
    rows_p, rows_s = hp.shape[0], hs.shape[0]
    n_tok = dest_rows.shape[1]
    n = (rows_p + rows_s) // n_tok
    ntp, nts = rows_p // (td * n), rows_s // (td * n)
    dest3 = dest_rows.reshape(TOP_K, ntp + nts, 1, td)
    grid_spec = pltpu.PrefetchScalarGridSpec(
        num_scalar_prefetch=1,
        grid=(ntp + nts,),
        in_specs=[
            pl.BlockSpec((TOP_K, 1, 1, td), lambda i, m: (0, i, 0, 0), memory_space=pltpu.SMEM),
            pl.BlockSpec((td * n, LANES), lambda i, m: (jnp.minimum(i, ntp - 1), 0)),
            pl.BlockSpec((td * n, LANES), lambda i, m: (jnp.maximum(i - ntp, 0), 0)),
        ],
        out_specs=pl.BlockSpec(memory_space=pl.ANY),
        scratch_shapes=[pltpu.VMEM((blk * n, LANES), U32), pltpu.SemaphoreType.DMA(()), pltpu.SemaphoreType.DMA(())],
    )
    return pl.pallas_call(
        functools.partial(_dispatch_kernel, td=td, n=n, blk=blk, n_blocks=n_blocks, ntp=ntp),
        grid_spec=grid_spec,
        out_shape=jax.ShapeDtypeStruct((n_blocks * blk * n, LANES), U32),
        compiler_params=pltpu.CompilerParams(
            dimension_semantics=("arbitrary",), vmem_limit_bytes=VMEM_LIMIT),
        name="moe_dispatch",
    )(meta, dest3, hp, hs)


def _expert_kernel(block_e_ref, valid_ref, x_ref, wg_ref, wu_ref, wd_ref, y_ref, wgb, wub, wdb, *, blk):
    i = pl.program_id(0)
    valid = valid_ref[i]
    xn = x_ref.shape[0] // blk
    yn = y_ref.shape[0] // blk
    part = blk // EXPERT_PARTS
    new_expert = (i == 0) | (block_e_ref[i] != block_e_ref[jnp.maximum(i - 1, 0)])

    @pl.when((valid > 0) & new_expert)
    def _():
        wgb[...] = wg_ref[0].astype(BF16)
        wub[...] = wu_ref[0].astype(BF16)
        wdb[...] = wd_ref[0].astype(BF16)

    def swiglu(row0, rows):
        lo, hi = _unpack_rows(_load_tile_rows(x_ref.at[pl.ds(row0 * xn, rows * xn)], (), rows, xn), BF16)
        half = lo.shape[-1]
        g = _dot(lo, wgb[0:half, :]) + _dot(hi, wgb[half:, :])
        u = _dot(lo, wub[0:half, :]) + _dot(hi, wub[half:, :])
        a = (g * jax.nn.sigmoid(g) * u).astype(BF16)
        _store_tile_rows(y_ref.at[pl.ds(row0 * yn, rows * yn)], (), _pack_rows(_dot(a, wdb[...])))

    def zero_rows(row0, rows):
        y_ref[pl.ds(row0 * yn, rows * yn), :] = jnp.zeros((rows * yn, LANES), y_ref.dtype)

    mostly_full = valid > blk - part

    @pl.when(mostly_full)
    def _():
        swiglu(0, blk)

    @pl.when(jnp.logical_not(mostly_full))
    def _():
        for q in range(EXPERT_PARTS - 1):
            @pl.when(valid > q * part)
            def _(q=q):
                swiglu(q * part, part)

            @pl.when(valid <= q * part)
            def _(q=q):
                zero_rows(q * part, part)
        zero_rows(blk - part, part)


def _experts(xs, block_e, valid, wg, wu, wd, blk):
    n_blocks = block_e.shape[0]
    _, D, de = wg.shape
    xn = xs.shape[0] // (n_blocks * blk)
    yn = D // 2 // LANES
    grid_spec = pltpu.PrefetchScalarGridSpec(
        num_scalar_prefetch=2,
        grid=(n_blocks,),
        in_specs=[
            pl.BlockSpec((blk * xn, LANES), lambda i, be, nu: (i, 0)),
            pl.BlockSpec((1, D, de), lambda i, be, nu: (be[i], 0, 0)),
            pl.BlockSpec((1, D, de), lambda i, be, nu: (be[i], 0, 0)),
            pl.BlockSpec((1, de, D), lambda i, be, nu: (be[i], 0, 0)),
        ],
        out_specs=pl.BlockSpec((blk * yn, LANES), lambda i, be, nu: (i, 0)),
        scratch_shapes=[pltpu.VMEM((D, de), BF16), pltpu.VMEM((D, de), BF16), pltpu.VMEM((de, D), BF16)],
    )
    return pl.pallas_call(
        functools.partial(_expert_kernel, blk=blk),
        grid_spec=grid_spec,
        out_shape=jax.ShapeDtypeStruct((n_blocks * blk * yn, LANES), U32),
        compiler_params=pltpu.CompilerParams(
            dimension_semantics=("arbitrary",), vmem_limit_bytes=VMEM_LIMIT),
        name="moe_experts",
    )(block_e, valid, xs, wg, wu, wd)


def _final_kernel(dest_cur, dest_nxt, x1_ref, rec_ref, p_ref, g_ple, w_gate, w_proj, ys_ref, y_ref,
                  buf_a, buf_b, sem, *, tf, n):
    i = pl.program_id(0)
    bufs = (buf_a, buf_b)

    def issue(dref, half, s):
        for r in range(tf):
            for kk in range(TOP_K):
                d = pl.multiple_of(dref[kk, 0, 0, half * tf + r], n)
                pltpu.make_async_copy(ys_ref.at[pl.ds(d, n)], bufs[s].at[kk, pl.ds(r * n, n)],
                                      sem.at[s]).start(priority=kk % 2)

    def drain(s):
        for kk in range(TOP_K):
            pltpu.make_async_copy(ys_ref.at[pl.ds(0, tf * n)], bufs[s].at[kk], sem.at[s]).wait()

    def compute(half, s):
        rows = slice(half * tf, (half + 1) * tf)
        rec = rec_ref[rows, :]
        y1 = jnp.concatenate(_unpack_rows(_load_tile_rows(bufs[s], (0,), tf, n), F32), axis=1)
        y2 = jnp.concatenate(_unpack_rows(_load_tile_rows(bufs[s], (1,), tf, n), F32), axis=1)
        x2 = x1_ref[rows, :] + (rec[:, 2:3] * y1 + rec[:, 3:4] * y2)
        gate = jax.nn.sigmoid(_dot(_rms(x2, g_ple[...]).astype(BF16), w_gate[...]))
        y_ref[rows, :] = x2 + gate * _dot(p_ref[rows, :].astype(BF16), w_proj[...])

    @pl.when(i == 0)
    def _():
        issue(dest_cur, 0, 0)

    drain(0)
    issue(dest_cur, 1, 1)
    compute(0, 0)
    drain(1)
    issue(dest_nxt, 0, 0)
    compute(1, 1)

    @pl.when(i == pl.num_programs(0) - 1)
    def _():
        drain(0)


def _final(x1, rec, p, dest_rows, ys, g_ple, w_gate, w_proj, tf):
    N, D = x1.shape
    n = D // 2 // LANES
    pair = 2 * tf
    assert N % pair == 0
    nt = N // pair
    dest3 = dest_rows.reshape(TOP_K, nt, 1, pair)
    smem_blk = lambda fn: pl.BlockSpec((TOP_K, 1, 1, pair), fn, memory_space=pltpu.SMEM)
    return pl.pallas_call(
        functools.partial(_final_kernel, tf=tf, n=n),
        grid=(nt,),
        in_specs=[
            smem_blk(lambda i: (0, i, 0, 0)),
            smem_blk(lambda i: (0, jnp.minimum(i + 1, nt - 1), 0, 0)),
            pl.BlockSpec((pair, D), lambda i: (i, 0)),
            pl.BlockSpec((pair, REC_W), lambda i: (i, 0)),
            pl.BlockSpec((pair, p.shape[-1]), lambda i: (i, 0)),
            _full(g_ple.shape), _full(w_gate.shape), _full(w_proj.shape),
            pl.BlockSpec(memory_space=pl.ANY),
        ],
        out_specs=pl.BlockSpec((pair, D), lambda i: (i, 0)),
        out_shape=jax.ShapeDtypeStruct((N, D), F32),
        scratch_shapes=[pltpu.VMEM((TOP_K, tf * n, LANES), U32), pltpu.VMEM((TOP_K, tf * n, LANES), U32),
                        pltpu.SemaphoreType.DMA((2,))],
        compiler_params=pltpu.CompilerParams(
            dimension_semantics=("arbitrary",), vmem_limit_bytes=VMEM_LIMIT),
        name="moe_combine_ple",
    )(dest3, dest3, x1, rec, p, g_ple, w_gate, w_proj, ys)


def _slots(rec, pstart, offset):
    e = rec[0:TOP_K, :].astype(I32)
    rank = rec[4:4 + TOP_K, :].astype(I32)
    onehot = e[None, :, :] == jnp.arange(N_EXPERTS, dtype=I32)[:, None, None]
    return rank + jnp.sum(jnp.where(onehot, (pstart + offset)[:, None, None], 0), axis=0)


def _bias_table(sinks):
    slopes = jnp.array([2.0 ** (-8.0 * (i + 1) / N_HEADS) for i in range(N_HEADS)], F32)
    qpos = WINDOW + jnp.arange(CHUNK, dtype=I32)
    col = jnp.arange(BAND_PAD, dtype=I32)
    dist = jnp.abs(qpos[:, None] - col[None, :]).astype(F32)
    core = slopes[:, None, None] * dist[None]
    sink = jnp.broadcast_to(-sinks.astype(F32)[:, None, None], core.shape)
    table = jnp.where(col == BAND, sink, jnp.where(col < BAND, core, MASKED))
    first_valid = jnp.array([WINDOW, WINDOW - CHUNK, 0], I32)
    table = jnp.where(col[None, None, None, :] < first_valid[:, None, None, None], MASKED, table[None])
    return table.reshape(3, N_HEADS * CHUNK, BAND_PAD)


def kernel(x_prompt, x_sample, p_prompt, p_sample, cache_k, cache_v, state_conv, g_mix, w_in, g_q, g_k, sinks,
           w_dw, b_dw, g_cn, b_cn, w_pw, b_pw, g_oa, g_oc, w_out, g_ffn, w_coarse, b_coarse, w_fine, b_fine,
           w_e_gate, w_e_up, w_e_down, g_ple, w_ple_gate, w_ple_proj):
    assert g_mix.shape[0] == 1
    l = 0
    B, T, D = x_prompt.shape
    DB, DS, _ = x_sample.shape
    assert cache_k.shape[2] == WINDOW and DS == CHUNK
    Np, Ns = B * T, DB * DS
    tt, td, tf = _tiles(T, Np, Ns)
    blk = EXPERT_BLOCK

    row = lambda a: a[l].reshape(1, -1)
    perm = jnp.concatenate([jnp.concatenate([jnp.arange(HEAD_DIM) + m * HEAD_DIM,
                                             jnp.arange(HEAD_DIM) + (GQ + m) * HEAD_DIM]) for m in range(GQ)])
    w_in_p = jnp.concatenate([w_in[l][:, perm], w_in[l][:, Q_W:]], axis=1).astype(BF16)
    w_out_p = jnp.concatenate([w_out[l][perm, :], w_out[l][Q_W:, :]], axis=0).astype(BF16)
    g_oa_p = g_oa[l][perm].reshape(1, -1)
    gain_qk = jnp.concatenate([jnp.tile(g_q[l] * (HEAD_DIM ** -0.5), N_HEADS), jnp.tile(g_k[l], N_KV_HEADS)]).reshape(1, -1)
    blk_id = jnp.arange(2 * LANES) // HEAD_DIM
    bd = jnp.where(blk_id[:, None] == blk_id[None, :], 1.0 / HEAD_DIM, 0.0).astype(BF16)
    w_r = jnp.concatenate(
        [w_coarse[l], jnp.transpose(w_fine[l], (1, 0, 2)).reshape(D, N_EXPERTS),
         jnp.zeros((D, LANES - N_GROUPS - N_EXPERTS), F32)], axis=1)
    w_rh = w_r.astype(BF16)
    w_rl = (w_r - w_rh.astype(F32)).astype(BF16)
    w_r2 = jnp.concatenate([w_rh, w_rl], axis=1)
    b_r = jnp.concatenate(
        [b_coarse[l], b_fine[l].reshape(-1), jnp.zeros((LANES - N_GROUPS - N_EXPERTS,), F32)]).reshape(1, LANES)
    bias = _bias_table(sinks[l])
    w_dw_rep = jnp.repeat(w_dw[l], SUBLANES, axis=0)

    def mixer_weights(bias_tbl):
        return (row(g_mix), w_in_p, bd, gain_qk, bias_tbl, w_dw_rep, row(b_dw),
                row(g_cn), row(b_cn), w_pw[l].astype(BF16), row(b_pw), g_oa_p, row(g_oc),
                w_out_p, row(g_ffn), w_r2, w_rh, b_r)

    x1p, hp, recp, rectp, cntp, nkp, nvp, ncp = _prompt_mixer(x_prompt, mixer_weights(bias), tt)
    nb = next(n for n in (8, 4, 2, 1) if DB % n == 0)
    x1s, hs, recs, rects, cnts, nks, nvs, ncs = _sample_mixer(
        x_sample, cache_k[l].reshape(DB, WINDOW, KV_W), cache_v[l].reshape(DB, WINDOW, KV_W), state_conv[l],
        mixer_weights(bias[2:3]), nb)

    recp, recs = recp.reshape(Np, REC_W), recs.reshape(Ns, REC_W)
    rectp = jnp.transpose(rectp, (1, 0, 2)).reshape(REC_W, Np)
    rects = jnp.transpose(rects, (1, 0, 2)).reshape(REC_W, Ns)
    cnt_p = cntp[0, :N_EXPERTS].astype(I32)
    cnt = cnt_p + cnts[0, :N_EXPERTS].astype(I32)
    n_blocks = ((Np + Ns) * TOP_K + N_EXPERTS * (blk - 1)) // blk
    padded = (cnt + blk - 1) // blk * blk
    pend = jnp.cumsum(padded)
    pstart = pend - padded
    dest_p = _slots(rectp, pstart, jnp.zeros_like(cnt_p))
    dest_s = _slots(rects, pstart, cnt_p)
    block_e = jnp.minimum(
        jnp.sum((pend[None, :] <= (jnp.arange(n_blocks, dtype=I32) * blk)[:, None]).astype(I32), axis=1),
        N_EXPERTS - 1).astype(I32)
    n_used = (pend[-1:] // blk).astype(I32)
    pad_start = pstart + cnt
    pad_head = pad_start % 2
    meta = jnp.concatenate([pad_start, pad_head, padded - cnt - pad_head, n_used]).astype(I32)

    xn = D // 2 // LANES
    yn = D // 2 // LANES
    xs = _dispatch(hp.reshape(Np * xn, LANES), hs.reshape(Ns * xn, LANES),
                   jnp.concatenate([dest_p, dest_s], axis=1) * xn, meta, n_blocks, blk, td)
    first_block = jnp.sum(jnp.where(block_e[:, None] == jnp.arange(N_EXPERTS, dtype=I32)[None, :],
                                    (pstart // blk)[None, :], 0), axis=1)
    cnt_of_block = jnp.sum(jnp.where(block_e[:, None] == jnp.arange(N_EXPERTS, dtype=I32)[None, :],
                                     cnt[None, :], 0), axis=1)
    valid = jnp.clip(cnt_of_block - (jnp.arange(n_blocks, dtype=I32) - first_block) * blk, 0, blk).astype(I32)
    ys = _experts(xs, block_e, valid, w_e_gate[l], w_e_up[l], w_e_down[l], blk)

    gp = row(g_ple)
    w_gate = w_ple_gate[l].astype(BF16)
    w_proj = w_ple_proj[l].astype(BF16)
    yp = _final(x1p.reshape(Np, D), recp, p_prompt[l].reshape(Np, -1), dest_p * yn, ys, gp, w_gate, w_proj, tf)
    ysm = _final(x1s.reshape(Ns, D), recs, p_sample[l].reshape(Ns, -1), dest_s * yn, ys, gp, w_gate, w_proj, tf)

    kv5 = lambda a, nbat: a.reshape(1, nbat, WINDOW, N_KV_HEADS, HEAD_DIM)
    return (yp.reshape(B, T, D), ysm.reshape(DB, DS, D),
            kv5(nkp, B), kv5(nvp, B), ncp[None],
            kv5(nks, DB), kv5(nvs, DB), ncs[None])
```

```python
import functools

import jax
import jax.numpy as jnp
from jax import lax
from jax.experimental import pallas as pl
from jax.experimental.pallas import tpu as pltpu

CHUNK = 64
HEAD_DIM = 64
N_HEADS = 8
N_KV_HEADS = 2
GQ = N_HEADS // N_KV_HEADS
Q_W = N_HEADS * HEAD_DIM
KV_W = N_KV_HEADS * HEAD_DIM
WINDOW = 128
BAND = WINDOW + CHUNK
BAND_PAD = 256
CONV_WIDTH = 31
CONV_HIST = CONV_WIDTH - 1
CONV_PAD = 32
CONV_BLOCK = 32
SUBLANES = 8
LANES = 128
N_GROUPS = 4
EXPERTS_PER_GROUP = 8
N_EXPERTS = N_GROUPS * EXPERTS_PER_GROUP
TOP_K = 2
EXPERT_BLOCK = 1024
EPS = 1e-6
REC_W = 8
FINE0 = 8
MASKED = 1e30
DMA_UNROLL = 8

F32 = jnp.float32
BF16 = jnp.bfloat16
U32 = jnp.uint32
I32 = jnp.int32

VMEM_LIMIT = 56 * 1024 * 1024


def _tiles(n_prompt_seq, n_prompt, n_sample):
    tt = next((t for t in (512, 256) if n_prompt_seq % t == 0), n_prompt_seq)
    td = next(t for t in (2048, 1024, 512, 256, 128, 64) if n_prompt % t == 0 and n_sample % t == 0)
    tf = min(td, 512)
    return tt, td, tf


def _rms(xf, g):
    return xf * lax.rsqrt(jnp.mean(xf * xf, axis=-1, keepdims=True) + EPS) * g


def _dot(a, b):
    return jnp.dot(a, b, preferred_element_type=F32)


def _dot_nt(a, b):
    return lax.dot_general(a, b, (((1,), (1,)), ((), ())), preferred_element_type=F32)


def _store_tile_rows(ref, lead, value):
    rows, width = value.shape
    n = width // LANES
    for c in range(n):
        ref[lead + (pl.ds(c, rows, stride=n), slice(None))] = value[:, c * LANES:(c + 1) * LANES]


def _load_tile_rows(ref, lead, rows, n):
    return jnp.concatenate([ref[lead + (pl.ds(c, rows, stride=n), slice(None))] for c in range(n)], axis=1)


def _project(x, g_mix, w_in_ref, bd_ref, gain_qk):
    xn = _rms(x, g_mix).astype(BF16)
    z = _dot(xn, w_in_ref[...])
    qk = z[:, 0:Q_W + KV_W]
    sq = (qk * qk).astype(BF16)
    bd = bd_ref[...]
    two = 2 * LANES
    ms = jnp.concatenate(
        [_dot(sq[:, 0:two], bd), _dot(sq[:, two:2 * two], bd), _dot(sq[:, 2 * two:], bd[0:KV_W, 0:KV_W])], axis=1)
    qkn = qk * lax.rsqrt(ms + EPS) * gain_qk
    v = z[:, Q_W + KV_W:Q_W + 2 * KV_W]
    c0 = Q_W + 2 * KV_W
    c_conv = (z.shape[1] - c0) // 2
    u = z[:, c0:c0 + c_conv] * jax.nn.sigmoid(z[:, c0 + c_conv:])
    return qkn, v, u


def _attend_chunk(q_groups, k_band, v_band, bias):
    lane = lax.broadcasted_iota(I32, (CHUNK, LANES), 1)
    low = lane < HEAD_DIM
    zero = jnp.zeros((CHUNK, LANES), F32)
    q_all = jnp.concatenate(
        [jnp.where(low, qg, zero) for qg in q_groups] + [jnp.where(low, zero, qg) for qg in q_groups],
        axis=0).astype(BF16)
    pad = jnp.zeros((BAND_PAD - BAND, LANES), BF16)
    s = _dot_nt(q_all, jnp.concatenate([k_band, pad], axis=0)) - bias
    m = jnp.max(s, axis=-1, keepdims=True)
    e = jnp.exp(s - m)
    denom = jnp.sum(e, axis=-1, keepdims=True)
    o = _dot(e.astype(BF16), jnp.concatenate([v_band, pad], axis=0)) * (1.0 / denom)
    half = GQ * CHUNK
    return [jnp.where(low, o[g * CHUNK:(g + 1) * CHUNK, :], o[half + g * CHUNK:half + (g + 1) * CHUNK, :])
            for g in range(GQ)]


def _shift_copies(ush, rows):
    for n in range(1, SUBLANES):
        ush[n, 0:rows, :] = ush[0, n:n + rows, :]


def _conv_rows(ush, r0, rows, w_dw_ref, b_dw):
    ch = b_dw.shape[-1]
    groups = rows // SUBLANES
    acc = jnp.broadcast_to(b_dw, (groups, SUBLANES, ch))
    for j in range(CONV_WIDTH):
        off = CONV_PAD - CONV_HIST + j
        taps = ush[off % SUBLANES, pl.ds(r0 + off - off % SUBLANES, rows), :].reshape(groups, SUBLANES, ch)
        acc = acc + w_dw_ref[j * SUBLANES:(j + 1) * SUBLANES, :][None] * taps
    return acc.reshape(rows, ch)


def _finish(x, oa, conv, refs):
    (g_cn, b_cn, w_pw, b_pw, g_oa, g_oc, w_out, g_ffn, w_r2, w_rh, b_r) = refs
    mu = jnp.mean(conv, axis=-1, keepdims=True)
    cen = conv - mu
    var = jnp.mean(cen * cen, axis=-1, keepdims=True)
    ln = cen * lax.rsqrt(var + EPS) * g_cn[...] + b_cn[...]
    act = (ln * jax.nn.sigmoid(ln)).astype(BF16)
    c = _dot(act, w_pw[...]) + b_pw[...]
    half = oa.shape[-1]
    mixed = (_dot(_rms(oa, g_oa[...]).astype(BF16), w_out[0:half, :])
             + _dot(_rms(c, g_oc[...]).astype(BF16), w_out[half:, :]))
    x1 = x + mixed
    h = _rms(x1, g_ffn[...])
    h_hi = h.astype(BF16)
    h_lo = (h - h_hi.astype(F32)).astype(BF16)
    a = _dot(h_hi, w_r2[...])
    logits = a[:, 0:LANES] + a[:, LANES:] + _dot(h_lo, w_rh[...]) + b_r[...]
    return x1, h, logits


def _pack_rows(h):
    half = h.shape[-1] // 2
    hb = h.astype(BF16).astype(F32)
    lo = lax.bitcast_convert_type(hb[:, 0:half], U32)
    hi = lax.bitcast_convert_type(hb[:, half:], U32)
    return (lo >> 16) | (hi & jnp.uint32(0xFFFF0000))


def _unpack_rows(w, dtype):
    lo = lax.bitcast_convert_type(w << 16, F32).astype(dtype)
    hi = lax.bitcast_convert_type(w & jnp.uint32(0xFFFF0000), F32).astype(dtype)
    return lo, hi


def _route(logits, cnt_ref):
    rows = logits.shape[0]
    lt = jnp.transpose(logits)
    ninf = -jnp.inf
    crow = lax.broadcasted_iota(I32, (SUBLANES, rows), 0).astype(F32)
    lc = jnp.where(crow < N_GROUPS, lt[0:SUBLANES, :], ninf)
    mc = jnp.max(lc, axis=0, keepdims=True)
    grp = jnp.min(jnp.where(lc == mc, crow, float(SUBLANES)), axis=0, keepdims=True)
    g1 = 1.0 / jnp.sum(jnp.exp(lc - mc), axis=0, keepdims=True)
    erow = lax.broadcasted_iota(I32, (N_EXPERTS, rows), 0).astype(F32)
    lo = grp * EXPERTS_PER_GROUP
    lf = jnp.where((erow >= lo) & (erow < lo + EXPERTS_PER_GROUP), lt[FINE0:FINE0 + N_EXPERTS, :], ninf)
    t1 = jnp.max(lf, axis=0, keepdims=True)
    e1 = jnp.min(jnp.where(lf == t1, erow, float(N_EXPERTS)), axis=0, keepdims=True)
    lf2 = jnp.where(erow == e1, ninf, lf)
    t2 = jnp.max(lf2, axis=0, keepdims=True)
    e2 = jnp.min(jnp.where(lf2 == t2, erow, float(N_EXPERTS)), axis=0, keepdims=True)
    e2x = jnp.exp(t2 - t1)
    inv = 1.0 / (1.0 + e2x)
    w1 = g1 * inv
    w2 = g1 * (e2x * inv)
    oh1 = (erow == e1).astype(F32)
    oh2 = (erow == e2).astype(F32)
    oh = oh1 + oh2
    ri = lax.broadcasted_iota(I32, (rows, rows), 0)
    ci = lax.broadcasted_iota(I32, (rows, rows), 1)
    before = (ri < ci).astype(BF16)
    cnt = cnt_ref[...]
    tot = _dot(oh.astype(BF16), before) + jnp.tile(cnt, (1, rows // LANES))
    r1 = jnp.sum(oh1 * tot, axis=0, keepdims=True)
    r2 = jnp.sum(oh2 * tot, axis=0, keepdims=True)
    cnt_ref[...] = cnt + jnp.sum(oh, axis=1, keepdims=True)
    row = lax.broadcasted_iota(I32, (REC_W, rows), 0)
    rec_t = jnp.where(row == 0, e1, 0.0)
    rec_t = jnp.where(row == 1, e2, rec_t)
    rec_t = jnp.where(row == 2, w1, rec_t)
    rec_t = jnp.where(row == 3, w2, rec_t)
    rec_t = jnp.where(row == 4, r1, rec_t)
    rec_t = jnp.where(row == 5, r2, rec_t)
    full = jnp.concatenate([rec_t, jnp.zeros((LANES - REC_W, rows), F32)], axis=0)
    return jnp.transpose(full)[:, 0:REC_W], rec_t


def _prompt_mixer_kernel(x_ref, g_mix, w_in, bd, gain_qk, bias_ref, w_dw, b_dw,
                         g_cn, b_cn, w_pw, b_pw, g_oa, g_oc, w_out, g_ffn, w_r2, w_rh, b_r,
                         x1_ref, h_ref, rec_ref, rect_ref, cnt_out, nk_ref, nv_ref, nc_ref,
                         kx, vx, ush, oa_s, conv_s, cnt_s, *, tt):
    b = pl.program_id(0)
    t = pl.program_id(1)
    n_chunks = tt // CHUNK

    @pl.when((b == 0) & (t == 0))
    def _():
        cnt_s[...] = jnp.zeros_like(cnt_s)

    @pl.when(t == 0)
    def _():
        kx[0:WINDOW, :] = jnp.zeros((WINDOW, LANES), BF16)
        vx[0:WINDOW, :] = jnp.zeros((WINDOW, LANES), BF16)
        ush[0, 0:CONV_PAD, :] = jnp.zeros((CONV_PAD, ush.shape[-1]), F32)

    x = x_ref[0]
    qkn, v, u = _project(x, g_mix[...], w_in, bd, gain_qk[...])
    k = qkn[:, Q_W:]
    kx[WINDOW:WINDOW + tt, :] = k.astype(BF16)
    vx[WINDOW:WINDOW + tt, :] = v.astype(BF16)
    ush[0, CONV_PAD:CONV_PAD + tt, :] = u
    _shift_copies(ush, tt + CONV_PAD - SUBLANES)

    for c in range(n_chunks):
        variant = jnp.minimum(t * n_chunks + c, WINDOW // CHUNK)
        rows = slice(c * CHUNK, (c + 1) * CHUNK)
        q_groups = [qkn[rows, m * LANES:(m + 1) * LANES] for m in range(GQ)]
        o_groups = _attend_chunk(q_groups, kx[c * CHUNK:c * CHUNK + BAND, :], vx[c * CHUNK:c * CHUNK + BAND, :],
                                 bias_ref[variant])
        for m in range(GQ):
            oa_s[rows, m * LANES:(m + 1) * LANES] = o_groups[m]

    for rb in range(tt // CONV_BLOCK):
        conv_s[rb * CONV_BLOCK:(rb + 1) * CONV_BLOCK, :] = _conv_rows(ush, rb * CONV_BLOCK, CONV_BLOCK, w_dw, b_dw[...])

    x1, hh, logits = _finish(x, oa_s[...], conv_s[...],
                             (g_cn, b_cn, w_pw, b_pw, g_oa, g_oc, w_out, g_ffn, w_r2, w_rh, b_r))
    x1_ref[0] = x1
    _store_tile_rows(h_ref, (0,), _pack_rows(hh))
    rec_ref[0], rect_ref[0] = _route(logits, cnt_s)
    cnt_out[...] = cnt_s[...]

    nk_ref[0] = k[tt - WINDOW:, :]
    nv_ref[0] = v[tt - WINDOW:, :]
    nc_ref[0] = u[tt - CONV_HIST:, :]

    kx[0:WINDOW, :] = kx[tt:tt + WINDOW, :]
    vx[0:WINDOW, :] = vx[tt:tt + WINDOW, :]
    ush[0, 0:CONV_PAD, :] = ush[0, tt:tt + CONV_PAD, :]


def _sample_mixer_kernel(x_ref, ck_ref, cv_ref, sc_ref, g_mix, w_in, bd, gain_qk, bias_ref, w_dw, b_dw,
                         g_cn, b_cn, w_pw, b_pw, g_oa, g_oc, w_out, g_ffn, w_r2, w_rh, b_r,
                         x1_ref, h_ref, rec_ref, rect_ref, cnt_out, nk_ref, nv_ref, nc_ref,
                         ush, oa_s, conv_s, cnt_s, *, nb):
    i = pl.program_id(0)

    @pl.when(i == 0)
    def _():
        cnt_s[...] = jnp.zeros_like(cnt_s)

    rows_all = nb * CHUNK
    x = x_ref[...].reshape(rows_all, x_ref.shape[-1])
    qkn, v, u = _project(x, g_mix[...], w_in, bd, gain_qk[...])
    k = qkn[:, Q_W:]
    for j in range(nb):
        rows = slice(j * CHUNK, (j + 1) * CHUNK)
        ck = ck_ref[j]
        cv = cv_ref[j]
        k_band = jnp.concatenate([ck, k[rows, :]], axis=0)
        v_band = jnp.concatenate([cv, v[rows, :]], axis=0)
        q_groups = [qkn[rows, m * LANES:(m + 1) * LANES] for m in range(GQ)]
        o_groups = _attend_chunk(q_groups, k_band.astype(BF16), v_band.astype(BF16), bias_ref[0])
        for m in range(GQ):
            oa_s[rows, m * LANES:(m + 1) * LANES] = o_groups[m]
        ush[0, 0:CONV_PAD, :] = jnp.zeros((CONV_PAD, ush.shape[-1]), F32)
        ush[0, CONV_PAD - CONV_HIST:CONV_PAD, :] = sc_ref[j]
        ush[0, CONV_PAD:CONV_PAD + CHUNK, :] = u[rows, :]
        _shift_copies(ush, CHUNK + CONV_PAD - SUBLANES)
        for rb in range(CHUNK // CONV_BLOCK):
            r0 = rb * CONV_BLOCK
            conv_s[j * CHUNK + r0:j * CHUNK + r0 + CONV_BLOCK, :] = _conv_rows(ush, r0, CONV_BLOCK, w_dw, b_dw[...])
        nk_ref[j] = k_band[CHUNK:, :]
        nv_ref[j] = v_band[CHUNK:, :]
        nc_ref[j] = ush[0, CONV_PAD + CHUNK - CONV_HIST:CONV_PAD + CHUNK, :]

    x1, hh, logits = _finish(x, oa_s[...], conv_s[...],
                             (g_cn, b_cn, w_pw, b_pw, g_oa, g_oc, w_out, g_ffn, w_r2, w_rh, b_r))
    x1_ref[...] = x1.reshape(x1_ref.shape)
    packed = _pack_rows(hh)
    for j in range(nb):
        _store_tile_rows(h_ref, (j,), packed[j * CHUNK:(j + 1) * CHUNK, :])
    rec, rec_t = _route(logits, cnt_s)
    rec_ref[...] = rec.reshape(rec_ref.shape)
    rect_ref[0] = rec_t
    cnt_out[...] = cnt_s[...]


def _full(shape):
    nd = len(shape)
    return pl.BlockSpec(shape, lambda *_: (0,) * nd)


def _prompt_mixer(x, wts, tt):
    B, T, D = x.shape
    c_conv = wts[6].shape[-1]
    xs_rows = D // 2 // LANES
    nt = T // tt
    tok = lambda rows, last: pl.BlockSpec((1, rows, last), lambda b, t: (b, t, 0))
    per_batch = lambda rows, last: pl.BlockSpec((1, rows, last), lambda b, t: (b, 0, 0))
    return pl.pallas_call(
        functools.partial(_prompt_mixer_kernel, tt=tt),
        grid=(B, nt),
        in_specs=[tok(tt, D)] + [_full(w.shape) for w in wts],
        out_specs=[tok(tt, D), tok(tt * xs_rows, LANES), tok(tt, REC_W),
                   pl.BlockSpec((1, REC_W, tt), lambda b, t: (b, 0, t)), pl.BlockSpec((N_EXPERTS, LANES), lambda b, t: (0, 0)),
                   per_batch(WINDOW, KV_W), per_batch(WINDOW, KV_W), per_batch(CONV_HIST, c_conv)],
        out_shape=[
            jax.ShapeDtypeStruct((B, T, D), F32),
            jax.ShapeDtypeStruct((B, T * xs_rows, LANES), U32),
            jax.ShapeDtypeStruct((B, T, REC_W), F32),
            jax.ShapeDtypeStruct((B, REC_W, T), F32),
            jax.ShapeDtypeStruct((N_EXPERTS, LANES), F32),
            jax.ShapeDtypeStruct((B, WINDOW, KV_W), F32),
            jax.ShapeDtypeStruct((B, WINDOW, KV_W), F32),
            jax.ShapeDtypeStruct((B, CONV_HIST, c_conv), F32),
        ],
        scratch_shapes=[
            pltpu.VMEM((WINDOW + tt, LANES), BF16),
            pltpu.VMEM((WINDOW + tt, LANES), BF16),
            pltpu.VMEM((SUBLANES, CONV_PAD + tt, c_conv), F32),
            pltpu.VMEM((tt, Q_W), F32),
            pltpu.VMEM((tt, c_conv), F32),
            pltpu.VMEM((N_EXPERTS, LANES), F32),
        ],
        compiler_params=pltpu.CompilerParams(
            dimension_semantics=("arbitrary", "arbitrary"), vmem_limit_bytes=VMEM_LIMIT),
        name="prompt_mixer",
    )(x, *wts)


def _sample_mixer(x, ck, cv, sc, wts, nb):
    B, T, D = x.shape
    assert T == CHUNK and B % nb == 0
    c_conv = wts[6].shape[-1]
    xs_rows = D // 2 // LANES
    blk3 = lambda rows, last: pl.BlockSpec((nb, rows, last), lambda i: (i, 0, 0))
    return pl.pallas_call(
        functools.partial(_sample_mixer_kernel, nb=nb),
        grid=(B // nb,),
        in_specs=[blk3(T, D), blk3(WINDOW, KV_W), blk3(WINDOW, KV_W), blk3(CONV_HIST, c_conv)]
                 + [_full(w.shape) for w in wts],
        out_specs=[blk3(T, D), blk3(T * xs_rows, LANES), blk3(T, REC_W),
                   pl.BlockSpec((1, REC_W, nb * T), lambda i: (i, 0, 0)), pl.BlockSpec((N_EXPERTS, LANES), lambda i: (0, 0)),
                   blk3(WINDOW, KV_W), blk3(WINDOW, KV_W), blk3(CONV_HIST, c_conv)],
        out_shape=[
            jax.ShapeDtypeStruct((B, T, D), F32),
            jax.ShapeDtypeStruct((B, T * xs_rows, LANES), U32),
            jax.ShapeDtypeStruct((B, T, REC_W), F32),
            jax.ShapeDtypeStruct((B // nb, REC_W, nb * T), F32),
            jax.ShapeDtypeStruct((N_EXPERTS, LANES), F32),
            jax.ShapeDtypeStruct((B, WINDOW, KV_W), F32),
            jax.ShapeDtypeStruct((B, WINDOW, KV_W), F32),
            jax.ShapeDtypeStruct((B, CONV_HIST, c_conv), F32),
        ],
        scratch_shapes=[
            pltpu.VMEM((SUBLANES, CONV_PAD + CHUNK, c_conv), F32),
            pltpu.VMEM((nb * CHUNK, Q_W), F32),
            pltpu.VMEM((nb * CHUNK, c_conv), F32),
            pltpu.VMEM((N_EXPERTS, LANES), F32),
        ],
        compiler_params=pltpu.CompilerParams(
            dimension_semantics=("arbitrary",), vmem_limit_bytes=VMEM_LIMIT),
        name="sample_mixer",
    )(x, ck, cv, sc, *wts)


def _dispatch_kernel(meta_ref, dest_ref, hp_ref, hs_ref, xs_ref, zbuf, sem, zsem, *, td, n, blk, n_blocks, ntp):
    i = pl.program_id(0)

    def issue_from(h_ref):
        def body(r8, carry):
            for uu in range(DMA_UNROLL):
                r = r8 * DMA_UNROLL + uu
                src = h_ref.at[pl.ds(pl.multiple_of(r * n, n), n)]
                for kk in range(TOP_K):
                    d = pl.multiple_of(dest_ref[kk, 0, 0, r], n)
                    pltpu.make_async_copy(src, xs_ref.at[pl.ds(d, n)], sem).start(priority=kk % 2)
            return carry
        lax.fori_loop(0, td // DMA_UNROLL, body, 0)

    @pl.when(i < ntp)
    def _():
        issue_from(hp_ref)

    @pl.when(i >= ntp)
    def _():
        issue_from(hs_ref)

    def pad_pass(act):
        def pad_expert(e, carry):
            start = meta_ref[e]
            head = meta_ref[N_EXPERTS + e]
            body = meta_ref[2 * N_EXPERTS + e]

            @pl.when(head == 1)
            def _():
                act(pltpu.make_async_copy(zbuf.at[pl.ds(0, n)], xs_ref.at[pl.ds(pl.multiple_of(start * n, n), n)], zsem))
            bit = blk // 2
            while bit >= 2:
                off = pl.multiple_of((start + head + (body // (2 * bit)) * (2 * bit)) * n, 2 * n)

                @pl.when((body // bit) % 2 == 1)
                def _(bit=bit, off=off):
                    act(pltpu.make_async_copy(zbuf.at[pl.ds(0, bit * n)], xs_ref.at[pl.ds(off, bit * n)], zsem))
                bit //= 2
            return carry

        lax.fori_loop(0, N_EXPERTS, pad_expert, 0)

        def pad_block(j, carry):
            act(pltpu.make_async_copy(zbuf, xs_ref.at[pl.ds(pl.multiple_of(j * (blk * n), blk * n), blk * n)], zsem))
            return carry

        lax.fori_loop(meta_ref[3 * N_EXPERTS], n_blocks, pad_block, 0)

    @pl.when(i == pl.num_programs(0) - 1)
    def _():
        zbuf[...] = jnp.zeros_like(zbuf)
        pad_pass(lambda cp: cp.start())
        pad_pass(lambda cp: cp.wait())

    for _ in range(TOP_K):
        pltpu.make_async_copy(hp_ref, xs_ref.at[pl.ds(0, td * n)], sem).wait()


def _dispatch(hp, hs, dest_rows, meta, n_blocks, blk, td):
    rows_p, rows_s = hp.shape[0], hs.shape[0]
    n_tok = dest_rows.shape[1]
    n = (rows_p + rows_s) // n_tok
    ntp, nts = rows_p // (td * n), rows_s // (td * n)
    dest3 = dest_rows.reshape(TOP_K, ntp + nts, 1, td)
    grid_spec = pltpu.PrefetchScalarGridSpec(
        num_scalar_prefetch=1,
        grid=(ntp + nts,),
        in_specs=[
            pl.BlockSpec((TOP_K, 1, 1, td), lambda i, m: (0, i, 0, 0), memory_space=pltpu.SMEM),
            pl.BlockSpec((td * n, LANES), lambda i, m: (jnp.minimum(i, ntp - 1), 0)),
            pl.BlockSpec((td * n, LANES), lambda i, m: (jnp.maximum(i - ntp, 0), 0)),
        ],
        out_specs=pl.BlockSpec(memory_space=pl.ANY),
        scratch_shapes=[pltpu.VMEM((blk * n, LANES), U32), pltpu.SemaphoreType.DMA(()), pltpu.SemaphoreType.DMA(())],
    )
    return pl.pallas_call(
        functools.partial(_dispatch_kernel, td=td, n=n, blk=blk, n_blocks=n_blocks, ntp=ntp),
        grid_spec=grid_spec,
        out_shape=jax.ShapeDtypeStruct((n_blocks * blk * n, LANES), U32),
        compiler_params=pltpu.CompilerParams(
            dimension_semantics=("arbitrary",), vmem_limit_bytes=VMEM_LIMIT),
        name="moe_dispatch",
    )(meta, dest3, hp, hs)


def _expert_kernel(block_e_ref, n_used_ref, x_ref, wg_ref, wu_ref, wd_ref, y_ref, wgb, wub, wdb, *, blk):
    i = pl.program_id(0)
    used = i < n_used_ref[0]
    new_expert = (i == 0) | (block_e_ref[i] != block_e_ref[jnp.maximum(i - 1, 0)])

    @pl.when(used & new_expert)
    def _():
        wgb[...] = wg_ref[0].astype(BF16)
        wub[...] = wu_ref[0].astype(BF16)
        wdb[...] = wd_ref[0].astype(BF16)

    @pl.when(used)
    def _():
        lo, hi = _unpack_rows(_load_tile_rows(x_ref, (), blk, x_ref.shape[0] // blk), BF16)
        half = lo.shape[-1]
        g = _dot(lo, wgb[0:half, :]) + _dot(hi, wgb[half:, :])
        u = _dot(lo, wub[0:half, :]) + _dot(hi, wub[half:, :])
        a = (g * jax.nn.sigmoid(g) * u).astype(BF16)
        _store_tile_rows(y_ref, (), _pack_rows(_dot(a, wdb[...])))

    @pl.when(jnp.logical_not(used))
    def _():
        y_ref[...] = jnp.zeros_like(y_ref)


def _experts(xs, block_e, n_used, wg, wu, wd, blk):
    n_blocks = block_e.shape[0]
    _, D, de = wg.shape
    xn = xs.shape[0] // (n_blocks * blk)
    yn = D // 2 // LANES
    grid_spec = pltpu.PrefetchScalarGridSpec(
        num_scalar_prefetch=2,
        grid=(n_blocks,),
        in_specs=[
            pl.BlockSpec((blk * xn, LANES), lambda i, be, nu: (i, 0)),
            pl.BlockSpec((1, D, de), lambda i, be, nu: (be[i], 0, 0)),
            pl.BlockSpec((1, D, de), lambda i, be, nu: (be[i], 0, 0)),
            pl.BlockSpec((1, de, D), lambda i, be, nu: (be[i], 0, 0)),
        ],
        out_specs=pl.BlockSpec((blk * yn, LANES), lambda i, be, nu: (i, 0)),
        scratch_shapes=[pltpu.VMEM((D, de), BF16), pltpu.VMEM((D, de), BF16), pltpu.VMEM((de, D), BF16)],
    )
    return pl.pallas_call(
        functools.partial(_expert_kernel, blk=blk),
        grid_spec=grid_spec,
        out_shape=jax.ShapeDtypeStruct((n_blocks * blk * yn, LANES), U32),
        compiler_params=pltpu.CompilerParams(
            dimension_semantics=("arbitrary",), vmem_limit_bytes=VMEM_LIMIT),
        name="moe_experts",
    )(block_e, n_used, xs, wg, wu, wd)


def _final_kernel(dest_cur, dest_nxt, x1_ref, rec_ref, p_ref, g_ple, w_gate, w_proj, ys_ref, y_ref,
                  buf_a, buf_b, sem, *, tf, n):
    i = pl.program_id(0)
    bufs = (buf_a, buf_b)

    def issue(dref, half, s):
        for r in range(tf):
            for kk in range(TOP_K):
                d = pl.multiple_of(dref[kk, 0, 0, half * tf + r], n)
                pltpu.make_async_copy(ys_ref.at[pl.ds(d, n)], bufs[s].at[kk, pl.ds(r * n, n)],
                                      sem.at[s]).start(priority=kk % 2)

    def drain(s):
        for kk in range(TOP_K):
            pltpu.make_async_copy(ys_ref.at[pl.ds(0, tf * n)], bufs[s].at[kk], sem.at[s]).wait()

    def compute(half, s):
        rows = slice(half * tf, (half + 1) * tf)
        rec = rec_ref[rows, :]
        y1 = jnp.concatenate(_unpack_rows(_load_tile_rows(bufs[s], (0,), tf, n), F32), axis=1)
        y2 = jnp.concatenate(_unpack_rows(_load_tile_rows(bufs[s], (1,), tf, n), F32), axis=1)
        x2 = x1_ref[rows, :] + (rec[:, 2:3] * y1 + rec[:, 3:4] * y2)
        gate = jax.nn.sigmoid(_dot(_rms(x2, g_ple[...]).astype(BF16), w_gate[...]))
        y_ref[rows, :] = x2 + gate * _dot(p_ref[rows, :].astype(BF16), w_proj[...])

    @pl.when(i == 0)
    def _():
        issue(dest_cur, 0, 0)

    drain(0)
    issue(dest_cur, 1, 1)
    compute(0, 0)
    drain(1)
    issue(dest_nxt, 0, 0)
    compute(1, 1)

    @pl.when(i == pl.num_programs(0) - 1)
    def _():
        drain(0)


def _final(x1, rec, p, dest_rows, ys, g_ple, w_gate, w_proj, tf):
    N, D = x1.shape
    n = D // 2 // LANES
    pair = 2 * tf
    assert N % pair == 0
    nt = N // pair
    dest3 = dest_rows.reshape(TOP_K, nt, 1, pair)
    smem_blk = lambda fn: pl.BlockSpec((TOP_K, 1, 1, pair), fn, memory_space=pltpu.SMEM)
    return pl.pallas_call(
        functools.partial(_final_kernel, tf=tf, n=n),
        grid=(nt,),
        in_specs=[
            smem_blk(lambda i: (0, i, 0, 0)),
            smem_blk(lambda i: (0, jnp.minimum(i + 1, nt - 1), 0, 0)),
            pl.BlockSpec((pair, D), lambda i: (i, 0)),
            pl.BlockSpec((pair, REC_W), lambda i: (i, 0)),
            pl.BlockSpec((pair, p.shape[-1]), lambda i: (i, 0)),
            _full(g_ple.shape), _full(w_gate.shape), _full(w_proj.shape),
            pl.BlockSpec(memory_space=pl.ANY),
        ],
        out_specs=pl.BlockSpec((pair, D), lambda i: (i, 0)),
        out_shape=jax.ShapeDtypeStruct((N, D), F32),
        scratch_shapes=[pltpu.VMEM((TOP_K, tf * n, LANES), U32), pltpu.VMEM((TOP_K, tf * n, LANES), U32),
                        pltpu.SemaphoreType.DMA((2,))],
        compiler_params=pltpu.CompilerParams(
            dimension_semantics=("arbitrary",), vmem_limit_bytes=VMEM_LIMIT),
        name="moe_combine_ple",
    )(dest3, dest3, x1, rec, p, g_ple, w_gate, w_proj, ys)


def _slots(rec, pstart, offset):
    e = rec[0:TOP_K, :].astype(I32)
    rank = rec[4:4 + TOP_K, :].astype(I32)
    onehot = e[None, :, :] == jnp.arange(N_EXPERTS, dtype=I32)[:, None, None]
    return rank + jnp.sum(jnp.where(onehot, (pstart + offset)[:, None, None], 0), axis=0)


def _bias_table(sinks):
    slopes = jnp.array([2.0 ** (-8.0 * (i + 1) / N_HEADS) for i in range(N_HEADS)], F32)
    qpos = WINDOW + jnp.arange(CHUNK, dtype=I32)
    col = jnp.arange(BAND_PAD, dtype=I32)
    dist = jnp.abs(qpos[:, None] - col[None, :]).astype(F32)
    core = slopes[:, None, None] * dist[None]
    sink = jnp.broadcast_to(-sinks.astype(F32)[:, None, None], core.shape)
    table = jnp.where(col == BAND, sink, jnp.where(col < BAND, core, MASKED))
    first_valid = jnp.array([WINDOW, WINDOW - CHUNK, 0], I32)
    table = jnp.where(col[None, None, None, :] < first_valid[:, None, None, None], MASKED, table[None])
    return table.reshape(3, N_HEADS * CHUNK, BAND_PAD)


def kernel(x_prompt, x_sample, p_prompt, p_sample, cache_k, cache_v, state_conv, g_mix, w_in, g_q, g_k, sinks,
           w_dw, b_dw, g_cn, b_cn, w_pw, b_pw, g_oa, g_oc, w_out, g_ffn, w_coarse, b_coarse, w_fine, b_fine,
           w_e_gate, w_e_up, w_e_down, g_ple, w_ple_gate, w_ple_proj):
    assert g_mix.shape[0] == 1
    l = 0
    B, T, D = x_prompt.shape
    DB, DS, _ = x_sample.shape
    assert cache_k.shape[2] == WINDOW and DS == CHUNK
    Np, Ns = B * T, DB * DS
    tt, td, tf = _tiles(T, Np, Ns)
    blk = EXPERT_BLOCK

    row = lambda a: a[l].reshape(1, -1)
    perm = jnp.concatenate([jnp.concatenate([jnp.arange(HEAD_DIM) + m * HEAD_DIM,
                                             jnp.arange(HEAD_DIM) + (GQ + m) * HEAD_DIM]) for m in range(GQ)])
    w_in_p = jnp.concatenate([w_in[l][:, perm], w_in[l][:, Q_W:]], axis=1).astype(BF16)
    w_out_p = jnp.concatenate([w_out[l][perm, :], w_out[l][Q_W:, :]], axis=0).astype(BF16)
    g_oa_p = g_oa[l][perm].reshape(1, -1)
    gain_qk = jnp.concatenate([jnp.tile(g_q[l] * (HEAD_DIM ** -0.5), N_HEADS), jnp.tile(g_k[l], N_KV_HEADS)]).reshape(1, -1)
    blk_id = jnp.arange(2 * LANES) // HEAD_DIM
    bd = jnp.where(blk_id[:, None] == blk_id[None, :], 1.0 / HEAD_DIM, 0.0).astype(BF16)
    w_r = jnp.concatenate(
        [w_coarse[l], jnp.zeros((D, FINE0 - N_GROUPS), F32),
         jnp.transpose(w_fine[l], (1, 0, 2)).reshape(D, N_EXPERTS),
         jnp.zeros((D, LANES - FINE0 - N_EXPERTS), F32)], axis=1)
    w_rh = w_r.astype(BF16)
    w_rl = (w_r - w_rh.astype(F32)).astype(BF16)
    w_r2 = jnp.concatenate([w_rh, w_rl], axis=1)
    b_r = jnp.concatenate(
        [b_coarse[l], jnp.zeros((FINE0 - N_GROUPS,), F32), b_fine[l].reshape(-1),
         jnp.zeros((LANES - FINE0 - N_EXPERTS,), F32)]).reshape(1, LANES)
    bias = _bias_table(sinks[l])
    w_dw_rep = jnp.repeat(w_dw[l], SUBLANES, axis=0)

    def mixer_weights(bias_tbl):
        return (row(g_mix), w_in_p, bd, gain_qk, bias_tbl, w_dw_rep, row(b_dw),
                row(g_cn), row(b_cn), w_pw[l].astype(BF16), row(b_pw), g_oa_p, row(g_oc),
                w_out_p, row(g_ffn), w_r2, w_rh, b_r)

    x1p, hp, recp, rectp, cntp, nkp, nvp, ncp = _prompt_mixer(x_prompt, mixer_weights(bias), tt)
    nb = next(n for n in (8, 4, 2, 1) if DB % n == 0)
    x1s, hs, recs, rects, cnts, nks, nvs, ncs = _sample_mixer(
        x_sample, cache_k[l].reshape(DB, WINDOW, KV_W), cache_v[l].reshape(DB, WINDOW, KV_W), state_conv[l],
        mixer_weights(bias[2:3]), nb)

    recp, recs = recp.reshape(Np, REC_W), recs.reshape(Ns, REC_W)
    rectp = jnp.transpose(rectp, (1, 0, 2)).reshape(REC_W, Np)
    rects = jnp.transpose(rects, (1, 0, 2)).reshape(REC_W, Ns)
    cnt_p = cntp[:, 0].astype(I32)
    cnt = cnt_p + cnts[:, 0].astype(I32)
    n_blocks = ((Np + Ns) * TOP_K + N_EXPERTS * (blk - 1)) // blk
    padded = (cnt + blk - 1) // blk * blk
    pend = jnp.cumsum(padded)
    pstart = pend - padded
    dest_p = _slots(rectp, pstart, jnp.zeros_like(cnt_p))
    dest_s = _slots(rects, pstart, cnt_p)
    block_e = jnp.minimum(
        jnp.sum((pend[None, :] <= (jnp.arange(n_blocks, dtype=I32) * blk)[:, None]).astype(I32), axis=1),
        N_EXPERTS - 1).astype(I32)
    n_used = (pend[-1:] // blk).astype(I32)
    pad_start = pstart + cnt
    pad_head = pad_start % 2
    meta = jnp.concatenate([pad_start, pad_head, padded - cnt - pad_head, n_used]).astype(I32)

    xn = D // 2 // LANES
    yn = D // 2 // LANES
    xs = _dispatch(hp.reshape(Np * xn, LANES), hs.reshape(Ns * xn, LANES),
                   jnp.concatenate([dest_p, dest_s], axis=1) * xn, meta, n_blocks, blk, td)
    ys = _experts(xs, block_e, n_used, w_e_gate[l], w_e_up[l], w_e_down[l], blk)

    gp = row(g_ple)
    w_gate = w_ple_gate[l].astype(BF16)
    w_proj = w_ple_proj[l].astype(BF16)
    yp = _final(x1p.reshape(Np, D), recp, p_prompt[l].reshape(Np, -1), dest_p * yn, ys, gp, w_gate, w_proj, tf)
    ysm = _final(x1s.reshape(Ns, D), recs, p_sample[l].reshape(Ns, -1), dest_s * yn, ys, gp, w_gate, w_proj, tf)

    kv5 = lambda a, nbat: a.reshape(1, nbat, WINDOW, N_KV_HEADS, HEAD_DIM)
    return (yp.reshape(B, T, D), ysm.reshape(DB, DS, D),
            kv5(nkp, B), kv5(nvp, B), ncp[None],
            kv5(nks, DB), kv5(nvs, DB), ncs[None])
```

```python
import functools

import jax
import jax.numpy as jnp
from jax import lax
from jax.experimental import pallas as pl
from jax.experimental.pallas import tpu as pltpu

CHUNK = 64
HEAD_DIM = 64
N_HEADS = 8
N_KV_HEADS = 2
GQ = N_HEADS // N_KV_HEADS
Q_W = N_HEADS * HEAD_DIM
KV_W = N_KV_HEADS * HEAD_DIM
WINDOW = 128
BAND = WINDOW + CHUNK
BAND_PAD = 256
CONV_WIDTH = 31
CONV_HIST = CONV_WIDTH - 1
CONV_PAD = 32
CONV_BLOCK = 32
SUBLANES = 8
LANES = 128
N_GROUPS = 4
EXPERTS_PER_GROUP = 8
N_EXPERTS = N_GROUPS * EXPERTS_PER_GROUP
TOP_K = 2
EXPERT_BLOCK = 1024
EPS = 1e-6
REC_W = 8
FINE0 = 8
MASKED = 1e30
DMA_UNROLL = 8

F32 = jnp.float32
BF16 = jnp.bfloat16
U32 = jnp.uint32
I32 = jnp.int32

VMEM_LIMIT = 56 * 1024 * 1024


def _tiles(n_prompt_seq, n_prompt, n_sample):
    tt = next((t for t in (512, 256) if n_prompt_seq % t == 0), n_prompt_seq)
    td = next(t for t in (2048, 1024, 512, 256, 128, 64) if n_prompt % t == 0 and n_sample % t == 0)
    tf = next(t for t in (512, 256, 128, 64, 32) if n_prompt % (2 * t) == 0 and n_sample % (2 * t) == 0)
    return tt, td, tf


def _rms(xf, g):
    return xf * lax.rsqrt(jnp.mean(xf * xf, axis=-1, keepdims=True) + EPS) * g


def _dot(a, b):
    return jnp.dot(a, b, preferred_element_type=F32)


def _dot_nt(a, b):
    return lax.dot_general(a, b, (((1,), (1,)), ((), ())), preferred_element_type=F32)


def _store_tile_rows(ref, lead, value):
    rows, width = value.shape
    n = width // LANES
    for c in range(n):
        ref[lead + (pl.ds(c, rows, stride=n), slice(None))] = value[:, c * LANES:(c + 1) * LANES]


def _load_tile_rows(ref, lead, rows, n):
    return jnp.concatenate([ref[lead + (pl.ds(c, rows, stride=n), slice(None))] for c in range(n)], axis=1)


def _project(x, g_mix, w_in_ref, bd_ref, gain_qk):
    xn = _rms(x, g_mix).astype(BF16)
    z = _dot(xn, w_in_ref[...])
    qk = z[:, 0:Q_W + KV_W]
    sq = (qk * qk).astype(BF16)
    bd = bd_ref[...]
    two = 2 * LANES
    ms = jnp.concatenate(
        [_dot(sq[:, 0:two], bd), _dot(sq[:, two:2 * two], bd), _dot(sq[:, 2 * two:], bd[0:KV_W, 0:KV_W])], axis=1)
    qkn = qk * lax.rsqrt(ms + EPS) * gain_qk
    v = z[:, Q_W + KV_W:Q_W + 2 * KV_W]
    c0 = Q_W + 2 * KV_W
    c_conv = (z.shape[1] - c0) // 2
    u = z[:, c0:c0 + c_conv] * jax.nn.sigmoid(z[:, c0 + c_conv:])
    return qkn, v, u


def _attend_chunk(q_groups, k_band, v_band, bias):
    lane = lax.broadcasted_iota(I32, (CHUNK, LANES), 1)
    low = lane < HEAD_DIM
    zero = jnp.zeros((CHUNK, LANES), F32)
    q_all = jnp.concatenate(
        [jnp.where(low, qg, zero) for qg in q_groups] + [jnp.where(low, zero, qg) for qg in q_groups],
        axis=0).astype(BF16)
    pad = jnp.zeros((BAND_PAD - BAND, LANES), BF16)
    s = _dot_nt(q_all, jnp.concatenate([k_band, pad], axis=0)) - bias
    m = jnp.max(s, axis=-1, keepdims=True)
    e = jnp.exp(s - m)
    denom = jnp.sum(e, axis=-1, keepdims=True)
    o = _dot(e.astype(BF16), jnp.concatenate([v_band, pad], axis=0)) * (1.0 / denom)
    half = GQ * CHUNK
    return [jnp.where(low, o[g * CHUNK:(g + 1) * CHUNK, :], o[half + g * CHUNK:half + (g + 1) * CHUNK, :])
            for g in range(GQ)]


def _shift_copies(ush, rows):
    for n in range(1, SUBLANES):
        ush[n, 0:rows, :] = ush[0, n:n + rows, :]


def _conv_rows(ush, r0, rows, w_dw_ref, b_dw):
    ch = b_dw.shape[-1]
    groups = rows // SUBLANES
    acc = jnp.broadcast_to(b_dw, (groups, SUBLANES, ch))
    for j in range(CONV_WIDTH):
        off = CONV_PAD - CONV_HIST + j
        taps = ush[off % SUBLANES, pl.ds(r0 + off - off % SUBLANES, rows), :].reshape(groups, SUBLANES, ch)
        acc = acc + w_dw_ref[j * SUBLANES:(j + 1) * SUBLANES, :][None] * taps
    return acc.reshape(rows, ch)


def _finish(x, oa, conv, refs):
    (g_cn, b_cn, w_pw, b_pw, g_oa, g_oc, w_out, g_ffn, w_r2, w_rh, b_r) = refs
    mu = jnp.mean(conv, axis=-1, keepdims=True)
    cen = conv - mu
    var = jnp.mean(cen * cen, axis=-1, keepdims=True)
    ln = cen * lax.rsqrt(var + EPS) * g_cn[...] + b_cn[...]
    act = (ln * jax.nn.sigmoid(ln)).astype(BF16)
    c = _dot(act, w_pw[...]) + b_pw[...]
    half = oa.shape[-1]
    mixed = (_dot(_rms(oa, g_oa[...]).astype(BF16), w_out[0:half, :])
             + _dot(_rms(c, g_oc[...]).astype(BF16), w_out[half:, :]))
    x1 = x + mixed
    h = _rms(x1, g_ffn[...])
    h_hi = h.astype(BF16)
    h_lo = (h - h_hi.astype(F32)).astype(BF16)
    a = _dot(h_hi, w_r2[...])
    logits = a[:, 0:LANES] + a[:, LANES:] + _dot(h_lo, w_rh[...]) + b_r[...]
    return x1, h, logits


def _pack_rows(h):
    half = h.shape[-1] // 2
    hb = h.astype(BF16).astype(F32)
    lo = lax.bitcast_convert_type(hb[:, 0:half], U32)
    hi = lax.bitcast_convert_type(hb[:, half:], U32)
    return (lo >> 16) | (hi & jnp.uint32(0xFFFF0000))


def _unpack_rows(w, dtype):
    lo = lax.bitcast_convert_type(w << 16, F32).astype(dtype)
    hi = lax.bitcast_convert_type(w & jnp.uint32(0xFFFF0000), F32).astype(dtype)
    return lo, hi


def _route(logits, cnt_ref):
    rows = logits.shape[0]
    lt = jnp.transpose(logits)
    ninf = -jnp.inf
    crow = lax.broadcasted_iota(I32, (SUBLANES, rows), 0).astype(F32)
    lc = jnp.where(crow < N_GROUPS, lt[0:SUBLANES, :], ninf)
    mc = jnp.max(lc, axis=0, keepdims=True)
    grp = jnp.min(jnp.where(lc == mc, crow, float(SUBLANES)), axis=0, keepdims=True)
    g1 = 1.0 / jnp.sum(jnp.exp(lc - mc), axis=0, keepdims=True)
    erow = lax.broadcasted_iota(I32, (N_EXPERTS, rows), 0).astype(F32)
    lo = grp * EXPERTS_PER_GROUP
    lf = jnp.where((erow >= lo) & (erow < lo + EXPERTS_PER_GROUP), lt[FINE0:FINE0 + N_EXPERTS, :], ninf)
    t1 = jnp.max(lf, axis=0, keepdims=True)
    e1 = jnp.min(jnp.where(lf == t1, erow, float(N_EXPERTS)), axis=0, keepdims=True)
    lf2 = jnp.where(erow == e1, ninf, lf)
    t2 = jnp.max(lf2, axis=0, keepdims=True)
    e2 = jnp.min(jnp.where(lf2 == t2, erow, float(N_EXPERTS)), axis=0, keepdims=True)
    e2x = jnp.exp(t2 - t1)
    inv = 1.0 / (1.0 + e2x)
    w1 = g1 * inv
    w2 = g1 * (e2x * inv)
    oh1 = (erow == e1).astype(F32)
    oh2 = (erow == e2).astype(F32)
    oh = oh1 + oh2
    ri = lax.broadcasted_iota(I32, (rows, rows), 0)
    ci = lax.broadcasted_iota(I32, (rows, rows), 1)
    before = (ri < ci).astype(BF16)
    cnt = cnt_ref[...]
    tot = _dot(oh.astype(BF16), before) + cnt[:, 0:1]
    r1 = jnp.sum(oh1 * tot, axis=0, keepdims=True)
    r2 = jnp.sum(oh2 * tot, axis=0, keepdims=True)
    cnt_ref[...] = cnt + jnp.sum(oh, axis=1, keepdims=True)
    row = lax.broadcasted_iota(I32, (REC_W, rows), 0)
    rec_t = jnp.where(row == 0, e1, 0.0)
    rec_t = jnp.where(row == 1, e2, rec_t)
    rec_t = jnp.where(row == 2, w1, rec_t)
    rec_t = jnp.where(row == 3, w2, rec_t)
    rec_t = jnp.where(row == 4, r1, rec_t)
    rec_t = jnp.where(row == 5, r2, rec_t)
    full = jnp.concatenate([rec_t, jnp.zeros((LANES - REC_W, rows), F32)], axis=0)
    return jnp.transpose(full)[:, 0:REC_W], rec_t


def _prompt_mixer_kernel(x_ref, g_mix, w_in, bd, gain_qk, bias_ref, w_dw, b_dw,
                         g_cn, b_cn, w_pw, b_pw, g_oa, g_oc, w_out, g_ffn, w_r2, w_rh, b_r,
                         x1_ref, h_ref, rec_ref, rect_ref, cnt_out, nk_ref, nv_ref, nc_ref,
                         kx, vx, ush, oa_s, conv_s, cnt_s, *, tt):
    b = pl.program_id(0)
    t = pl.program_id(1)
    n_chunks = tt // CHUNK

    @pl.when((b == 0) & (t == 0))
    def _():
        cnt_s[...] = jnp.zeros_like(cnt_s)

    @pl.when(t == 0)
    def _():
        kx[0:WINDOW, :] = jnp.zeros((WINDOW, LANES), BF16)
        vx[0:WINDOW, :] = jnp.zeros((WINDOW, LANES), BF16)
        ush[0, 0:CONV_PAD, :] = jnp.zeros((CONV_PAD, ush.shape[-1]), F32)

    x = x_ref[0]
    qkn, v, u = _project(x, g_mix[...], w_in, bd, gain_qk[...])
    k = qkn[:, Q_W:]
    kx[WINDOW:WINDOW + tt, :] = k.astype(BF16)
    vx[WINDOW:WINDOW + tt, :] = v.astype(BF16)
    ush[0, CONV_PAD:CONV_PAD + tt, :] = u
    _shift_copies(ush, tt + CONV_PAD - SUBLANES)

    for c in range(n_chunks):
        variant = jnp.minimum(t * n_chunks + c, WINDOW // CHUNK)
        rows = slice(c * CHUNK, (c + 1) * CHUNK)
        q_groups = [qkn[rows, m * LANES:(m + 1) * LANES] for m in range(GQ)]
        o_groups = _attend_chunk(q_groups, kx[c * CHUNK:c * CHUNK + BAND, :], vx[c * CHUNK:c * CHUNK + BAND, :],
                                 bias_ref[variant])
        for m in range(GQ):
            oa_s[rows, m * LANES:(m + 1) * LANES] = o_groups[m]

    for rb in range(tt // CONV_BLOCK):
        conv_s[rb * CONV_BLOCK:(rb + 1) * CONV_BLOCK, :] = _conv_rows(ush, rb * CONV_BLOCK, CONV_BLOCK, w_dw, b_dw[...])

    x1, hh, logits = _finish(x, oa_s[...], conv_s[...],
                             (g_cn, b_cn, w_pw, b_pw, g_oa, g_oc, w_out, g_ffn, w_r2, w_rh, b_r))
    x1_ref[0] = x1
    _store_tile_rows(h_ref, (0,), _pack_rows(hh))
    rec_ref[0], rect_ref[0] = _route(logits, cnt_s)
    cnt_out[...] = cnt_s[...]

    nk_ref[0] = k[tt - WINDOW:, :]
    nv_ref[0] = v[tt - WINDOW:, :]
    nc_ref[0] = u[tt - CONV_HIST:, :]

    kx[0:WINDOW, :] = kx[tt:tt + WINDOW, :]
    vx[0:WINDOW, :] = vx[tt:tt + WINDOW, :]
    ush[0, 0:CONV_PAD, :] = ush[0, tt:tt + CONV_PAD, :]


def _sample_mixer_kernel(x_ref, ck_ref, cv_ref, sc_ref, g_mix, w_in, bd, gain_qk, bias_ref, w_dw, b_dw,
                         g_cn, b_cn, w_pw, b_pw, g_oa, g_oc, w_out, g_ffn, w_r2, w_rh, b_r,
                         x1_ref, h_ref, rec_ref, rect_ref, cnt_out, nk_ref, nv_ref, nc_ref,
                         ush, oa_s, conv_s, cnt_s, *, nb):
    i = pl.program_id(0)

    @pl.when(i == 0)
    def _():
        cnt_s[...] = jnp.zeros_like(cnt_s)

    rows_all = nb * CHUNK
    x = x_ref[...].reshape(rows_all, x_ref.shape[-1])
    qkn, v, u = _project(x, g_mix[...], w_in, bd, gain_qk[...])
    k = qkn[:, Q_W:]
    for j in range(nb):
        rows = slice(j * CHUNK, (j + 1) * CHUNK)
        ck = ck_ref[j]
        cv = cv_ref[j]
        k_band = jnp.concatenate([ck, k[rows, :]], axis=0)
        v_band = jnp.concatenate([cv, v[rows, :]], axis=0)
        q_groups = [qkn[rows, m * LANES:(m + 1) * LANES] for m in range(GQ)]
        o_groups = _attend_chunk(q_groups, k_band.astype(BF16), v_band.astype(BF16), bias_ref[0])
        for m in range(GQ):
            oa_s[rows, m * LANES:(m + 1) * LANES] = o_groups[m]
        ush[0, 0:CONV_PAD, :] = jnp.zeros((CONV_PAD, ush.shape[-1]), F32)
        ush[0, CONV_PAD - CONV_HIST:CONV_PAD, :] = sc_ref[j]
        ush[0, CONV_PAD:CONV_PAD + CHUNK, :] = u[rows, :]
        _shift_copies(ush, CHUNK + CONV_PAD - SUBLANES)
        for rb in range(CHUNK // CONV_BLOCK):
            r0 = rb * CONV_BLOCK
            conv_s[j * CHUNK + r0:j * CHUNK + r0 + CONV_BLOCK, :] = _conv_rows(ush, r0, CONV_BLOCK, w_dw, b_dw[...])
        nk_ref[j] = k_band[CHUNK:, :]
        nv_ref[j] = v_band[CHUNK:, :]
        nc_ref[j] = ush[0, CONV_PAD + CHUNK - CONV_HIST:CONV_PAD + CHUNK, :]

    x1, hh, logits = _finish(x, oa_s[...], conv_s[...],
                             (g_cn, b_cn, w_pw, b_pw, g_oa, g_oc, w_out, g_ffn, w_r2, w_rh, b_r))
    x1_ref[...] = x1.reshape(x1_ref.shape)
    packed = _pack_rows(hh)
    for j in range(nb):
        _store_tile_rows(h_ref, (j,), packed[j * CHUNK:(j + 1) * CHUNK, :])
    rec, rec_t = _route(logits, cnt_s)
    rec_ref[...] = rec.reshape(rec_ref.shape)
    rect_ref[0] = rec_t
    cnt_out[...] = cnt_s[...]


def _full(shape):
    nd = len(shape)
    return pl.BlockSpec(shape, lambda *_: (0,) * nd)


def _prompt_mixer(x, wts, tt):
    B, T, D = x.shape
    c_conv = wts[6].shape[-1]
    xs_rows = D // 2 // LANES
    nt = T // tt
    tok = lambda rows, last: pl.BlockSpec((1, rows, last), lambda b, t: (b, t, 0))
    per_batch = lambda rows, last: pl.BlockSpec((1, rows, last), lambda b, t: (b, 0, 0))
    return pl.pallas_call(
        functools.partial(_prompt_mixer_kernel, tt=tt),
        grid=(B, nt),
        in_specs=[tok(tt, D)] + [_full(w.shape) for w in wts],
        out_specs=[tok(tt, D), tok(tt * xs_rows, LANES), tok(tt, REC_W),
                   pl.BlockSpec((1, REC_W, tt), lambda b, t: (b, 0, t)), pl.BlockSpec((N_EXPERTS, LANES), lambda b, t: (0, 0)),
                   per_batch(WINDOW, KV_W), per_batch(WINDOW, KV_W), per_batch(CONV_HIST, c_conv)],
        out_shape=[
            jax.ShapeDtypeStruct((B, T, D), F32),
            jax.ShapeDtypeStruct((B, T * xs_rows, LANES), U32),
            jax.ShapeDtypeStruct((B, T, REC_W), F32),
            jax.ShapeDtypeStruct((B, REC_W, T), F32),
            jax.ShapeDtypeStruct((N_EXPERTS, LANES), F32),
            jax.ShapeDtypeStruct((B, WINDOW, KV_W), F32),
            jax.ShapeDtypeStruct((B, WINDOW, KV_W), F32),
            jax.ShapeDtypeStruct((B, CONV_HIST, c_conv), F32),
        ],
        scratch_shapes=[
            pltpu.VMEM((WINDOW + tt, LANES), BF16),
            pltpu.VMEM((WINDOW + tt, LANES), BF16),
            pltpu.VMEM((SUBLANES, CONV_PAD + tt, c_conv), F32),
            pltpu.VMEM((tt, Q_W), F32),
            pltpu.VMEM((tt, c_conv), F32),
            pltpu.VMEM((N_EXPERTS, LANES), F32),
        ],
        compiler_params=pltpu.CompilerParams(
            dimension_semantics=("arbitrary", "arbitrary"), vmem_limit_bytes=VMEM_LIMIT),
        name="prompt_mixer",
    )(x, *wts)


def _sample_mixer(x, ck, cv, sc, wts, nb):
    B, T, D = x.shape
    assert T == CHUNK and B % nb == 0
    c_conv = wts[6].shape[-1]
    xs_rows = D // 2 // LANES
    blk3 = lambda rows, last: pl.BlockSpec((nb, rows, last), lambda i: (i, 0, 0))
    return pl.pallas_call(
        functools.partial(_sample_mixer_kernel, nb=nb),
        grid=(B // nb,),
        in_specs=[blk3(T, D), blk3(WINDOW, KV_W), blk3(WINDOW, KV_W), blk3(CONV_HIST, c_conv)]
                 + [_full(w.shape) for w in wts],
        out_specs=[blk3(T, D), blk3(T * xs_rows, LANES), blk3(T, REC_W),
                   pl.BlockSpec((1, REC_W, nb * T), lambda i: (i, 0, 0)), pl.BlockSpec((N_EXPERTS, LANES), lambda i: (0, 0)),
                   blk3(WINDOW, KV_W), blk3(WINDOW, KV_W), blk3(CONV_HIST, c_conv)],
        out_shape=[
            jax.ShapeDtypeStruct((B, T, D), F32),
            jax.ShapeDtypeStruct((B, T * xs_rows, LANES), U32),
            jax.ShapeDtypeStruct((B, T, REC_W), F32),
            jax.ShapeDtypeStruct((B // nb, REC_W, nb * T), F32),
            jax.ShapeDtypeStruct((N_EXPERTS, LANES), F32),
            jax.ShapeDtypeStruct((B, WINDOW, KV_W), F32),
            jax.ShapeDtypeStruct((B, WINDOW, KV_W), F32),
            jax.ShapeDtypeStruct((B, CONV_HIST, c_conv), F32),
        ],
        scratch_shapes=[
            pltpu.VMEM((SUBLANES, CONV_PAD + CHUNK, c_conv), F32),
            pltpu.VMEM((nb * CHUNK, Q_W), F32),
            pltpu.VMEM((nb * CHUNK, c_conv), F32),
            pltpu.VMEM((N_EXPERTS, LANES), F32),
        ],
        compiler_params=pltpu.CompilerParams(
            dimension_semantics=("arbitrary",), vmem_limit_bytes=VMEM_LIMIT),
        name="sample_mixer",
    )(x, ck, cv, sc, *wts)


def _dispatch_kernel(meta_ref, dest_ref, hp_ref, hs_ref, xs_ref, zbuf, sem, zsem, *, td, n, blk, n_blocks, ntp):
    i = pl.program_id(0)

    def issue_from(h_ref):
        def body(r8, carry):
            for uu in range(DMA_UNROLL):
                r = r8 * DMA_UNROLL + uu
                src = h_ref.at[pl.ds(pl.multiple_of(r * n, n), n)]
                for kk in range(TOP_K):
                    d = pl.multiple_of(dest_ref[kk, 0, 0, r], n)
                    pltpu.make_async_copy(src, xs_ref.at[pl.ds(d, n)], sem).start(priority=kk % 2)
            return carry
        lax.fori_loop(0, td // DMA_UNROLL, body, 0)

    @pl.when(i < ntp)
    def _():
        issue_from(hp_ref)

    @pl.when(i >= ntp)
    def _():
        issue_from(hs_ref)

    def pad_pass(act):
        def pad_expert(e, carry):
            start = meta_ref[e]
            head = meta_ref[N_EXPERTS + e]
            body = meta_ref[2 * N_EXPERTS + e]

            @pl.when(head == 1)
            def _():
                act(pltpu.make_async_copy(zbuf.at[pl.ds(0, n)], xs_ref.at[pl.ds(pl.multiple_of(start * n, n), n)], zsem))
            bit = blk // 2
            while bit >= 2:
                off = pl.multiple_of((start + head + (body // (2 * bit)) * (2 * bit)) * n, 2 * n)

                @pl.when((body // bit) % 2 == 1)
                def _(bit=bit, off=off):
                    act(pltpu.make_async_copy(zbuf.at[pl.ds(0, bit * n)], xs_ref.at[pl.ds(off, bit * n)], zsem))
                bit //= 2
            return carry

        lax.fori_loop(0, N_EXPERTS, pad_expert, 0)

        def pad_block(j, carry):
            act(pltpu.make_async_copy(zbuf, xs_ref.at[pl.ds(pl.multiple_of(j * (blk * n), blk * n), blk * n)], zsem))
            return carry

        lax.fori_loop(meta_ref[3 * N_EXPERTS], n_blocks, pad_block, 0)

    @pl.when(i == pl.num_programs(0) - 1)
    def _():
        zbuf[...] = jnp.zeros_like(zbuf)
        pad_pass(lambda cp: cp.start())
        pad_pass(lambda cp: cp.wait())

    for _ in range(TOP_K):
        pltpu.make_async_copy(hp_ref, xs_ref.at[pl.ds(0, td * n)], sem).wait()


def _dispatch(hp, hs, dest_rows, meta, n_blocks, blk, td):
    rows_p, rows_s = hp.shape[0], hs.shape[0]
    n_tok = dest_rows.shape[1]
    n = (rows_p + rows_s) // n_tok
    ntp, nts = rows_p // (td * n), rows_s // (td * n)
    dest3 = dest_rows.reshape(TOP_K, ntp + nts, 1, td)
    grid_spec = pltpu.PrefetchScalarGridSpec(
        num_scalar_prefetch=1,
        grid=(ntp + nts,),
        in_specs=[
            pl.BlockSpec((TOP_K, 1, 1, td), lambda i, m: (0, i, 0, 0), memory_space=pltpu.SMEM),
            pl.BlockSpec((td * n, LANES), lambda i, m: (jnp.minimum(i, ntp - 1), 0)),
            pl.BlockSpec((td * n, LANES), lambda i, m: (jnp.maximum(i - ntp, 0), 0)),
        ],
        out_specs=pl.BlockSpec(memory_space=pl.ANY),
        scratch_shapes=[pltpu.VMEM((blk * n, LANES), U32), pltpu.SemaphoreType.DMA(()), pltpu.SemaphoreType.DMA(())],
    )
    return pl.pallas_call(
        functools.partial(_dispatch_kernel, td=td, n=n, blk=blk, n_blocks=n_blocks, ntp=ntp),
        grid_spec=grid_spec,
        out_shape=jax.ShapeDtypeStruct((n_blocks * blk * n, LANES), U32),
        compiler_params=pltpu.CompilerParams(
            dimension_semantics=("arbitrary",), vmem_limit_bytes=VMEM_LIMIT),
        name="moe_dispatch",
    )(meta, dest3, hp, hs)


def _expert_kernel(block_e_ref, n_used_ref, x_ref, wg_ref, wu_ref, wd_ref, y_ref, wgb, wub, wdb, *, blk):
    i = pl.program_id(0)
    used = i < n_used_ref[0]
    new_expert = (i == 0) | (block_e_ref[i] != block_e_ref[jnp.maximum(i - 1, 0)])

    @pl.when(used & new_expert)
    def _():
        wgb[...] = wg_ref[0].astype(BF16)
        wub[...] = wu_ref[0].astype(BF16)
        wdb[...] = wd_ref[0].astype(BF16)

    @pl.when(used)
    def _():
        lo, hi = _unpack_rows(_load_tile_rows(x_ref, (), blk, x_ref.shape[0] // blk), BF16)
        half = lo.shape[-1]
        g = _dot(lo, wgb[0:half, :]) + _dot(hi, wgb[half:, :])
        u = _dot(lo, wub[0:half, :]) + _dot(hi, wub[half:, :])
        a = (g * jax.nn.sigmoid(g) * u).astype(BF16)
        _store_tile_rows(y_ref, (), _pack_rows(_dot(a, wdb[...])))

    @pl.when(jnp.logical_not(used))
    def _():
        y_ref[...] = jnp.zeros_like(y_ref)


def _experts(xs, block_e, n_used, wg, wu, wd, blk):
    n_blocks = block_e.shape[0]
    _, D, de = wg.shape
    xn = xs.shape[0] // (n_blocks * blk)
    yn = D // 2 // LANES
    grid_spec = pltpu.PrefetchScalarGridSpec(
        num_scalar_prefetch=2,
        grid=(n_blocks,),
        in_specs=[
            pl.BlockSpec((blk * xn, LANES), lambda i, be, nu: (i, 0)),
            pl.BlockSpec((1, D, de), lambda i, be, nu: (be[i], 0, 0)),
            pl.BlockSpec((1, D, de), lambda i, be, nu: (be[i], 0, 0)),
            pl.BlockSpec((1, de, D), lambda i, be, nu: (be[i], 0, 0)),
        ],
        out_specs=pl.BlockSpec((blk * yn, LANES), lambda i, be, nu: (i, 0)),
        scratch_shapes=[pltpu.VMEM((D, de), BF16), pltpu.VMEM((D, de), BF16), pltpu.VMEM((de, D), BF16)],
    )
    return pl.pallas_call(
        functools.partial(_expert_kernel, blk=blk),
        grid_spec=grid_spec,
        out_shape=jax.ShapeDtypeStruct((n_blocks * blk * yn, LANES), U32),
        compiler_params=pltpu.CompilerParams(
            dimension_semantics=("arbitrary",), vmem_limit_bytes=VMEM_LIMIT),
        name="moe_experts",
    )(block_e, n_used, xs, wg, wu, wd)


def _final_kernel(dest_cur, dest_nxt, x1_ref, rec_ref, p_ref, g_ple, w_gate, w_proj, ys_ref, y_ref,
                  buf_a, buf_b, sem, *, tf, n):
    i = pl.program_id(0)
    bufs = (buf_a, buf_b)

    def issue(dref, half, s):
        for r in range(tf):
            for kk in range(TOP_K):
                d = pl.multiple_of(dref[kk, 0, 0, half * tf + r], n)
                pltpu.make_async_copy(ys_ref.at[pl.ds(d, n)], bufs[s].at[kk, pl.ds(r * n, n)],
                                      sem.at[s]).start(priority=kk % 2)

    def drain(s):
        for kk in range(TOP_K):
            pltpu.make_async_copy(ys_ref.at[pl.ds(0, tf * n)], bufs[s].at[kk], sem.at[s]).wait()

    def compute(half, s):
        rows = slice(half * tf, (half + 1) * tf)
        rec = rec_ref[rows, :]
        y1 = jnp.concatenate(_unpack_rows(_load_tile_rows(bufs[s], (0,), tf, n), F32), axis=1)
        y2 = jnp.concatenate(_unpack_rows(_load_tile_rows(bufs[s], (1,), tf, n), F32), axis=1)
        x2 = x1_ref[rows, :] + (rec[:, 2:3] * y1 + rec[:, 3:4] * y2)
        gate = jax.nn.sigmoid(_dot(_rms(x2, g_ple[...]).astype(BF16), w_gate[...]))
        y_ref[rows, :] = x2 + gate * _dot(p_ref[rows, :].astype(BF16), w_proj[...])

    @pl.when(i == 0)
    def _():
        issue(dest_cur, 0, 0)

    drain(0)
    issue(dest_cur, 1, 1)
    compute(0, 0)
    drain(1)
    issue(dest_nxt, 0, 0)
    compute(1, 1)

    @pl.when(i == pl.num_programs(0) - 1)
    def _():
        drain(0)


def _final(x1, rec, p, dest_rows, ys, g_ple, w_gate, w_proj, tf):
    N, D = x1.shape
    n = D // 2 // LANES
    pair = 2 * tf
    assert N % pair == 0
    nt = N // pair
    dest3 = dest_rows.reshape(TOP_K, nt, 1, pair)
    smem_blk = lambda fn: pl.BlockSpec((TOP_K, 1, 1, pair), fn, memory_space=pltpu.SMEM)
    return pl.pallas_call(
        functools.partial(_final_kernel, tf=tf, n=n),
        grid=(nt,),
        in_specs=[
            smem_blk(lambda i: (0, i, 0, 0)),
            smem_blk(lambda i: (0, jnp.minimum(i + 1, nt - 1), 0, 0)),
            pl.BlockSpec((pair, D), lambda i: (i, 0)),
            pl.BlockSpec((pair, REC_W), lambda i: (i, 0)),
            pl.BlockSpec((pair, p.shape[-1]), lambda i: (i, 0)),
            _full(g_ple.shape), _full(w_gate.shape), _full(w_proj.shape),
            pl.BlockSpec(memory_space=pl.ANY),
        ],
        out_specs=pl.BlockSpec((pair, D), lambda i: (i, 0)),
        out_shape=jax.ShapeDtypeStruct((N, D), F32),
        scratch_shapes=[pltpu.VMEM((TOP_K, tf * n, LANES), U32), pltpu.VMEM((TOP_K, tf * n, LANES), U32),
                        pltpu.SemaphoreType.DMA((2,))],
        compiler_params=pltpu.CompilerParams(
            dimension_semantics=("arbitrary",), vmem_limit_bytes=VMEM_LIMIT),
        name="moe_combine_ple",
    )(dest3, dest3, x1, rec, p, g_ple, w_gate, w_proj, ys)


def _slots(rec, pstart, offset):
    e = rec[0:TOP_K, :].astype(I32)
    rank = rec[4:4 + TOP_K, :].astype(I32)
    onehot = e[None, :, :] == jnp.arange(N_EXPERTS, dtype=I32)[:, None, None]
    return rank + jnp.sum(jnp.where(onehot, (pstart + offset)[:, None, None], 0), axis=0)


def _bias_table(sinks):
    slopes = jnp.array([2.0 ** (-8.0 * (i + 1) / N_HEADS) for i in range(N_HEADS)], F32)
    qpos = WINDOW + jnp.arange(CHUNK, dtype=I32)
    col = jnp.arange(BAND_PAD, dtype=I32)
    dist = jnp.abs(qpos[:, None] - col[None, :]).astype(F32)
    core = slopes[:, None, None] * dist[None]
    sink = jnp.broadcast_to(-sinks.astype(F32)[:, None, None], core.shape)
    table = jnp.where(col == BAND, sink, jnp.where(col < BAND, core, MASKED))
    first_valid = jnp.array([WINDOW, WINDOW - CHUNK, 0], I32)
    table = jnp.where(col[None, None, None, :] < first_valid[:, None, None, None], MASKED, table[None])
    return table.reshape(3, N_HEADS * CHUNK, BAND_PAD)


def kernel(x_prompt, x_sample, p_prompt, p_sample, cache_k, cache_v, state_conv, g_mix, w_in, g_q, g_k, sinks,
           w_dw, b_dw, g_cn, b_cn, w_pw, b_pw, g_oa, g_oc, w_out, g_ffn, w_coarse, b_coarse, w_fine, b_fine,
           w_e_gate, w_e_up, w_e_down, g_ple, w_ple_gate, w_ple_proj):
    assert g_mix.shape[0] == 1
    l = 0
    B, T, D = x_prompt.shape
    DB, DS, _ = x_sample.shape
    assert cache_k.shape[2] == WINDOW and DS == CHUNK
    Np, Ns = B * T, DB * DS
    tt, td, tf = _tiles(T, Np, Ns)
    blk = EXPERT_BLOCK

    row = lambda a: a[l].reshape(1, -1)
    perm = jnp.concatenate([jnp.concatenate([jnp.arange(HEAD_DIM) + m * HEAD_DIM,
                                             jnp.arange(HEAD_DIM) + (GQ + m) * HEAD_DIM]) for m in range(GQ)])
    w_in_p = jnp.concatenate([w_in[l][:, perm], w_in[l][:, Q_W:]], axis=1).astype(BF16)
    w_out_p = jnp.concatenate([w_out[l][perm, :], w_out[l][Q_W:, :]], axis=0).astype(BF16)
    g_oa_p = g_oa[l][perm].reshape(1, -1)
    gain_qk = jnp.concatenate([jnp.tile(g_q[l] * (HEAD_DIM ** -0.5), N_HEADS), jnp.tile(g_k[l], N_KV_HEADS)]).reshape(1, -1)
    blk_id = jnp.arange(2 * LANES) // HEAD_DIM
    bd = jnp.where(blk_id[:, None] == blk_id[None, :], 1.0 / HEAD_DIM, 0.0).astype(BF16)
    w_r = jnp.concatenate(
        [w_coarse[l], jnp.zeros((D, FINE0 - N_GROUPS), F32),
         jnp.transpose(w_fine[l], (1, 0, 2)).reshape(D, N_EXPERTS),
         jnp.zeros((D, LANES - FINE0 - N_EXPERTS), F32)], axis=1)
    w_rh = w_r.astype(BF16)
    w_rl = (w_r - w_rh.astype(F32)).astype(BF16)
    w_r2 = jnp.concatenate([w_rh, w_rl], axis=1)
    b_r = jnp.concatenate(
        [b_coarse[l], jnp.zeros((FINE0 - N_GROUPS,), F32), b_fine[l].reshape(-1),
         jnp.zeros((LANES - FINE0 - N_EXPERTS,), F32)]).reshape(1, LANES)
    bias = _bias_table(sinks[l])
    w_dw_rep = jnp.repeat(w_dw[l], SUBLANES, axis=0)

    def mixer_weights(bias_tbl):
        return (row(g_mix), w_in_p, bd, gain_qk, bias_tbl, w_dw_rep, row(b_dw),
                row(g_cn), row(b_cn), w_pw[l].astype(BF16), row(b_pw), g_oa_p, row(g_oc),
                w_out_p, row(g_ffn), w_r2, w_rh, b_r)

    x1p, hp, recp, rectp, cntp, nkp, nvp, ncp = _prompt_mixer(x_prompt, mixer_weights(bias), tt)
    nb = next(n for n in (8, 4, 2, 1) if DB % n == 0)
    x1s, hs, recs, rects, cnts, nks, nvs, ncs = _sample_mixer(
        x_sample, cache_k[l].reshape(DB, WINDOW, KV_W), cache_v[l].reshape(DB, WINDOW, KV_W), state_conv[l],
        mixer_weights(bias[2:3]), nb)

    recp, recs = recp.reshape(Np, REC_W), recs.reshape(Ns, REC_W)
    rectp = jnp.transpose(rectp, (1, 0, 2)).reshape(REC_W, Np)
    rects = jnp.transpose(rects, (1, 0, 2)).reshape(REC_W, Ns)
    cnt_p = cntp[:, 0].astype(I32)
    cnt = cnt_p + cnts[:, 0].astype(I32)
    n_blocks = ((Np + Ns) * TOP_K + N_EXPERTS * (blk - 1)) // blk
    padded = (cnt + blk - 1) // blk * blk
    pend = jnp.cumsum(padded)
    pstart = pend - padded
    dest_p = _slots(rectp, pstart, jnp.zeros_like(cnt_p))
    dest_s = _slots(rects, pstart, cnt_p)
    block_e = jnp.minimum(
        jnp.sum((pend[None, :] <= (jnp.arange(n_blocks, dtype=I32) * blk)[:, None]).astype(I32), axis=1),
        N_EXPERTS - 1).astype(I32)
    n_used = (pend[-1:] // blk).astype(I32)
    pad_start = pstart + cnt
    pad_head = pad_start % 2
    meta = jnp.concatenate([pad_start, pad_head, padded - cnt - pad_head, n_used]).astype(I32)

    xn = D // 2 // LANES
    yn = D // 2 // LANES
    xs = _dispatch(hp.reshape(Np * xn, LANES), hs.reshape(Ns * xn, LANES),
                   jnp.concatenate([dest_p, dest_s], axis=1) * xn, meta, n_blocks, blk, td)
    ys = _experts(xs, block_e, n_used, w_e_gate[l], w_e_up[l], w_e_down[l], blk)

    gp = row(g_ple)
    w_gate = w_ple_gate[l].astype(BF16)
    w_proj = w_ple_proj[l].astype(BF16)
    yp = _final(x1p.reshape(Np, D), recp, p_prompt[l].reshape(Np, -1), dest_p * yn, ys, gp, w_gate, w_proj, tf)
    ysm = _final(x1s.reshape(Ns, D), recs, p_sample[l].reshape(Ns, -1), dest_s * yn, ys, gp, w_gate, w_proj, tf)

    kv5 = lambda a, nbat: a.reshape(1, nbat, WINDOW, N_KV_HEADS, HEAD_DIM)
    return (yp.reshape(B, T, D), ysm.reshape(DB, DS, D),
            kv5(nkp, B), kv5(nvp, B), ncp[None],
            kv5(nks, DB), kv5(nvs, DB), ncs[None])
```

```python
import functools

import jax
import jax.numpy as jnp
from jax import lax
from jax.experimental import pallas as pl
from jax.experimental.pallas import tpu as pltpu

CHUNK = 64
HEAD_DIM = 64
N_HEADS = 8
N_KV_HEADS = 2
GQ = N_HEADS // N_KV_HEADS
Q_W = N_HEADS * HEAD_DIM
KV_W = N_KV_HEADS * HEAD_DIM
WINDOW = 128
BAND = WINDOW + CHUNK
BAND_PAD = 256
CONV_WIDTH = 31
CONV_HIST = CONV_WIDTH - 1
CONV_PAD = 32
CONV_BLOCK = 32
SUBLANES = 8
LANES = 128
N_GROUPS = 4
EXPERTS_PER_GROUP = 8
N_EXPERTS = N_GROUPS * EXPERTS_PER_GROUP
TOP_K = 2
EXPERT_BLOCK = 1024
EPS = 1e-6
REC_W = 8
FINE0 = 8
MASKED = 1e30
DMA_UNROLL = 8

F32 = jnp.float32
BF16 = jnp.bfloat16
U32 = jnp.uint32
I32 = jnp.int32

VMEM_LIMIT = 56 * 1024 * 1024


def _tiles(n_prompt_seq, n_prompt, n_sample):
    tt = next((t for t in (512, 256) if n_prompt_seq % t == 0), n_prompt_seq)
    td = next(t for t in (2048, 1024, 512, 256, 128, 64) if n_prompt % t == 0 and n_sample % t == 0)
    tf = next(t for t in (512, 256, 128, 64, 32) if n_prompt % (2 * t) == 0 and n_sample % (2 * t) == 0)
    return tt, td, tf


def _rms(xf, g):
    return xf * lax.rsqrt(jnp.mean(xf * xf, axis=-1, keepdims=True) + EPS) * g


def _dot(a, b):
    return jnp.dot(a, b, preferred_element_type=F32)


def _dot_nt(a, b):
    return lax.dot_general(a, b, (((1,), (1,)), ((), ())), preferred_element_type=F32)


def _store_tile_rows(ref, lead, value, row0=0):
    rows, width = value.shape
    n = width // LANES
    for c in range(n):
        ref[lead + (pl.ds(row0 * n + c, rows, stride=n), slice(None))] = value[:, c * LANES:(c + 1) * LANES]


def _load_tile_rows(ref, lead, rows, n, row0=0):
    return jnp.concatenate([ref[lead + (pl.ds(row0 * n + c, rows, stride=n), slice(None))] for c in range(n)], axis=1)


def _project(x, g_mix, w_in_ref, bd_ref, gain_qk):
    xn = _rms(x, g_mix).astype(BF16)
    z = _dot(xn, w_in_ref[...])
    qk = z[:, 0:Q_W + KV_W]
    sq = (qk * qk).astype(BF16)
    bd = bd_ref[...]
    two = 2 * LANES
    ms = jnp.concatenate(
        [_dot(sq[:, 0:two], bd), _dot(sq[:, two:2 * two], bd), _dot(sq[:, 2 * two:], bd[0:KV_W, 0:KV_W])], axis=1)
    qkn = qk * lax.rsqrt(ms + EPS) * gain_qk
    v = z[:, Q_W + KV_W:Q_W + 2 * KV_W]
    c0 = Q_W + 2 * KV_W
    c_conv = (z.shape[1] - c0) // 2
    u = z[:, c0:c0 + c_conv] * jax.nn.sigmoid(z[:, c0 + c_conv:])
    return qkn, v, u


def _attend_chunk(q_groups, k_band, v_band, bias):
    lane = lax.broadcasted_iota(I32, (CHUNK, LANES), 1)
    low = lane < HEAD_DIM
    zero = jnp.zeros((CHUNK, LANES), F32)
    q_all = jnp.concatenate(
        [jnp.where(low, qg, zero) for qg in q_groups] + [jnp.where(low, zero, qg) for qg in q_groups],
        axis=0).astype(BF16)
    pad = jnp.zeros((BAND_PAD - BAND, LANES), BF16)
    s = _dot_nt(q_all, jnp.concatenate([k_band, pad], axis=0)) - bias
    m = jnp.max(s, axis=-1, keepdims=True)
    e = jnp.exp(s - m)
    denom = jnp.sum(e, axis=-1, keepdims=True)
    o = _dot(e.astype(BF16), jnp.concatenate([v_band, pad], axis=0)) * (1.0 / denom)
    half = GQ * CHUNK
    return [jnp.where(low, o[g * CHUNK:(g + 1) * CHUNK, :], o[half + g * CHUNK:half + (g + 1) * CHUNK, :])
            for g in range(GQ)]


def _shift_copies(ush, rows, n):
    for k in range(1, SUBLANES // n):
        ush[k, 0:rows * n, :] = ush[0, k * n:(k + rows) * n, :]


def _conv_rows(ush, r0, rows, w_dw_ref, b_dw, n):
    per_reg = SUBLANES // n
    groups = rows // per_reg
    acc = jnp.broadcast_to(b_dw, (groups, SUBLANES, LANES))
    for j in range(CONV_WIDTH):
        off = CONV_PAD - CONV_HIST + j
        k = off % per_reg
        taps = ush[k, pl.ds((r0 + off - k) * n, rows * n), :].reshape(groups, SUBLANES, LANES)
        acc = acc + w_dw_ref[j * SUBLANES:(j + 1) * SUBLANES, :][None] * taps
    return acc.reshape(rows * n, LANES)


def _finish(x, oa, conv, refs):
    (g_cn, b_cn, w_pw, b_pw, g_oa, g_oc, w_out, g_ffn, w_r2, w_rh, b_r) = refs
    mu = jnp.mean(conv, axis=-1, keepdims=True)
    cen = conv - mu
    var = jnp.mean(cen * cen, axis=-1, keepdims=True)
    ln = cen * lax.rsqrt(var + EPS) * g_cn[...] + b_cn[...]
    act = (ln * jax.nn.sigmoid(ln)).astype(BF16)
    c = _dot(act, w_pw[...]) + b_pw[...]
    half = oa.shape[-1]
    mixed = (_dot(_rms(oa, g_oa[...]).astype(BF16), w_out[0:half, :])
             + _dot(_rms(c, g_oc[...]).astype(BF16), w_out[half:, :]))
    x1 = x + mixed
    h = _rms(x1, g_ffn[...])
    h_hi = h.astype(BF16)
    h_lo = (h - h_hi.astype(F32)).astype(BF16)
    a = _dot(h_hi, w_r2[...])
    logits = a[:, 0:LANES] + a[:, LANES:] + _dot(h_lo, w_rh[...]) + b_r[...]
    return x1, h, logits


def _pack_rows(h):
    half = h.shape[-1] // 2
    hb = h.astype(BF16).astype(F32)
    lo = lax.bitcast_convert_type(hb[:, 0:half], U32)
    hi = lax.bitcast_convert_type(hb[:, half:], U32)
    return (lo >> 16) | (hi & jnp.uint32(0xFFFF0000))


def _unpack_rows(w, dtype):
    lo = lax.bitcast_convert_type(w << 16, F32).astype(dtype)
    hi = lax.bitcast_convert_type(w & jnp.uint32(0xFFFF0000), F32).astype(dtype)
    return lo, hi


def _route(logits, cnt_ref):
    rows = logits.shape[0]
    lt = jnp.transpose(logits)
    ninf = -jnp.inf
    crow = lax.broadcasted_iota(I32, (SUBLANES, rows), 0).astype(F32)
    lc = jnp.where(crow < N_GROUPS, lt[0:SUBLANES, :], ninf)
    mc = jnp.max(lc, axis=0, keepdims=True)
    grp = jnp.min(jnp.where(lc == mc, crow, float(SUBLANES)), axis=0, keepdims=True)
    g1 = 1.0 / jnp.sum(jnp.exp(lc - mc), axis=0, keepdims=True)
    erow = lax.broadcasted_iota(I32, (N_EXPERTS, rows), 0).astype(F32)
    lo = grp * EXPERTS_PER_GROUP
    lf = jnp.where((erow >= lo) & (erow < lo + EXPERTS_PER_GROUP), lt[FINE0:FINE0 + N_EXPERTS, :], ninf)
    t1 = jnp.max(lf, axis=0, keepdims=True)
    e1 = jnp.min(jnp.where(lf == t1, erow, float(N_EXPERTS)), axis=0, keepdims=True)
    lf2 = jnp.where(erow == e1, ninf, lf)
    t2 = jnp.max(lf2, axis=0, keepdims=True)
    e2 = jnp.min(jnp.where(lf2 == t2, erow, float(N_EXPERTS)), axis=0, keepdims=True)
    e2x = jnp.exp(t2 - t1)
    inv = 1.0 / (1.0 + e2x)
    w1 = g1 * inv
    w2 = g1 * (e2x * inv)
    oh1 = (erow == e1).astype(F32)
    oh2 = (erow == e2).astype(F32)
    oh = oh1 + oh2
    ri = lax.broadcasted_iota(I32, (rows, rows), 0)
    ci = lax.broadcasted_iota(I32, (rows, rows), 1)
    before = (ri < ci).astype(BF16)
    cnt = cnt_ref[...]
    tot = _dot(oh.astype(BF16), before) + cnt[:, 0:1]
    r1 = jnp.sum(oh1 * tot, axis=0, keepdims=True)
    r2 = jnp.sum(oh2 * tot, axis=0, keepdims=True)
    cnt_ref[...] = cnt + jnp.sum(oh, axis=1, keepdims=True)
    row = lax.broadcasted_iota(I32, (REC_W, rows), 0)
    rec_t = jnp.where(row == 0, e1, 0.0)
    rec_t = jnp.where(row == 1, e2, rec_t)
    rec_t = jnp.where(row == 2, w1, rec_t)
    rec_t = jnp.where(row == 3, w2, rec_t)
    rec_t = jnp.where(row == 4, r1, rec_t)
    rec_t = jnp.where(row == 5, r2, rec_t)
    full = jnp.concatenate([rec_t, jnp.zeros((LANES - REC_W, rows), F32)], axis=0)
    return jnp.transpose(full)[:, 0:REC_W], rec_t


def _prompt_mixer_kernel(x_ref, g_mix, w_in, bd, gain_qk, bias_ref, w_dw, b_dw,
                         g_cn, b_cn, w_pw, b_pw, g_oa, g_oc, w_out, g_ffn, w_r2, w_rh, b_r,
                         x1_ref, h_ref, rec_ref, rect_ref, cnt_out, nk_ref, nv_ref, nc_ref,
                         kx, vx, ush, oa_s, conv_s, cnt_s, *, tt, n):
    b = pl.program_id(0)
    t = pl.program_id(1)
    n_chunks = tt // CHUNK
    per_reg = SUBLANES // n

    @pl.when((b == 0) & (t == 0))
    def _():
        cnt_s[...] = jnp.zeros_like(cnt_s)

    @pl.when(t == 0)
    def _():
        kx[0:WINDOW, :] = jnp.zeros((WINDOW, LANES), BF16)
        vx[0:WINDOW, :] = jnp.zeros((WINDOW, LANES), BF16)
        ush[0, 0:CONV_PAD * n, :] = jnp.zeros((CONV_PAD * n, LANES), F32)

    x = x_ref[0]
    qkn, v, u = _project(x, g_mix[...], w_in, bd, gain_qk[...])
    k = qkn[:, Q_W:]
    kx[WINDOW:WINDOW + tt, :] = k.astype(BF16)
    vx[WINDOW:WINDOW + tt, :] = v.astype(BF16)
    _store_tile_rows(ush, (0,), u, row0=CONV_PAD)
    _shift_copies(ush, tt + CONV_PAD - per_reg, n)

    for c in range(n_chunks):
        variant = jnp.minimum(t * n_chunks + c, WINDOW // CHUNK)
        rows = slice(c * CHUNK, (c + 1) * CHUNK)
        q_groups = [qkn[rows, m * LANES:(m + 1) * LANES] for m in range(GQ)]
        o_groups = _attend_chunk(q_groups, kx[c * CHUNK:c * CHUNK + BAND, :], vx[c * CHUNK:c * CHUNK + BAND, :],
                                 bias_ref[variant])
        for m in range(GQ):
            oa_s[rows, m * LANES:(m + 1) * LANES] = o_groups[m]

    for rb in range(tt // CONV_BLOCK):
        conv_s[rb * CONV_BLOCK * n:(rb + 1) * CONV_BLOCK * n, :] = _conv_rows(
            ush, rb * CONV_BLOCK, CONV_BLOCK, w_dw, b_dw[...], n)

    x1, hh, logits = _finish(x, oa_s[...], _load_tile_rows(conv_s, (), tt, n),
                             (g_cn, b_cn, w_pw, b_pw, g_oa, g_oc, w_out, g_ffn, w_r2, w_rh, b_r))
    x1_ref[0] = x1
    _store_tile_rows(h_ref, (0,), _pack_rows(hh))
    rec_ref[0], rect_ref[0] = _route(logits, cnt_s)
    cnt_out[...] = cnt_s[...]

    nk_ref[0] = k[tt - WINDOW:, :]
    nv_ref[0] = v[tt - WINDOW:, :]
    nc_ref[0] = u[tt - CONV_HIST:, :]

    kx[0:WINDOW, :] = kx[tt:tt + WINDOW, :]
    vx[0:WINDOW, :] = vx[tt:tt + WINDOW, :]
    ush[0, 0:CONV_PAD * n, :] = ush[0, tt * n:(tt + CONV_PAD) * n, :]


def _sample_mixer_kernel(x_ref, ck_ref, cv_ref, sc_ref, g_mix, w_in, bd, gain_qk, bias_ref, w_dw, b_dw,
                         g_cn, b_cn, w_pw, b_pw, g_oa, g_oc, w_out, g_ffn, w_r2, w_rh, b_r,
                         x1_ref, h_ref, rec_ref, rect_ref, cnt_out, nk_ref, nv_ref, nc_ref,
                         ush, oa_s, conv_s, cnt_s, *, nb, n):
    i = pl.program_id(0)

    @pl.when(i == 0)
    def _():
        cnt_s[...] = jnp.zeros_like(cnt_s)

    rows_all = nb * CHUNK
    x = x_ref[...].reshape(rows_all, x_ref.shape[-1])
    qkn, v, u = _project(x, g_mix[...], w_in, bd, gain_qk[...])
    k = qkn[:, Q_W:]
    for j in range(nb):
        rows = slice(j * CHUNK, (j + 1) * CHUNK)
        ck = ck_ref[j]
        cv = cv_ref[j]
        k_band = jnp.concatenate([ck, k[rows, :]], axis=0)
        v_band = jnp.concatenate([cv, v[rows, :]], axis=0)
        q_groups = [qkn[rows, m * LANES:(m + 1) * LANES] for m in range(GQ)]
        o_groups = _attend_chunk(q_groups, k_band.astype(BF16), v_band.astype(BF16), bias_ref[0])
        for m in range(GQ):
            oa_s[rows, m * LANES:(m + 1) * LANES] = o_groups[m]
        ush[0, 0:CONV_PAD * n, :] = jnp.zeros((CONV_PAD * n, LANES), F32)
        _store_tile_rows(ush, (0,), sc_ref[j], row0=CONV_PAD - CONV_HIST)
        _store_tile_rows(ush, (0,), u[rows, :], row0=CONV_PAD)
        _shift_copies(ush, CHUNK + CONV_PAD - SUBLANES // n, n)
        for rb in range(CHUNK // CONV_BLOCK):
            r0 = rb * CONV_BLOCK
            conv_s[(j * CHUNK + r0) * n:(j * CHUNK + r0 + CONV_BLOCK) * n, :] = _conv_rows(
                ush, r0, CONV_BLOCK, w_dw, b_dw[...], n)
        nk_ref[j] = k_band[CHUNK:, :]
        nv_ref[j] = v_band[CHUNK:, :]
        nc_ref[j] = u[(j + 1) * CHUNK - CONV_HIST:(j + 1) * CHUNK, :]

    x1, hh, logits = _finish(x, oa_s[...], _load_tile_rows(conv_s, (), rows_all, n),
                             (g_cn, b_cn, w_pw, b_pw, g_oa, g_oc, w_out, g_ffn, w_r2, w_rh, b_r))
    x1_ref[...] = x1.reshape(x1_ref.shape)
    packed = _pack_rows(hh)
    for j in range(nb):
        _store_tile_rows(h_ref, (j,), packed[j * CHUNK:(j + 1) * CHUNK, :])
    rec, rec_t = _route(logits, cnt_s)
    rec_ref[...] = rec.reshape(rec_ref.shape)
    rect_ref[0] = rec_t
    cnt_out[...] = cnt_s[...]


def _full(shape):
    nd = len(shape)
    return pl.BlockSpec(shape, lambda *_: (0,) * nd)


def _prompt_mixer(x, wts, tt):
    B, T, D = x.shape
    c_conv = wts[9].shape[0]
    n_conv = c_conv // LANES
    assert SUBLANES % n_conv == 0 and CONV_BLOCK % (SUBLANES // n_conv) == 0
    xs_rows = D // 2 // LANES
    nt = T // tt
    tok = lambda rows, last: pl.BlockSpec((1, rows, last), lambda b, t: (b, t, 0))
    per_batch = lambda rows, last: pl.BlockSpec((1, rows, last), lambda b, t: (b, 0, 0))
    return pl.pallas_call(
        functools.partial(_prompt_mixer_kernel, tt=tt, n=n_conv),
        grid=(B, nt),
        in_specs=[tok(tt, D)] + [_full(w.shape) for w in wts],
        out_specs=[tok(tt, D), tok(tt * xs_rows, LANES), tok(tt, REC_W),
                   pl.BlockSpec((1, REC_W, tt), lambda b, t: (b, 0, t)), pl.BlockSpec((N_EXPERTS, LANES), lambda b, t: (0, 0)),
                   per_batch(WINDOW, KV_W), per_batch(WINDOW, KV_W), per_batch(CONV_HIST, c_conv)],
        out_shape=[
            jax.ShapeDtypeStruct((B, T, D), F32),
            jax.ShapeDtypeStruct((B, T * xs_rows, LANES), U32),
            jax.ShapeDtypeStruct((B, T, REC_W), F32),
            jax.ShapeDtypeStruct((B, REC_W, T), F32),
            jax.ShapeDtypeStruct((N_EXPERTS, LANES), F32),
            jax.ShapeDtypeStruct((B, WINDOW, KV_W), F32),
            jax.ShapeDtypeStruct((B, WINDOW, KV_W), F32),
            jax.ShapeDtypeStruct((B, CONV_HIST, c_conv), F32),
        ],
        scratch_shapes=[
            pltpu.VMEM((WINDOW + tt, LANES), BF16),
            pltpu.VMEM((WINDOW + tt, LANES), BF16),
            pltpu.VMEM((SUBLANES // n_conv, (CONV_PAD + tt) * n_conv, LANES), F32),
            pltpu.VMEM((tt, Q_W), F32),
            pltpu.VMEM((tt * n_conv, LANES), F32),
            pltpu.VMEM((N_EXPERTS, LANES), F32),
        ],
        compiler_params=pltpu.CompilerParams(
            dimension_semantics=("arbitrary", "arbitrary"), vmem_limit_bytes=VMEM_LIMIT),
        name="prompt_mixer",
    )(x, *wts)


def _sample_mixer(x, ck, cv, sc, wts, nb):
    B, T, D = x.shape
    assert T == CHUNK and B % nb == 0
    c_conv = wts[9].shape[0]
    n_conv = c_conv // LANES
    assert SUBLANES % n_conv == 0 and CONV_BLOCK % (SUBLANES // n_conv) == 0
    xs_rows = D // 2 // LANES
    blk3 = lambda rows, last: pl.BlockSpec((nb, rows, last), lambda i: (i, 0, 0))
    return pl.pallas_call(
        functools.partial(_sample_mixer_kernel, nb=nb, n=n_conv),
        grid=(B // nb,),
        in_specs=[blk3(T, D), blk3(WINDOW, KV_W), blk3(WINDOW, KV_W), blk3(CONV_HIST, c_conv)]
                 + [_full(w.shape) for w in wts],
        out_specs=[blk3(T, D), blk3(T * xs_rows, LANES), blk3(T, REC_W),
                   pl.BlockSpec((1, REC_W, nb * T), lambda i: (i, 0, 0)), pl.BlockSpec((N_EXPERTS, LANES), lambda i: (0, 0)),
                   blk3(WINDOW, KV_W), blk3(WINDOW, KV_W), blk3(CONV_HIST, c_conv)],
        out_shape=[
            jax.ShapeDtypeStruct((B, T, D), F32),
            jax.ShapeDtypeStruct((B, T * xs_rows, LANES), U32),
            jax.ShapeDtypeStruct((B, T, REC_W), F32),
            jax.ShapeDtypeStruct((B // nb, REC_W, nb * T), F32),
            jax.ShapeDtypeStruct((N_EXPERTS, LANES), F32),
            jax.ShapeDtypeStruct((B, WINDOW, KV_W), F32),
            jax.ShapeDtypeStruct((B, WINDOW, KV_W), F32),
            jax.ShapeDtypeStruct((B, CONV_HIST, c_conv), F32),
        ],
        scratch_shapes=[
            pltpu.VMEM((SUBLANES // n_conv, (CONV_PAD + CHUNK) * n_conv, LANES), F32),
            pltpu.VMEM((nb * CHUNK, Q_W), F32),
            pltpu.VMEM((nb * CHUNK * n_conv, LANES), F32),
            pltpu.VMEM((N_EXPERTS, LANES), F32),
        ],
        compiler_params=pltpu.CompilerParams(
            dimension_semantics=("arbitrary",), vmem_limit_bytes=VMEM_LIMIT),
        name="sample_mixer",
    )(x, ck, cv, sc, *wts)


def _dispatch_kernel(meta_ref, dest_ref, hp_ref, hs_ref, xs_ref, zbuf, sem, zsem, *, td, n, blk, n_blocks, ntp):
    i = pl.program_id(0)

    def issue_from(h_ref):
        def body(r8, carry):
            for uu in range(DMA_UNROLL):
                r = r8 * DMA_UNROLL + uu
                src = h_ref.at[pl.ds(pl.multiple_of(r * n, n), n)]
                for kk in range(TOP_K):
                    d = pl.multiple_of(dest_ref[kk, 0, 0, r], n)
                    pltpu.make_async_copy(src, xs_ref.at[pl.ds(d, n)], sem).start(priority=kk % 2)
            return carry
        lax.fori_loop(0, td // DMA_UNROLL, body, 0)

    @pl.when(i < ntp)
    def _():
        issue_from(hp_ref)

    @pl.when(i >= ntp)
    def _():
        issue_from(hs_ref)

    def pad_pass(act):
        def pad_expert(e, carry):
            start = meta_ref[e]
            head = meta_ref[N_EXPERTS + e]
            body = meta_ref[2 * N_EXPERTS + e]

            @pl.when(head == 1)
            def _():
                act(pltpu.make_async_copy(zbuf.at[pl.ds(0, n)], xs_ref.at[pl.ds(pl.multiple_of(start * n, n), n)], zsem))
            bit = blk // 2
            while bit >= 2:
                off = pl.multiple_of((start + head + (body // (2 * bit)) * (2 * bit)) * n, 2 * n)

                @pl.when((body // bit) % 2 == 1)
                def _(bit=bit, off=off):
                    act(pltpu.make_async_copy(zbuf.at[pl.ds(0, bit * n)], xs_ref.at[pl.ds(off, bit * n)], zsem))
                bit //= 2
            return carry

        lax.fori_loop(0, N_EXPERTS, pad_expert, 0)

        def pad_block(j, carry):
            act(pltpu.make_async_copy(zbuf, xs_ref.at[pl.ds(pl.multiple_of(j * (blk * n), blk * n), blk * n)], zsem))
            return carry

        lax.fori_loop(meta_ref[3 * N_EXPERTS], n_blocks, pad_block, 0)

    @pl.when(i == pl.num_programs(0) - 1)
    def _():
        zbuf[...] = jnp.zeros_like(zbuf)
        pad_pass(lambda cp: cp.start())
        pad_pass(lambda cp: cp.wait())

    for _ in range(TOP_K):
        pltpu.make_async_copy(hp_ref, xs_ref.at[pl.ds(0, td * n)], sem).wait()


def _dispatch(hp, hs, dest_rows, meta, n_blocks, blk, td):
    rows_p, rows_s = hp.shape[0], hs.shape[0]
    n_tok = dest_rows.shape[1]
    n = (rows_p + rows_s) // n_tok
    ntp, nts = rows_p // (td * n), rows_s // (td * n)
    dest3 = dest_rows.reshape(TOP_K, ntp + nts, 1, td)
    grid_spec = pltpu.PrefetchScalarGridSpec(
        num_scalar_prefetch=1,
        grid=(ntp + nts,),
        in_specs=[
            pl.BlockSpec((TOP_K, 1, 1, td), lambda i, m: (0, i, 0, 0), memory_space=pltpu.SMEM),
            pl.BlockSpec((td * n, LANES), lambda i, m: (jnp.minimum(i, ntp - 1), 0)),
            pl.BlockSpec((td * n, LANES), lambda i, m: (jnp.maximum(i - ntp, 0), 0)),
        ],
        out_specs=pl.BlockSpec(memory_space=pl.ANY),
        scratch_shapes=[pltpu.VMEM((blk * n, LANES), U32), pltpu.SemaphoreType.DMA(()), pltpu.SemaphoreType.DMA(())],
    )
    return pl.pallas_call(
        functools.partial(_dispatch_kernel, td=td, n=n, blk=blk, n_blocks=n_blocks, ntp=ntp),
        grid_spec=grid_spec,
        out_shape=jax.ShapeDtypeStruct((n_blocks * blk * n, LANES), U32),
        compiler_params=pltpu.CompilerParams(
            dimension_semantics=("arbitrary",), vmem_limit_bytes=VMEM_LIMIT),
        name="moe_dispatch",
    )(meta, dest3, hp, hs)


def _expert_kernel(block_e_ref, n_used_ref, x_ref, wg_ref, wu_ref, wd_ref, y_ref, wgb, wub, wdb, *, blk):
    i = pl.program_id(0)
    used = i < n_used_ref[0]
    new_expert = (i == 0) | (block_e_ref[i] != block_e_ref[jnp.maximum(i - 1, 0)])

    @pl.when(used & new_expert)
    def _():
        wgb[...] = wg_ref[0].astype(BF16)
        wub[...] = wu_ref[0].astype(BF16)
        wdb[...] = wd_ref[0].astype(BF16)

    @pl.when(used)
    def _():
        lo, hi = _unpack_rows(_load_tile_rows(x_ref, (), blk, x_ref.shape[0] // blk), BF16)
        half = lo.shape[-1]
        g = _dot(lo, wgb[0:half, :]) + _dot(hi, wgb[half:, :])
        u = _dot(lo, wub[0:half, :]) + _dot(hi, wub[half:, :])
        a = (g * jax.nn.sigmoid(g) * u).astype(BF16)
        _store_tile_rows(y_ref, (), _pack_rows(_dot(a, wdb[...])))

    @pl.when(jnp.logical_not(used))
    def _():
        y_ref[...] = jnp.zeros_like(y_ref)


def _experts(xs, block_e, n_used, wg, wu, wd, blk):
    n_blocks = block_e.shape[0]
    _, D, de = wg.shape
    xn = xs.shape[0] // (n_blocks * blk)
    yn = D // 2 // LANES
    grid_spec = pltpu.PrefetchScalarGridSpec(
        num_scalar_prefetch=2,
        grid=(n_blocks,),
        in_specs=[
            pl.BlockSpec((blk * xn, LANES), lambda i, be, nu: (i, 0)),
            pl.BlockSpec((1, D, de), lambda i, be, nu: (be[i], 0, 0)),
            pl.BlockSpec((1, D, de), lambda i, be, nu: (be[i], 0, 0)),
            pl.BlockSpec((1, de, D), lambda i, be, nu: (be[i], 0, 0)),
        ],
        out_specs=pl.BlockSpec((blk * yn, LANES), lambda i, be, nu: (i, 0)),
        scratch_shapes=[pltpu.VMEM((D, de), BF16), pltpu.VMEM((D, de), BF16), pltpu.VMEM((de, D), BF16)],
    )
    return pl.pallas_call(
        functools.partial(_expert_kernel, blk=blk),
        grid_spec=grid_spec,
        out_shape=jax.ShapeDtypeStruct((n_blocks * blk * yn, LANES), U32),
        compiler_params=pltpu.CompilerParams(
            dimension_semantics=("arbitrary",), vmem_limit_bytes=VMEM_LIMIT),
        name="moe_experts",
    )(block_e, n_used, xs, wg, wu, wd)


def _final_kernel(dest_cur, dest_nxt, x1_ref, rec_ref, p_ref, g_ple, w_gate, w_proj, ys_ref, y_ref,
                  buf_a, buf_b, sem, *, tf, n):
    i = pl.program_id(0)
    bufs = (buf_a, buf_b)

    def issue(dref, half, s):
        for r in range(tf):
            for kk in range(TOP_K):
                d = pl.multiple_of(dref[kk, 0, 0, half * tf + r], n)
                pltpu.make_async_copy(ys_ref.at[pl.ds(d, n)], bufs[s].at[kk, pl.ds(r * n, n)],
                                      sem.at[s]).start(priority=kk % 2)

    def drain(s):
        for kk in range(TOP_K):
            pltpu.make_async_copy(ys_ref.at[pl.ds(0, tf * n)], bufs[s].at[kk], sem.at[s]).wait()

    def compute(half, s):
        rows = slice(half * tf, (half + 1) * tf)
        rec = rec_ref[rows, :]
        y1 = jnp.concatenate(_unpack_rows(_load_tile_rows(bufs[s], (0,), tf, n), F32), axis=1)
        y2 = jnp.concatenate(_unpack_rows(_load_tile_rows(bufs[s], (1,), tf, n), F32), axis=1)
        x2 = x1_ref[rows, :] + (rec[:, 2:3] * y1 + rec[:, 3:4] * y2)
        gate = jax.nn.sigmoid(_dot(_rms(x2, g_ple[...]).astype(BF16), w_gate[...]))
        y_ref[rows, :] = x2 + gate * _dot(p_ref[rows, :].astype(BF16), w_proj[...])

    @pl.when(i == 0)
    def _():
        issue(dest_cur, 0, 0)

    drain(0)
    issue(dest_cur, 1, 1)
    compute(0, 0)
    drain(1)
    issue(dest_nxt, 0, 0)
    compute(1, 1)

    @pl.when(i == pl.num_programs(0) - 1)
    def _():
        drain(0)


def _final(x1, rec, p, dest_rows, ys, g_ple, w_gate, w_proj, tf):
    N, D = x1.shape
    n = D // 2 // LANES
    pair = 2 * tf
    assert N % pair == 0
    nt = N // pair
    dest3 = dest_rows.reshape(TOP_K, nt, 1, pair)
    smem_blk = lambda fn: pl.BlockSpec((TOP_K, 1, 1, pair), fn, memory_space=pltpu.SMEM)
    return pl.pallas_call(
        functools.partial(_final_kernel, tf=tf, n=n),
        grid=(nt,),
        in_specs=[
            smem_blk(lambda i: (0, i, 0, 0)),
            smem_blk(lambda i: (0, jnp.minimum(i + 1, nt - 1), 0, 0)),
            pl.BlockSpec((pair, D), lambda i: (i, 0)),
            pl.BlockSpec((pair, REC_W), lambda i: (i, 0)),
            pl.BlockSpec((pair, p.shape[-1]), lambda i: (i, 0)),
            _full(g_ple.shape), _full(w_gate.shape), _full(w_proj.shape),
            pl.BlockSpec(memory_space=pl.ANY),
        ],
        out_specs=pl.BlockSpec((pair, D), lambda i: (i, 0)),
        out_shape=jax.ShapeDtypeStruct((N, D), F32),
        scratch_shapes=[pltpu.VMEM((TOP_K, tf * n, LANES), U32), pltpu.VMEM((TOP_K, tf * n, LANES), U32),
                        pltpu.SemaphoreType.DMA((2,))],
        compiler_params=pltpu.CompilerParams(
            dimension_semantics=("arbitrary",), vmem_limit_bytes=VMEM_LIMIT),
        name="moe_combine_ple",
    )(dest3, dest3, x1, rec, p, g_ple, w_gate, w_proj, ys)


def _slots(rec, pstart, offset):
    e = rec[0:TOP_K, :].astype(I32)
    rank = rec[4:4 + TOP_K, :].astype(I32)
    onehot = e[None, :, :] == jnp.arange(N_EXPERTS, dtype=I32)[:, None, None]
    return rank + jnp.sum(jnp.where(onehot, (pstart + offset)[:, None, None], 0), axis=0)


def _bias_table(sinks):
    slopes = jnp.array([2.0 ** (-8.0 * (i + 1) / N_HEADS) for i in range(N_HEADS)], F32)
    qpos = WINDOW + jnp.arange(CHUNK, dtype=I32)
    col = jnp.arange(BAND_PAD, dtype=I32)
    dist = jnp.abs(qpos[:, None] - col[None, :]).astype(F32)
    core = slopes[:, None, None] * dist[None]
    sink = jnp.broadcast_to(-sinks.astype(F32)[:, None, None], core.shape)
    table = jnp.where(col == BAND, sink, jnp.where(col < BAND, core, MASKED))
    first_valid = jnp.array([WINDOW, WINDOW - CHUNK, 0], I32)
    table = jnp.where(col[None, None, None, :] < first_valid[:, None, None, None], MASKED, table[None])
    return table.reshape(3, N_HEADS * CHUNK, BAND_PAD)


def kernel(x_prompt, x_sample, p_prompt, p_sample, cache_k, cache_v, state_conv, g_mix, w_in, g_q, g_k, sinks,
           w_dw, b_dw, g_cn, b_cn, w_pw, b_pw, g_oa, g_oc, w_out, g_ffn, w_coarse, b_coarse, w_fine, b_fine,
           w_e_gate, w_e_up, w_e_down, g_ple, w_ple_gate, w_ple_proj):
    assert g_mix.shape[0] == 1
    l = 0
    B, T, D = x_prompt.shape
    DB, DS, _ = x_sample.shape
    assert cache_k.shape[2] == WINDOW and DS == CHUNK
    Np, Ns = B * T, DB * DS
    tt, td, tf = _tiles(T, Np, Ns)
    blk = EXPERT_BLOCK

    row = lambda a: a[l].reshape(1, -1)
    perm = jnp.concatenate([jnp.concatenate([jnp.arange(HEAD_DIM) + m * HEAD_DIM,
                                             jnp.arange(HEAD_DIM) + (GQ + m) * HEAD_DIM]) for m in range(GQ)])
    w_in_p = jnp.concatenate([w_in[l][:, perm], w_in[l][:, Q_W:]], axis=1).astype(BF16)
    w_out_p = jnp.concatenate([w_out[l][perm, :], w_out[l][Q_W:, :]], axis=0).astype(BF16)
    g_oa_p = g_oa[l][perm].reshape(1, -1)
    gain_qk = jnp.concatenate([jnp.tile(g_q[l] * (HEAD_DIM ** -0.5), N_HEADS), jnp.tile(g_k[l], N_KV_HEADS)]).reshape(1, -1)
    blk_id = jnp.arange(2 * LANES) // HEAD_DIM
    bd = jnp.where(blk_id[:, None] == blk_id[None, :], 1.0 / HEAD_DIM, 0.0).astype(BF16)
    w_r = jnp.concatenate(
        [w_coarse[l], jnp.zeros((D, FINE0 - N_GROUPS), F32),
         jnp.transpose(w_fine[l], (1, 0, 2)).reshape(D, N_EXPERTS),
         jnp.zeros((D, LANES - FINE0 - N_EXPERTS), F32)], axis=1)
    w_rh = w_r.astype(BF16)
    w_rl = (w_r - w_rh.astype(F32)).astype(BF16)
    w_r2 = jnp.concatenate([w_rh, w_rl], axis=1)
    b_r = jnp.concatenate(
        [b_coarse[l], jnp.zeros((FINE0 - N_GROUPS,), F32), b_fine[l].reshape(-1),
         jnp.zeros((LANES - FINE0 - N_EXPERTS,), F32)]).reshape(1, LANES)
    bias = _bias_table(sinks[l])
    n_conv = w_dw.shape[-1] // LANES
    tile_trm = lambda a: jnp.tile(a.reshape(-1, n_conv, LANES), (1, SUBLANES // n_conv, 1)).reshape(-1, LANES)
    w_dw_rep = tile_trm(w_dw[l])
    b_dw_rep = tile_trm(b_dw[l])

    def mixer_weights(bias_tbl):
        return (row(g_mix), w_in_p, bd, gain_qk, bias_tbl, w_dw_rep, b_dw_rep,
                row(g_cn), row(b_cn), w_pw[l].astype(BF16), row(b_pw), g_oa_p, row(g_oc),
                w_out_p, row(g_ffn), w_r2, w_rh, b_r)

    x1p, hp, recp, rectp, cntp, nkp, nvp, ncp = _prompt_mixer(x_prompt, mixer_weights(bias), tt)
    nb = next(n for n in (8, 4, 2, 1) if DB % n == 0)
    x1s, hs, recs, rects, cnts, nks, nvs, ncs = _sample_mixer(
        x_sample, cache_k[l].reshape(DB, WINDOW, KV_W), cache_v[l].reshape(DB, WINDOW, KV_W), state_conv[l],
        mixer_weights(bias[2:3]), nb)

    recp, recs = recp.reshape(Np, REC_W), recs.reshape(Ns, REC_W)
    rectp = jnp.transpose(rectp, (1, 0, 2)).reshape(REC_W, Np)
    rects = jnp.transpose(rects, (1, 0, 2)).reshape(REC_W, Ns)
    cnt_p = cntp[:, 0].astype(I32)
    cnt = cnt_p + cnts[:, 0].astype(I32)
    n_blocks = ((Np + Ns) * TOP_K + N_EXPERTS * (blk - 1)) // blk
    padded = (cnt + blk - 1) // blk * blk
    pend = jnp.cumsum(padded)
    pstart = pend - padded
    dest_p = _slots(rectp, pstart, jnp.zeros_like(cnt_p))
    dest_s = _slots(rects, pstart, cnt_p)
    block_e = jnp.minimum(
        jnp.sum((pend[None, :] <= (jnp.arange(n_blocks, dtype=I32) * blk)[:, None]).astype(I32), axis=1),
        N_EXPERTS - 1).astype(I32)
    n_used = (pend[-1:] // blk).astype(I32)
    pad_start = pstart + cnt
    pad_head = pad_start % 2
    meta = jnp.concatenate([pad_start, pad_head, padded - cnt - pad_head, n_used]).astype(I32)

    xn = D // 2 // LANES
    yn = D // 2 // LANES
    xs = _dispatch(hp.reshape(Np * xn, LANES), hs.reshape(Ns * xn, LANES),
                   jnp.concatenate([dest_p, dest_s], axis=1) * xn, meta, n_blocks, blk, td)
    ys = _experts(xs, block_e, n_used, w_e_gate[l], w_e_up[l], w_e_down[l], blk)

    gp = row(g_ple)
    w_gate = w_ple_gate[l].astype(BF16)
    w_proj = w_ple_proj[l].astype(BF16)
    yp = _final(x1p.reshape(Np, D), recp, p_prompt[l].reshape(Np, -1), dest_p * yn, ys, gp, w_gate, w_proj, tf)
    ysm = _final(x1s.reshape(Ns, D), recs, p_sample[l].reshape(Ns, -1), dest_s * yn, ys, gp, w_gate, w_proj, tf)

    kv5 = lambda a, nbat: a.reshape(1, nbat, WINDOW, N_KV_HEADS, HEAD_DIM)
    return (yp.reshape(B, T, D), ysm.reshape(DB, DS, D),
            kv5(nkp, B), kv5(nvp, B), ncp[None],
            kv5(nks, DB), kv5(nvs, DB), ncs[None])
```

```python
import functools

import jax
import jax.numpy as jnp
from jax import lax
from jax.experimental import pallas as pl
from jax.experimental.pallas import tpu as pltpu

CHUNK = 64
HEAD_DIM = 64
N_HEADS = 8
N_KV_HEADS = 2
GQ = N_HEADS // N_KV_HEADS
Q_W = N_HEADS * HEAD_DIM
KV_W = N_KV_HEADS * HEAD_DIM
WINDOW = 128
BAND = WINDOW + CHUNK
BAND_PAD = 256
CONV_WIDTH = 31
CONV_HIST = CONV_WIDTH - 1
CONV_PAD = 32
CONV_BLOCK = 32
SUBLANES = 8
LANES = 128
N_GROUPS = 4
EXPERTS_PER_GROUP = 8
N_EXPERTS = N_GROUPS * EXPERTS_PER_GROUP
TOP_K = 2
EXPERT_BLOCK = 1024
EPS = 1e-6
REC_W = 8
FINE0 = 8
MASKED = 1e30
DMA_UNROLL = 8

F32 = jnp.float32
BF16 = jnp.bfloat16
U32 = jnp.uint32
I32 = jnp.int32

VMEM_LIMIT = 56 * 1024 * 1024


def _tiles(n_prompt_seq, n_prompt, n_sample):
    tt = next((t for t in (1024, 512, 256) if n_prompt_seq % t == 0), n_prompt_seq)
    td = next(t for t in (2048, 1024, 512, 256, 128, 64) if n_prompt % t == 0 and n_sample % t == 0)
    tf = next(t for t in (512, 256, 128, 64, 32) if n_prompt % (2 * t) == 0 and n_sample % (2 * t) == 0)
    return tt, td, tf


def _rms(xf, g):
    return xf * lax.rsqrt(jnp.mean(xf * xf, axis=-1, keepdims=True) + EPS) * g


def _dot(a, b):
    return jnp.dot(a, b, preferred_element_type=F32)


def _dot_nt(a, b):
    return lax.dot_general(a, b, (((1,), (1,)), ((), ())), preferred_element_type=F32)


def _store_tile_rows(ref, lead, value, row0=0):
    rows, width = value.shape
    n = width // LANES
    for c in range(n):
        ref[lead + (pl.ds(row0 * n + c, rows, stride=n), slice(None))] = value[:, c * LANES:(c + 1) * LANES]


def _load_tile_rows(ref, lead, rows, n, row0=0):
    return jnp.concatenate([ref[lead + (pl.ds(row0 * n + c, rows, stride=n), slice(None))] for c in range(n)], axis=1)


def _project(x, g_mix, w_in_ref, bd_ref, gain_qk):
    xn = _rms(x, g_mix).astype(BF16)
    z = _dot(xn, w_in_ref[...])
    qk = z[:, 0:Q_W + KV_W]
    sq = (qk * qk).astype(BF16)
    bd = bd_ref[...]
    two = 2 * LANES
    ms = jnp.concatenate(
        [_dot(sq[:, 0:two], bd), _dot(sq[:, two:2 * two], bd), _dot(sq[:, 2 * two:], bd[0:KV_W, 0:KV_W])], axis=1)
    qkn = qk * lax.rsqrt(ms + EPS) * gain_qk
    v = z[:, Q_W + KV_W:Q_W + 2 * KV_W]
    c0 = Q_W + 2 * KV_W
    c_conv = (z.shape[1] - c0) // 2
    u = z[:, c0:c0 + c_conv] * jax.nn.sigmoid(z[:, c0 + c_conv:])
    return qkn, v, u


def _attend_chunk(q_groups, k_band, v_band, bias):
    lane = lax.broadcasted_iota(I32, (CHUNK, LANES), 1)
    low = lane < HEAD_DIM
    zero = jnp.zeros((CHUNK, LANES), F32)
    q_all = jnp.concatenate(
        [jnp.where(low, qg, zero) for qg in q_groups] + [jnp.where(low, zero, qg) for qg in q_groups],
        axis=0).astype(BF16)
    pad = jnp.zeros((BAND_PAD - BAND, LANES), BF16)
    s = _dot_nt(q_all, jnp.concatenate([k_band, pad], axis=0)) - bias
    m = jnp.max(s, axis=-1, keepdims=True)
    e = jnp.exp(s - m)
    denom = jnp.sum(e, axis=-1, keepdims=True)
    o = _dot(e.astype(BF16), jnp.concatenate([v_band, pad], axis=0)) * (1.0 / denom)
    half = GQ * CHUNK
    return [jnp.where(low, o[g * CHUNK:(g + 1) * CHUNK, :], o[half + g * CHUNK:half + (g + 1) * CHUNK, :])
            for g in range(GQ)]


def _shift_copies(ush, rows, n):
    for k in range(1, SUBLANES // n):
        ush[k, 0:rows * n, :] = ush[0, k * n:(k + rows) * n, :]


def _conv_rows(ush, r0, rows, w_dw_ref, b_dw, n):
    per_reg = SUBLANES // n
    groups = rows // per_reg
    acc = jnp.broadcast_to(b_dw, (groups, SUBLANES, LANES))
    for j in range(CONV_WIDTH):
        off = CONV_PAD - CONV_HIST + j
        k = off % per_reg
        taps = ush[k, pl.ds((r0 + off - k) * n, rows * n), :].reshape(groups, SUBLANES, LANES)
        acc = acc + w_dw_ref[j * SUBLANES:(j + 1) * SUBLANES, :][None] * taps
    return acc.reshape(rows * n, LANES)


def _finish(x, oa, conv, refs):
    (g_cn, b_cn, w_pw, b_pw, g_oa, g_oc, w_out, g_ffn, w_r2, w_rh, b_r) = refs
    mu = jnp.mean(conv, axis=-1, keepdims=True)
    cen = conv - mu
    var = jnp.mean(cen * cen, axis=-1, keepdims=True)
    ln = cen * lax.rsqrt(var + EPS) * g_cn[...] + b_cn[...]
    act = (ln * jax.nn.sigmoid(ln)).astype(BF16)
    c = _dot(act, w_pw[...]) + b_pw[...]
    half = oa.shape[-1]
    mixed = (_dot(_rms(oa, g_oa[...]).astype(BF16), w_out[0:half, :])
             + _dot(_rms(c, g_oc[...]).astype(BF16), w_out[half:, :]))
    x1 = x + mixed
    h = _rms(x1, g_ffn[...])
    h_hi = h.astype(BF16)
    h_lo = (h - h_hi.astype(F32)).astype(BF16)
    a = _dot(h_hi, w_r2[...])
    logits = a[:, 0:LANES] + a[:, LANES:] + _dot(h_lo, w_rh[...]) + b_r[...]
    return x1, h, logits


def _pack_rows(h):
    half = h.shape[-1] // 2
    hb = h.astype(BF16).astype(F32)
    lo = lax.bitcast_convert_type(hb[:, 0:half], U32)
    hi = lax.bitcast_convert_type(hb[:, half:], U32)
    return (lo >> 16) | (hi & jnp.uint32(0xFFFF0000))


def _unpack_rows(w, dtype):
    lo = lax.bitcast_convert_type(w << 16, F32).astype(dtype)
    hi = lax.bitcast_convert_type(w & jnp.uint32(0xFFFF0000), F32).astype(dtype)
    return lo, hi


def _route(logits, cnt_ref):
    rows = logits.shape[0]
    lt = jnp.transpose(logits)
    ninf = -jnp.inf
    crow = lax.broadcasted_iota(I32, (SUBLANES, rows), 0).astype(F32)
    lc = jnp.where(crow < N_GROUPS, lt[0:SUBLANES, :], ninf)
    mc = jnp.max(lc, axis=0, keepdims=True)
    grp = jnp.min(jnp.where(lc == mc, crow, float(SUBLANES)), axis=0, keepdims=True)
    g1 = 1.0 / jnp.sum(jnp.exp(lc - mc), axis=0, keepdims=True)
    erow = lax.broadcasted_iota(I32, (N_EXPERTS, rows), 0).astype(F32)
    lo = grp * EXPERTS_PER_GROUP
    lf = jnp.where((erow >= lo) & (erow < lo + EXPERTS_PER_GROUP), lt[FINE0:FINE0 + N_EXPERTS, :], ninf)
    t1 = jnp.max(lf, axis=0, keepdims=True)
    e1 = jnp.min(jnp.where(lf == t1, erow, float(N_EXPERTS)), axis=0, keepdims=True)
    lf2 = jnp.where(erow == e1, ninf, lf)
    t2 = jnp.max(lf2, axis=0, keepdims=True)
    e2 = jnp.min(jnp.where(lf2 == t2, erow, float(N_EXPERTS)), axis=0, keepdims=True)
    e2x = jnp.exp(t2 - t1)
    inv = 1.0 / (1.0 + e2x)
    w1 = g1 * inv
    w2 = g1 * (e2x * inv)
    oh1 = (erow == e1).astype(F32)
    oh2 = (erow == e2).astype(F32)
    oh = oh1 + oh2
    ri = lax.broadcasted_iota(I32, (rows, rows), 0)
    ci = lax.broadcasted_iota(I32, (rows, rows), 1)
    before = (ri < ci).astype(BF16)
    cnt = cnt_ref[...]
    tot = _dot(oh.astype(BF16), before) + cnt[:, 0:1]
    r1 = jnp.sum(oh1 * tot, axis=0, keepdims=True)
    r2 = jnp.sum(oh2 * tot, axis=0, keepdims=True)
    cnt_ref[...] = cnt + jnp.sum(oh, axis=1, keepdims=True)
    row = lax.broadcasted_iota(I32, (REC_W, rows), 0)
    rec_t = jnp.where(row == 0, e1, 0.0)
    rec_t = jnp.where(row == 1, e2, rec_t)
    rec_t = jnp.where(row == 2, w1, rec_t)
    rec_t = jnp.where(row == 3, w2, rec_t)
    rec_t = jnp.where(row == 4, r1, rec_t)
    rec_t = jnp.where(row == 5, r2, rec_t)
    full = jnp.concatenate([rec_t, jnp.zeros((LANES - REC_W, rows), F32)], axis=0)
    return jnp.transpose(full)[:, 0:REC_W], rec_t


def _prompt_mixer_kernel(x_ref, g_mix, w_in, bd, gain_qk, bias_ref, w_dw, b_dw,
                         g_cn, b_cn, w_pw, b_pw, g_oa, g_oc, w_out, g_ffn, w_r2, w_rh, b_r,
                         x1_ref, h_ref, rec_ref, rect_ref, cnt_out, nk_ref, nv_ref, nc_ref,
                         kx, vx, ush, oa_s, conv_s, cnt_s, *, tt, n):
    b = pl.program_id(0)
    t = pl.program_id(1)
    n_chunks = tt // CHUNK
    per_reg = SUBLANES // n

    @pl.when((b == 0) & (t == 0))
    def _():
        cnt_s[...] = jnp.zeros_like(cnt_s)

    @pl.when(t == 0)
    def _():
        kx[0:WINDOW, :] = jnp.zeros((WINDOW, LANES), BF16)
        vx[0:WINDOW, :] = jnp.zeros((WINDOW, LANES), BF16)
        ush[0, 0:CONV_PAD * n, :] = jnp.zeros((CONV_PAD * n, LANES), F32)

    x = x_ref[0]
    qkn, v, u = _project(x, g_mix[...], w_in, bd, gain_qk[...])
    k = qkn[:, Q_W:]
    kx[WINDOW:WINDOW + tt, :] = k.astype(BF16)
    vx[WINDOW:WINDOW + tt, :] = v.astype(BF16)
    _store_tile_rows(ush, (0,), u, row0=CONV_PAD)
    _shift_copies(ush, tt + CONV_PAD - per_reg, n)

    for c in range(n_chunks):
        variant = jnp.minimum(t * n_chunks + c, WINDOW // CHUNK)
        rows = slice(c * CHUNK, (c + 1) * CHUNK)
        q_groups = [qkn[rows, m * LANES:(m + 1) * LANES] for m in range(GQ)]
        o_groups = _attend_chunk(q_groups, kx[c * CHUNK:c * CHUNK + BAND, :], vx[c * CHUNK:c * CHUNK + BAND, :],
                                 bias_ref[variant])
        for m in range(GQ):
            oa_s[rows, m * LANES:(m + 1) * LANES] = o_groups[m]

    for rb in range(tt // CONV_BLOCK):
        conv_s[rb * CONV_BLOCK * n:(rb + 1) * CONV_BLOCK * n, :] = _conv_rows(
            ush, rb * CONV_BLOCK, CONV_BLOCK, w_dw, b_dw[...], n)

    x1, hh, logits = _finish(x, oa_s[...], _load_tile_rows(conv_s, (), tt, n),
                             (g_cn, b_cn, w_pw, b_pw, g_oa, g_oc, w_out, g_ffn, w_r2, w_rh, b_r))
    x1_ref[0] = x1
    _store_tile_rows(h_ref, (0,), _pack_rows(hh))
    rec_ref[0], rect_ref[0] = _route(logits, cnt_s)
    cnt_out[...] = cnt_s[...]

    nk_ref[0] = k[tt - WINDOW:, :]
    nv_ref[0] = v[tt - WINDOW:, :]
    nc_ref[0] = u[tt - CONV_HIST:, :]

    kx[0:WINDOW, :] = kx[tt:tt + WINDOW, :]
    vx[0:WINDOW, :] = vx[tt:tt + WINDOW, :]
    ush[0, 0:CONV_PAD * n, :] = ush[0, tt * n:(tt + CONV_PAD) * n, :]


def _sample_mixer_kernel(x_ref, ck_ref, cv_ref, sc_ref, g_mix, w_in, bd, gain_qk, bias_ref, w_dw, b_dw,
                         g_cn, b_cn, w_pw, b_pw, g_oa, g_oc, w_out, g_ffn, w_r2, w_rh, b_r,
                         x1_ref, h_ref, rec_ref, rect_ref, cnt_out, nk_ref, nv_ref, nc_ref,
                         ush, oa_s, conv_s, cnt_s, *, nb, n):
    i = pl.program_id(0)

    @pl.when(i == 0)
    def _():
        cnt_s[...] = jnp.zeros_like(cnt_s)

    rows_all = nb * CHUNK
    x = x_ref[...].reshape(rows_all, x_ref.shape[-1])
    qkn, v, u = _project(x, g_mix[...], w_in, bd, gain_qk[...])
    k = qkn[:, Q_W:]
    for j in range(nb):
        rows = slice(j * CHUNK, (j + 1) * CHUNK)
        ck = ck_ref[j]
        cv = cv_ref[j]
        k_band = jnp.concatenate([ck, k[rows, :]], axis=0)
        v_band = jnp.concatenate([cv, v[rows, :]], axis=0)
        q_groups = [qkn[rows, m * LANES:(m + 1) * LANES] for m in range(GQ)]
        o_groups = _attend_chunk(q_groups, k_band.astype(BF16), v_band.astype(BF16), bias_ref[0])
        for m in range(GQ):
            oa_s[rows, m * LANES:(m + 1) * LANES] = o_groups[m]
        ush[0, 0:CONV_PAD * n, :] = jnp.zeros((CONV_PAD * n, LANES), F32)
        _store_tile_rows(ush, (0,), sc_ref[j], row0=CONV_PAD - CONV_HIST)
        _store_tile_rows(ush, (0,), u[rows, :], row0=CONV_PAD)
        _shift_copies(ush, CHUNK + CONV_PAD - SUBLANES // n, n)
        for rb in range(CHUNK // CONV_BLOCK):
            r0 = rb * CONV_BLOCK
            conv_s[(j * CHUNK + r0) * n:(j * CHUNK + r0 + CONV_BLOCK) * n, :] = _conv_rows(
                ush, r0, CONV_BLOCK, w_dw, b_dw[...], n)
        nk_ref[j] = k_band[CHUNK:, :]
        nv_ref[j] = v_band[CHUNK:, :]
        nc_ref[j] = u[(j + 1) * CHUNK - CONV_HIST:(j + 1) * CHUNK, :]

    x1, hh, logits = _finish(x, oa_s[...], _load_tile_rows(conv_s, (), rows_all, n),
                             (g_cn, b_cn, w_pw, b_pw, g_oa, g_oc, w_out, g_ffn, w_r2, w_rh, b_r))
    x1_ref[...] = x1.reshape(x1_ref.shape)
    packed = _pack_rows(hh)
    for j in range(nb):
        _store_tile_rows(h_ref, (j,), packed[j * CHUNK:(j + 1) * CHUNK, :])
    rec, rec_t = _route(logits, cnt_s)
    rec_ref[...] = rec.reshape(rec_ref.shape)
    rect_ref[0] = rec_t
    cnt_out[...] = cnt_s[...]


def _full(shape):
    nd = len(shape)
    return pl.BlockSpec(shape, lambda *_: (0,) * nd)


def _prompt_mixer(x, wts, tt):
    B, T, D = x.shape
    c_conv = wts[9].shape[0]
    n_conv = c_conv // LANES
    assert SUBLANES % n_conv == 0 and CONV_BLOCK % (SUBLANES // n_conv) == 0
    xs_rows = D // 2 // LANES
    nt = T // tt
    tok = lambda rows, last: pl.BlockSpec((1, rows, last), lambda b, t: (b, t, 0))
    per_batch = lambda rows, last: pl.BlockSpec((1, rows, last), lambda b, t: (b, 0, 0))
    return pl.pallas_call(
        functools.partial(_prompt_mixer_kernel, tt=tt, n=n_conv),
        grid=(B, nt),
        in_specs=[tok(tt, D)] + [_full(w.shape) for w in wts],
        out_specs=[tok(tt, D), tok(tt * xs_rows, LANES), tok(tt, REC_W),
                   pl.BlockSpec((1, REC_W, tt), lambda b, t: (b, 0, t)), pl.BlockSpec((N_EXPERTS, LANES), lambda b, t: (0, 0)),
                   per_batch(WINDOW, KV_W), per_batch(WINDOW, KV_W), per_batch(CONV_HIST, c_conv)],
        out_shape=[
            jax.ShapeDtypeStruct((B, T, D), F32),
            jax.ShapeDtypeStruct((B, T * xs_rows, LANES), U32),
            jax.ShapeDtypeStruct((B, T, REC_W), F32),
            jax.ShapeDtypeStruct((B, REC_W, T), F32),
            jax.ShapeDtypeStruct((N_EXPERTS, LANES), F32),
            jax.ShapeDtypeStruct((B, WINDOW, KV_W), F32),
            jax.ShapeDtypeStruct((B, WINDOW, KV_W), F32),
            jax.ShapeDtypeStruct((B, CONV_HIST, c_conv), F32),
        ],
        scratch_shapes=[
            pltpu.VMEM((WINDOW + tt, LANES), BF16),
            pltpu.VMEM((WINDOW + tt, LANES), BF16),
            pltpu.VMEM((SUBLANES // n_conv, (CONV_PAD + tt) * n_conv, LANES), F32),
            pltpu.VMEM((tt, Q_W), F32),
            pltpu.VMEM((tt * n_conv, LANES), F32),
            pltpu.VMEM((N_EXPERTS, LANES), F32),
        ],
        compiler_params=pltpu.CompilerParams(
            dimension_semantics=("arbitrary", "arbitrary"), vmem_limit_bytes=VMEM_LIMIT),
        name="prompt_mixer",
    )(x, *wts)


def _sample_mixer(x, ck, cv, sc, wts, nb):
    B, T, D = x.shape
    assert T == CHUNK and B % nb == 0
    c_conv = wts[9].shape[0]
    n_conv = c_conv // LANES
    assert SUBLANES % n_conv == 0 and CONV_BLOCK % (SUBLANES // n_conv) == 0
    xs_rows = D // 2 // LANES
    blk3 = lambda rows, last: pl.BlockSpec((nb, rows, last), lambda i: (i, 0, 0))
    return pl.pallas_call(
        functools.partial(_sample_mixer_kernel, nb=nb, n=n_conv),
        grid=(B // nb,),
        in_specs=[blk3(T, D), blk3(WINDOW, KV_W), blk3(WINDOW, KV_W), blk3(CONV_HIST, c_conv)]
                 + [_full(w.shape) for w in wts],
        out_specs=[blk3(T, D), blk3(T * xs_rows, LANES), blk3(T, REC_W),
                   pl.BlockSpec((1, REC_W, nb * T), lambda i: (i, 0, 0)), pl.BlockSpec((N_EXPERTS, LANES), lambda i: (0, 0)),
                   blk3(WINDOW, KV_W), blk3(WINDOW, KV_W), blk3(CONV_HIST, c_conv)],
        out_shape=[
            jax.ShapeDtypeStruct((B, T, D), F32),
            jax.ShapeDtypeStruct((B, T * xs_rows, LANES), U32),
            jax.ShapeDtypeStruct((B, T, REC_W), F32),
            jax.ShapeDtypeStruct((B // nb, REC_W, nb * T), F32),
            jax.ShapeDtypeStruct((N_EXPERTS, LANES), F32),
            jax.ShapeDtypeStruct((B, WINDOW, KV_W), F32),
            jax.ShapeDtypeStruct((B, WINDOW, KV_W), F32),
            jax.ShapeDtypeStruct((B, CONV_HIST, c_conv), F32),
        ],
        scratch_shapes=[
            pltpu.VMEM((SUBLANES // n_conv, (CONV_PAD + CHUNK) * n_conv, LANES), F32),
            pltpu.VMEM((nb * CHUNK, Q_W), F32),
            pltpu.VMEM((nb * CHUNK * n_conv, LANES), F32),
            pltpu.VMEM((N_EXPERTS, LANES), F32),
        ],
        compiler_params=pltpu.CompilerParams(
            dimension_semantics=("arbitrary",), vmem_limit_bytes=VMEM_LIMIT),
        name="sample_mixer",
    )(x, ck, cv, sc, *wts)


def _dispatch_kernel(meta_ref, dest_ref, hp_ref, hs_ref, xs_ref, zbuf, sem, zsem, *, td, n, blk, n_blocks, ntp):
    i = pl.program_id(0)

    def issue_from(h_ref):
        def body(r8, carry):
            for uu in range(DMA_UNROLL):
                r = r8 * DMA_UNROLL + uu
                src = h_ref.at[pl.ds(pl.multiple_of(r * n, n), n)]
                for kk in range(TOP_K):
                    d = pl.multiple_of(dest_ref[kk, 0, 0, r], n)
                    pltpu.make_async_copy(src, xs_ref.at[pl.ds(d, n)], sem).start(priority=kk % 2)
            return carry
        lax.fori_loop(0, td // DMA_UNROLL, body, 0)

    @pl.when(i < ntp)
    def _():
        issue_from(hp_ref)

    @pl.when(i >= ntp)
    def _():
        issue_from(hs_ref)

    def pad_pass(act):
        def pad_expert(e, carry):
            start = meta_ref[e]
            head = meta_ref[N_EXPERTS + e]
            body = meta_ref[2 * N_EXPERTS + e]

            @pl.when(head == 1)
            def _():
                act(pltpu.make_async_copy(zbuf.at[pl.ds(0, n)], xs_ref.at[pl.ds(pl.multiple_of(start * n, n), n)], zsem))
            bit = blk // 2
            while bit >= 2:
                off = pl.multiple_of((start + head + (body // (2 * bit)) * (2 * bit)) * n, 2 * n)

                @pl.when((body // bit) % 2 == 1)
                def _(bit=bit, off=off):
                    act(pltpu.make_async_copy(zbuf.at[pl.ds(0, bit * n)], xs_ref.at[pl.ds(off, bit * n)], zsem))
                bit //= 2
            return carry

        lax.fori_loop(0, N_EXPERTS, pad_expert, 0)

        def pad_block(j, carry):
            act(pltpu.make_async_copy(zbuf, xs_ref.at[pl.ds(pl.multiple_of(j * (blk * n), blk * n), blk * n)], zsem))
            return carry

        lax.fori_loop(meta_ref[3 * N_EXPERTS], n_blocks, pad_block, 0)

    @pl.when(i == pl.num_programs(0) - 1)
    def _():
        zbuf[...] = jnp.zeros_like(zbuf)
        pad_pass(lambda cp: cp.start())
        pad_pass(lambda cp: cp.wait())

    for _ in range(TOP_K):
        pltpu.make_async_copy(hp_ref, xs_ref.at[pl.ds(0, td * n)], sem).wait()


def _dispatch(hp, hs, dest_rows, meta, n_blocks, blk, td):
    rows_p, rows_s = hp.shape[0], hs.shape[0]
    n_tok = dest_rows.shape[1]
    n = (rows_p + rows_s) // n_tok
    ntp, nts = rows_p // (td * n), rows_s // (td * n)
    dest3 = dest_rows.reshape(TOP_K, ntp + nts, 1, td)
    grid_spec = pltpu.PrefetchScalarGridSpec(
        num_scalar_prefetch=1,
        grid=(ntp + nts,),
        in_specs=[
            pl.BlockSpec((TOP_K, 1, 1, td), lambda i, m: (0, i, 0, 0), memory_space=pltpu.SMEM),
            pl.BlockSpec((td * n, LANES), lambda i, m: (jnp.minimum(i, ntp - 1), 0)),
            pl.BlockSpec((td * n, LANES), lambda i, m: (jnp.maximum(i - ntp, 0), 0)),
        ],
        out_specs=pl.BlockSpec(memory_space=pl.ANY),
        scratch_shapes=[pltpu.VMEM((blk * n, LANES), U32), pltpu.SemaphoreType.DMA(()), pltpu.SemaphoreType.DMA(())],
    )
    return pl.pallas_call(
        functools.partial(_dispatch_kernel, td=td, n=n, blk=blk, n_blocks=n_blocks, ntp=ntp),
        grid_spec=grid_spec,
        out_shape=jax.ShapeDtypeStruct((n_blocks * blk * n, LANES), U32),
        compiler_params=pltpu.CompilerParams(
            dimension_semantics=("arbitrary",), vmem_limit_bytes=VMEM_LIMIT),
        name="moe_dispatch",
    )(meta, dest3, hp, hs)


def _expert_kernel(block_e_ref, n_used_ref, x_ref, wg_ref, wu_ref, wd_ref, y_ref, wgb, wub, wdb, *, blk):
    i = pl.program_id(0)
    used = i < n_used_ref[0]
    new_expert = (i == 0) | (block_e_ref[i] != block_e_ref[jnp.maximum(i - 1, 0)])

    @pl.when(used & new_expert)
    def _():
        wgb[...] = wg_ref[0].astype(BF16)
        wub[...] = wu_ref[0].astype(BF16)
        wdb[...] = wd_ref[0].astype(BF16)

    @pl.when(used)
    def _():
        lo, hi = _unpack_rows(_load_tile_rows(x_ref, (), blk, x_ref.shape[0] // blk), BF16)
        half = lo.shape[-1]
        g = _dot(lo, wgb[0:half, :]) + _dot(hi, wgb[half:, :])
        u = _dot(lo, wub[0:half, :]) + _dot(hi, wub[half:, :])
        a = (g * jax.nn.sigmoid(g) * u).astype(BF16)
        _store_tile_rows(y_ref, (), _pack_rows(_dot(a, wdb[...])))

    @pl.when(jnp.logical_not(used))
    def _():
        y_ref[...] = jnp.zeros_like(y_ref)


def _experts(xs, block_e, n_used, wg, wu, wd, blk):
    n_blocks = block_e.shape[0]
    _, D, de = wg.shape
    xn = xs.shape[0] // (n_blocks * blk)
    yn = D // 2 // LANES
    grid_spec = pltpu.PrefetchScalarGridSpec(
        num_scalar_prefetch=2,
        grid=(n_blocks,),
        in_specs=[
            pl.BlockSpec((blk * xn, LANES), lambda i, be, nu: (i, 0)),
            pl.BlockSpec((1, D, de), lambda i, be, nu: (be[i], 0, 0)),
            pl.BlockSpec((1, D, de), lambda i, be, nu: (be[i], 0, 0)),
            pl.BlockSpec((1, de, D), lambda i, be, nu: (be[i], 0, 0)),
        ],
        out_specs=pl.BlockSpec((blk * yn, LANES), lambda i, be, nu: (i, 0)),
        scratch_shapes=[pltpu.VMEM((D, de), BF16), pltpu.VMEM((D, de), BF16), pltpu.VMEM((de, D), BF16)],
    )
    return pl.pallas_call(
        functools.partial(_expert_kernel, blk=blk),
        grid_spec=grid_spec,
        out_shape=jax.ShapeDtypeStruct((n_blocks * blk * yn, LANES), U32),
        compiler_params=pltpu.CompilerParams(
            dimension_semantics=("arbitrary",), vmem_limit_bytes=VMEM_LIMIT),
        name="moe_experts",
    )(block_e, n_used, xs, wg, wu, wd)


def _final_kernel(dest_cur, dest_nxt, x1_ref, rec_ref, p_ref, g_ple, w_gate, w_proj, ys_ref, y_ref,
                  buf_a, buf_b, sem, *, tf, n):
    i = pl.program_id(0)
    bufs = (buf_a, buf_b)

    def issue(dref, half, s):
        for r in range(tf):
            for kk in range(TOP_K):
                d = pl.multiple_of(dref[kk, 0, 0, half * tf + r], n)
                pltpu.make_async_copy(ys_ref.at[pl.ds(d, n)], bufs[s].at[kk, pl.ds(r * n, n)],
                                      sem.at[s]).start(priority=kk % 2)

    def drain(s):
        for kk in range(TOP_K):
            pltpu.make_async_copy(ys_ref.at[pl.ds(0, tf * n)], bufs[s].at[kk], sem.at[s]).wait()

    def compute(half, s):
        rows = slice(half * tf, (half + 1) * tf)
        rec = rec_ref[rows, :]
        y1 = jnp.concatenate(_unpack_rows(_load_tile_rows(bufs[s], (0,), tf, n), F32), axis=1)
        y2 = jnp.concatenate(_unpack_rows(_load_tile_rows(bufs[s], (1,), tf, n), F32), axis=1)
        x2 = x1_ref[rows, :] + (rec[:, 2:3] * y1 + rec[:, 3:4] * y2)
        gate = jax.nn.sigmoid(_dot(_rms(x2, g_ple[...]).astype(BF16), w_gate[...]))
        y_ref[rows, :] = x2 + gate * _dot(p_ref[rows, :].astype(BF16), w_proj[...])

    @pl.when(i == 0)
    def _():
        issue(dest_cur, 0, 0)

    drain(0)
    issue(dest_cur, 1, 1)
    compute(0, 0)
    drain(1)
    issue(dest_nxt, 0, 0)
    compute(1, 1)

    @pl.when(i == pl.num_programs(0) - 1)
    def _():
        drain(0)


def _final(x1, rec, p, dest_rows, ys, g_ple, w_gate, w_proj, tf):
    N, D = x1.shape
    n = D // 2 // LANES
    pair = 2 * tf
    assert N % pair == 0
    nt = N // pair
    dest3 = dest_rows.reshape(TOP_K, nt, 1, pair)
    smem_blk = lambda fn: pl.BlockSpec((TOP_K, 1, 1, pair), fn, memory_space=pltpu.SMEM)
    return pl.pallas_call(
        functools.partial(_final_kernel, tf=tf, n=n),
        grid=(nt,),
        in_specs=[
            smem_blk(lambda i: (0, i, 0, 0)),
            smem_blk(lambda i: (0, jnp.minimum(i + 1, nt - 1), 0, 0)),
            pl.BlockSpec((pair, D), lambda i: (i, 0)),
            pl.BlockSpec((pair, REC_W), lambda i: (i, 0)),
            pl.BlockSpec((pair, p.shape[-1]), lambda i: (i, 0)),
            _full(g_ple.shape), _full(w_gate.shape), _full(w_proj.shape),
            pl.BlockSpec(memory_space=pl.ANY),
        ],
        out_specs=pl.BlockSpec((pair, D), lambda i: (i, 0)),
        out_shape=jax.ShapeDtypeStruct((N, D), F32),
        scratch_shapes=[pltpu.VMEM((TOP_K, tf * n, LANES), U32), pltpu.VMEM((TOP_K, tf * n, LANES), U32),
                        pltpu.SemaphoreType.DMA((2,))],
        compiler_params=pltpu.CompilerParams(
            dimension_semantics=("arbitrary",), vmem_limit_bytes=VMEM_LIMIT),
        name="moe_combine_ple",
    )(dest3, dest3, x1, rec, p, g_ple, w_gate, w_proj, ys)


def _slots(rec, pstart, offset):
    e = rec[0:TOP_K, :].astype(I32)
    rank = rec[4:4 + TOP_K, :].astype(I32)
    onehot = e[None, :, :] == jnp.arange(N_EXPERTS, dtype=I32)[:, None, None]
    return rank + jnp.sum(jnp.where(onehot, (pstart + offset)[:, None, None], 0), axis=0)


def _bias_table(sinks):
    slopes = jnp.array([2.0 ** (-8.0 * (i + 1) / N_HEADS) for i in range(N_HEADS)], F32)
    qpos = WINDOW + jnp.arange(CHUNK, dtype=I32)
    col = jnp.arange(BAND_PAD, dtype=I32)
    dist = jnp.abs(qpos[:, None] - col[None, :]).astype(F32)
    core = slopes[:, None, None] * dist[None]
    sink = jnp.broadcast_to(-sinks.astype(F32)[:, None, None], core.shape)
    table = jnp.where(col == BAND, sink, jnp.where(col < BAND, core, MASKED))
    first_valid = jnp.array([WINDOW, WINDOW - CHUNK, 0], I32)
    table = jnp.where(col[None, None, None, :] < first_valid[:, None, None, None], MASKED, table[None])
    return table.reshape(3, N_HEADS * CHUNK, BAND_PAD)


def kernel(x_prompt, x_sample, p_prompt, p_sample, cache_k, cache_v, state_conv, g_mix, w_in, g_q, g_k, sinks,
           w_dw, b_dw, g_cn, b_cn, w_pw, b_pw, g_oa, g_oc, w_out, g_ffn, w_coarse, b_coarse, w_fine, b_fine,
           w_e_gate, w_e_up, w_e_down, g_ple, w_ple_gate, w_ple_proj):
    assert g_mix.shape[0] == 1
    l = 0
    B, T, D = x_prompt.shape
    DB, DS, _ = x_sample.shape
    assert cache_k.shape[2] == WINDOW and DS == CHUNK
    Np, Ns = B * T, DB * DS
    tt, td, tf = _tiles(T, Np, Ns)
    blk = EXPERT_BLOCK

    row = lambda a: a[l].reshape(1, -1)
    perm = jnp.concatenate([jnp.concatenate([jnp.arange(HEAD_DIM) + m * HEAD_DIM,
                                             jnp.arange(HEAD_DIM) + (GQ + m) * HEAD_DIM]) for m in range(GQ)])
    w_in_p = jnp.concatenate([w_in[l][:, perm], w_in[l][:, Q_W:]], axis=1).astype(BF16)
    w_out_p = jnp.concatenate([w_out[l][perm, :], w_out[l][Q_W:, :]], axis=0).astype(BF16)
    g_oa_p = g_oa[l][perm].reshape(1, -1)
    gain_qk = jnp.concatenate([jnp.tile(g_q[l] * (HEAD_DIM ** -0.5), N_HEADS), jnp.tile(g_k[l], N_KV_HEADS)]).reshape(1, -1)
    blk_id = jnp.arange(2 * LANES) // HEAD_DIM
    bd = jnp.where(blk_id[:, None] == blk_id[None, :], 1.0 / HEAD_DIM, 0.0).astype(BF16)
    w_r = jnp.concatenate(
        [w_coarse[l], jnp.zeros((D, FINE0 - N_GROUPS), F32),
         jnp.transpose(w_fine[l], (1, 0, 2)).reshape(D, N_EXPERTS),
         jnp.zeros((D, LANES - FINE0 - N_EXPERTS), F32)], axis=1)
    w_rh = w_r.astype(BF16)
    w_rl = (w_r - w_rh.astype(F32)).astype(BF16)
    w_r2 = jnp.concatenate([w_rh, w_rl], axis=1)
    b_r = jnp.concatenate(
        [b_coarse[l], jnp.zeros((FINE0 - N_GROUPS,), F32), b_fine[l].reshape(-1),
         jnp.zeros((LANES - FINE0 - N_EXPERTS,), F32)]).reshape(1, LANES)
    bias = _bias_table(sinks[l])
    n_conv = w_dw.shape[-1] // LANES
    tile_trm = lambda a: jnp.tile(a.reshape(-1, n_conv, LANES), (1, SUBLANES // n_conv, 1)).reshape(-1, LANES)
    w_dw_rep = tile_trm(w_dw[l])
    b_dw_rep = tile_trm(b_dw[l])

    def mixer_weights(bias_tbl):
        return (row(g_mix), w_in_p, bd, gain_qk, bias_tbl, w_dw_rep, b_dw_rep,
                row(g_cn), row(b_cn), w_pw[l].astype(BF16), row(b_pw), g_oa_p, row(g_oc),
                w_out_p, row(g_ffn), w_r2, w_rh, b_r)

    x1p, hp, recp, rectp, cntp, nkp, nvp, ncp = _prompt_mixer(x_prompt, mixer_weights(bias), tt)
    nb = next(n for n in (8, 4, 2, 1) if DB % n == 0)
    x1s, hs, recs, rects, cnts, nks, nvs, ncs = _sample_mixer(
        x_sample, cache_k[l].reshape(DB, WINDOW, KV_W), cache_v[l].reshape(DB, WINDOW, KV_W), state_conv[l],
        mixer_weights(bias[2:3]), nb)

    recp, recs = recp.reshape(Np, REC_W), recs.reshape(Ns, REC_W)
    rectp = jnp.transpose(rectp, (1, 0, 2)).reshape(REC_W, Np)
    rects = jnp.transpose(rects, (1, 0, 2)).reshape(REC_W, Ns)
    cnt_p = cntp[:, 0].astype(I32)
    cnt = cnt_p + cnts[:, 0].astype(I32)
    n_blocks = ((Np + Ns) * TOP_K + N_EXPERTS * (blk - 1)) // blk
    padded = (cnt + blk - 1) // blk * blk
    pend = jnp.cumsum(padded)
    pstart = pend - padded
    dest_p = _slots(rectp, pstart, jnp.zeros_like(cnt_p))
    dest_s = _slots(rects, pstart, cnt_p)
    block_e = jnp.minimum(
        jnp.sum((pend[None, :] <= (jnp.arange(n_blocks, dtype=I32) * blk)[:, None]).astype(I32), axis=1),
        N_EXPERTS - 1).astype(I32)
    n_used = (pend[-1:] // blk).astype(I32)
    pad_start = pstart + cnt
    pad_head = pad_start % 2
    meta = jnp.concatenate([pad_start, pad_head, padded - cnt - pad_head, n_used]).astype(I32)

    xn = D // 2 // LANES
    yn = D // 2 // LANES
    xs = _dispatch(hp.reshape(Np * xn, LANES), hs.reshape(Ns * xn, LANES),
                   jnp.concatenate([dest_p, dest_s], axis=1) * xn, meta, n_blocks, blk, td)
    ys = _experts(xs, block_e, n_used, w_e_gate[l], w_e_up[l], w_e_down[l], blk)

    gp = row(g_ple)
    w_gate = w_ple_gate[l].astype(BF16)
    w_proj = w_ple_proj[l].astype(BF16)
    yp = _final(x1p.reshape(Np, D), recp, p_prompt[l].reshape(Np, -1), dest_p * yn, ys, gp, w_gate, w_proj, tf)
    ysm = _final(x1s.reshape(Ns, D), recs, p_sample[l].reshape(Ns, -1), dest_s * yn, ys, gp, w_gate, w_proj, tf)

    kv5 = lambda a, nbat: a.reshape(1, nbat, WINDOW, N_KV_HEADS, HEAD_DIM)
    return (yp.reshape(B, T, D), ysm.reshape(DB, DS, D),
            kv5(nkp, B), kv5(nvp, B), ncp[None],
            kv5(nks, DB), kv5(nvs, DB), ncs[None])
```
